```python
import math
import jax, jax.numpy as jnp
from jax import lax
import numpy as np

D_MODEL = 2048
BATCH = 4
SEQ = 2048
DEPTH = 1
DEC_BATCH = 32
DEC_SEQ = 1
PAST_LEN = 16384
PAGE_SIZE = 128

N_META = 16
GLA_HEADS = 4
GLA_DK = D_MODEL // 2 // GLA_HEADS
GLA_DV = D_MODEL // GLA_HEADS
GLA_DK_TOT = GLA_HEADS * GLA_DK
GLA_DV_TOT = GLA_HEADS * GLA_DV
GLA_GATE_RANK = 16
GLA_GATE_NORM = 16.0
GLA_CHUNK = 64
ATT_HEADS = 16
ATT_KV_HEADS = 4
HEAD_DIM = D_MODEL // ATT_HEADS
ATT_Q_TOT = ATT_HEADS * HEAD_DIM
ATT_KV_TOT = ATT_KV_HEADS * HEAD_DIM
IDX_HEADS = 16
IDX_DIM = 64
TOPK_MAX = 256
Q_BLOCK = 128
ROPE_THETA = 500000.0
D_FF = 5632
CONV_W = 3
EPS = 1e-6

IN_SIZES = (GLA_DK_TOT, GLA_DK_TOT, GLA_DV_TOT, GLA_DV_TOT, GLA_GATE_RANK,
            ATT_Q_TOT, ATT_KV_TOT, ATT_KV_TOT, IDX_HEADS * IDX_DIM, IDX_DIM, IDX_HEADS,
            D_MODEL, D_MODEL)
IN_WIDTH = sum(IN_SIZES)

kernel_name = "hybrid_gla_dsa_convffn_decode_step"


def rmsnorm(x, g):
    xf = x.astype(jnp.float32)
    y = xf * lax.rsqrt(jnp.mean(xf * xf, axis=-1, keepdims=True) + EPS)
    return (y * g.astype(jnp.float32)).astype(x.dtype)


def rope(x, pos):
    dh = x.shape[-1]
    rot = dh // 4
    half = rot // 2
    inv = jnp.exp(-math.log(ROPE_THETA) * jnp.arange(half, dtype=jnp.float32) * 2.0 / rot)
    ang = pos.astype(jnp.float32)[:, None] * inv[None, :]
    cos = jnp.cos(ang)[None, :, None, :]
    sin = jnp.sin(ang)[None, :, None, :]
    xf = x.astype(jnp.float32)
    x1, x2 = xf[..., :half], xf[..., half:rot]
    out = jnp.concatenate([x1 * cos - x2 * sin, x1 * sin + x2 * cos, xf[..., rot:]], axis=-1)
    return out.astype(x.dtype)


def split_cols(p, sizes):
    out, start = [], 0
    for s in sizes:
        out.append(p[..., start:start + s])
        start += s
    return out


def mixer_inputs(h, pos, w_in, w_alpha_up, b_alpha):
    B, T, _ = h.shape
    (q_a, k_a, v_a, r_a, ga_low, q_b, k_b, v_b, qi, ki, wi, gate_a, gate_b) = split_cols(h @ w_in, IN_SIZES)
    f32 = jnp.float32
    heads = lambda a, n, d: a.reshape(B, T, n, d).transpose(0, 2, 1, 3).astype(f32)
    q_a = heads(q_a, GLA_HEADS, GLA_DK) * (GLA_DK ** -0.5)
    k_a = heads(k_a, GLA_HEADS, GLA_DK)
    v_a = heads(v_a, GLA_HEADS, GLA_DV)
    log_alpha = jax.nn.log_sigmoid((ga_low @ w_alpha_up + b_alpha).astype(f32)) / GLA_GATE_NORM
    g_a = heads(log_alpha, GLA_HEADS, GLA_DK)
    q_b = rope(q_b.reshape(B, T, ATT_HEADS, HEAD_DIM), pos)
    k_b = rope(k_b.reshape(B, T, ATT_KV_HEADS, HEAD_DIM), pos)
    v_b = v_b.reshape(B, T, ATT_KV_HEADS, HEAD_DIM)
    qi = rope(qi.reshape(B, T, IDX_HEADS, IDX_DIM), pos)
    ki = rope(ki.reshape(B, T, 1, IDX_DIM), pos)[:, :, 0]
    wi = wi.astype(f32) * (IDX_HEADS ** -0.5) * (IDX_DIM ** -0.5)
    return (q_a, k_a, v_a, r_a, g_a, q_b, k_b, v_b, qi, ki, wi, gate_a, gate_b)


def gla_chunk(S, q, k, v, g):
    b = jnp.cumsum(g, axis=2)
    o_inter = jnp.einsum('bhtk,bhkv->bhtv', q * jnp.exp(b), S)
    C = q.shape[2]
    causal = jnp.tril(jnp.ones((C, C), dtype=bool))[None, None, :, :, None]
    diff = b[:, :, :, None, :] - b[:, :, None, :, :]
    decay = jnp.where(causal, jnp.exp(jnp.minimum(diff, 0.0)), 0.0)
    A = jnp.einsum('bhtk,bhsk,bhtsk->bhts', q, k, decay)
    o_intra = jnp.einsum('bhts,bhsv->bhtv', A, v)
    b_last = b[:, :, -1:, :]
    S_new = jnp.exp(b_last[:, :, 0, :])[..., None] * S + jnp.einsum('bhsk,bhsv->bhkv', k * jnp.exp(b_last - b), v)
    return S_new, o_inter + o_intra


def gla_prompt(q, k, v, g):
    B, H, T, _ = q.shape
    P = GLA_CHUNK - N_META
    padf = lambda a: jnp.pad(a, ((0, 0), (0, 0), (P, 0), (0, 0)))
    n = (T + P) // GLA_CHUNK
    blocks = lambda a: padf(a).reshape(B, H, n, GLA_CHUNK, a.shape[-1]).transpose(2, 0, 1, 3, 4)
    S0 = jnp.zeros((B, H, GLA_DK, GLA_DV), jnp.float32)
    S_fin, o = lax.scan(lambda S, xs: gla_chunk(S, *xs), S0, (blocks(q), blocks(k), blocks(v), blocks(g)))
    o = o.transpose(1, 2, 0, 3, 4).reshape(B, H, T + P, GLA_DV)[:, :, P:]
    return S_fin, o


def gla_readout(o, r, gain):
    B, H, T, dv = o.shape
    on = o * lax.rsqrt(jnp.mean(o * o, axis=-1, keepdims=True) + EPS)
    on = on.transpose(0, 2, 1, 3).reshape(B, T, H * dv)
    return (on * gain.astype(jnp.float32) * jax.nn.silu(r.astype(jnp.float32))).astype(r.dtype)


def index_topk(qi, wi, ki, q_pos, k_sel):
    s = jnp.einsum('bqhd,bsd->bqhs', qi.astype(jnp.float32), ki.astype(jnp.float32))
    score = jnp.einsum('bqhs,bqh->bqs', jax.nn.relu(s), wi)
    key_pos = jnp.arange(ki.shape[1], dtype=jnp.int32)
    admissible = key_pos[None, :] <= q_pos[:, None]
    score = jnp.where(admissible[None], score, -jnp.inf)
    _, idx = lax.top_k(score, k_sel)
    valid = idx <= q_pos[None, :, None]
    return idx, valid


def sparse_attend(q, k_sel, v_sel, valid):
    B, Tq, H, D = q.shape
    G = H // ATT_KV_HEADS
    qg = q.reshape(B, Tq, ATT_KV_HEADS, G, D).astype(jnp.float32)
    s = jnp.einsum('bqngd,bqknd->bqngk', qg, k_sel.astype(jnp.float32)) * (D ** -0.5)
    s = jnp.where(valid[:, :, None, None, :], s, -jnp.inf)
    p = jax.nn.softmax(s, axis=-1)
    o = jnp.einsum('bqngk,bqknd->bqngd', p, v_sel.astype(jnp.float32))
    return o.reshape(B, Tq, H * D).astype(q.dtype)


def dsa_prompt(q, k, v, qi, ki, wi):
    B, T = q.shape[:2]
    k_sel = min(TOPK_MAX, T // 4)
    nblk = -(-T // Q_BLOCK)
    Tp = nblk * Q_BLOCK
    def to_blocks(a):
        a = jnp.pad(a, [(0, 0), (0, Tp - T)] + [(0, 0)] * (a.ndim - 2))
        return a.reshape((B, nblk, Q_BLOCK) + a.shape[2:]).swapaxes(0, 1)
    pos_blocks = jnp.arange(Tp, dtype=jnp.int32).reshape(nblk, Q_BLOCK)
    gather = jax.vmap(lambda rows, ix: rows[ix])
    def block(args):
        qb, qib, wib, posb = args
        idx, valid = index_topk(qib, wib, ki, posb, k_sel)
        return sparse_attend(qb, gather(k, idx), gather(v, idx), valid)
    out = lax.map(block, (to_blocks(q), to_blocks(qi), to_blocks(wi), pos_blocks))
    return out.swapaxes(0, 1).reshape(B, Tp, -1)[:, :T]


def dsa_sample(q, k_new, v_new, qi, ki_new, wi, cache_k, cache_v, cache_idx_k, page_table):
    DB, Tq = q.shape[:2]
    past = page_table.shape[1] * PAGE_SIZE
    k_sel = min(TOPK_MAX, (past + Tq) // 4)
    ki_past = cache_idx_k[page_table].reshape(DB, past, IDX_DIM).astype(ki_new.dtype)
    ki_all = jnp.concatenate([ki_past, ki_new], axis=1)
    q_pos = past + jnp.arange(Tq, dtype=jnp.int32)
    idx, valid = index_topk(qi, wi, ki_all, q_pos, k_sel)
    in_cache = (idx < past)[..., None, None]
    c = jnp.minimum(idx, past - 1)
    bidx = jnp.arange(DB)[:, None, None]
    phys = page_table[bidx, c // PAGE_SIZE]
    slot = c % PAGE_SIZE
    n = jnp.clip(idx - past, 0, Tq - 1)
    sel = lambda cache, new: jnp.where(in_cache, cache[phys, slot].astype(new.dtype), new[bidx, n])
    return sparse_attend(q, sel(cache_k, k_new), sel(cache_v, v_new), valid)


def conv_ffn(h, conv_state, w_up, conv_w, conv_b, w_down):
    T = h.shape[1]
    u = h @ w_up
    a, b = u[..., :D_FF], u[..., D_FF:]
    ext = jnp.concatenate([conv_state.astype(a.dtype), a], axis=1)
    conv = conv_b
    for j in range(CONV_W):
        conv = conv + conv_w[j] * ext[:, j:j + T]
    out = (jax.nn.silu(conv) * b) @ w_down
    return out, ext[:, -(CONV_W - 1):]


def decoder_layer(x, pos, is_prompt, gla_state, conv_state, cache_k, cache_v, cache_idx_k, page_table,
                  norm_mix_g, w_in, w_alpha_up, b_alpha, gla_norm_g, w_branch_a, w_branch_b, w_out,
                  norm_ffn_g, w_up, conv_w, conv_b, w_down):
    h = rmsnorm(x, norm_mix_g)
    (q_a, k_a, v_a, r_a, g_a, q_b, k_b, v_b, qi, ki, wi, gate_a, gate_b) = mixer_inputs(h, pos, w_in, w_alpha_up, b_alpha)
    if is_prompt:
        S_new, o_a = gla_prompt(q_a, k_a, v_a, g_a)
        o_b = dsa_prompt(q_b, k_b, v_b, qi, ki, wi)
    else:
        S_new, o_a = gla_chunk(gla_state.astype(jnp.float32), q_a, k_a, v_a, g_a)
        o_b = dsa_sample(q_b, k_b, v_b, qi, ki, wi, cache_k, cache_v, cache_idx_k, page_table)
    branch_a = gla_readout(o_a, r_a, gla_norm_g) @ w_branch_a
    branch_b = o_b @ w_branch_b
    mix = (jax.nn.sigmoid(gate_a.astype(jnp.float32)) * branch_a.astype(jnp.float32)
           + jax.nn.sigmoid(gate_b.astype(jnp.float32)) * branch_b.astype(jnp.float32)).astype(x.dtype)
    x = x + mix @ w_out
    f, conv_new = conv_ffn(rmsnorm(x, norm_ffn_g), conv_state, w_up, conv_w, conv_b, w_down)
    x = x + f
    return x, k_b, v_b, ki, S_new.astype(x.dtype), conv_new


def setup_inputs(seed: int = 0) -> dict:
    key = jax.random.key(seed)
    ks = jax.random.split(key, 24)
    f32 = jnp.float32
    nrm = lambda k, shape, scale: jax.random.normal(k, shape, f32) * scale
    n_pages = PAST_LEN // PAGE_SIZE
    n_used = DEC_BATCH * n_pages
    n_pool = (n_used * 5 + 3) // 4
    page_table = jax.random.permutation(ks[0], n_pool)[:n_used].reshape(DEC_BATCH, n_pages).astype(jnp.int32)
    return {
        "x_prompt": nrm(ks[1], (BATCH, SEQ, D_MODEL), 1.0),
        "x_sample": nrm(ks[2], (DEC_BATCH, DEC_SEQ, D_MODEL), 1.0),
        "cache_k": nrm(ks[3], (DEPTH, n_pool, PAGE_SIZE, ATT_KV_HEADS, HEAD_DIM), 1.0),
        "cache_v": nrm(ks[4], (DEPTH, n_pool, PAGE_SIZE, ATT_KV_HEADS, HEAD_DIM), 1.0),
        "cache_idx_k": nrm(ks[5], (DEPTH, n_pool, PAGE_SIZE, IDX_DIM), 1.0),
        "state_gla": nrm(ks[6], (DEPTH, DEC_BATCH, GLA_HEADS, GLA_DK, GLA_DV), 1.0),
        "state_conv": nrm(ks[7], (DEPTH, DEC_BATCH, CONV_W - 1, D_FF), 1.0),
        "page_table": page_table,
        "meta_tokens": nrm(ks[8], (N_META, D_MODEL), 1.0),
        "norm_mix_g": 1.0 + nrm(ks[9], (DEPTH, D_MODEL), 0.02),
        "w_in": nrm(ks[10], (DEPTH, D_MODEL, IN_WIDTH), D_MODEL ** -0.5),
        "w_alpha_up": nrm(ks[11], (DEPTH, GLA_GATE_RANK, GLA_DK_TOT), GLA_GATE_RANK ** -0.5),
        "b_alpha": nrm(ks[12], (DEPTH, GLA_DK_TOT), 0.02),
        "gla_norm_g": 1.0 + nrm(ks[13], (DEPTH, GLA_DV_TOT), 0.02),
        "w_branch_a": nrm(ks[14], (DEPTH, GLA_DV_TOT, D_MODEL), GLA_DV_TOT ** -0.5),
        "w_branch_b": nrm(ks[15], (DEPTH, ATT_Q_TOT, D_MODEL), ATT_Q_TOT ** -0.5),
        "w_out": nrm(ks[16], (DEPTH, D_MODEL, D_MODEL), D_MODEL ** -0.5),
        "norm_ffn_g": 1.0 + nrm(ks[17], (DEPTH, D_MODEL), 0.02),
        "w_up": nrm(ks[18], (DEPTH, D_MODEL, 2 * D_FF), D_MODEL ** -0.5),
        "conv_w": nrm(ks[19], (DEPTH, CONV_W, D_FF), CONV_W ** -0.5),
        "conv_b": nrm(ks[20], (DEPTH, D_FF), 0.02),
        "w_down": nrm(ks[21], (DEPTH, D_FF, D_MODEL), D_FF ** -0.5),
        "norm_final_g": 1.0 + nrm(ks[22], (D_MODEL,), 0.02),
    }


def reference(x_prompt, x_sample, cache_k, cache_v, cache_idx_k, state_gla, state_conv, page_table,
              meta_tokens, norm_mix_g, w_in, w_alpha_up, b_alpha, gla_norm_g, w_branch_a, w_branch_b,
              w_out, norm_ffn_g, w_up, conv_w, conv_b, w_down, norm_final_g):
    B, S, D = x_prompt.shape
    xp = jnp.concatenate([jnp.broadcast_to(meta_tokens[None].astype(x_prompt.dtype), (B, N_META, D)), x_prompt], axis=1)
    pos_p = jnp.arange(S + N_META, dtype=jnp.int32)
    xs = x_sample
    past = page_table.shape[1] * PAGE_SIZE
    pos_s = past + jnp.arange(x_sample.shape[1], dtype=jnp.int32)
    kp_l, vp_l, kip_l, sp_l, cp_l = [], [], [], [], []
    ks_l, vs_l, kis_l, ss_l, cs_l = [], [], [], [], []
    for l in range(DEPTH):
        wl = (norm_mix_g[l], w_in[l], w_alpha_up[l], b_alpha[l], gla_norm_g[l], w_branch_a[l], w_branch_b[l],
              w_out[l], norm_ffn_g[l], w_up[l], conv_w[l], conv_b[l], w_down[l])
        conv0 = jnp.zeros((B, CONV_W - 1, D_FF), xp.dtype)
        xp, kp, vp, kip, sp, cp = decoder_layer(xp, pos_p, True, None, conv0, None, None, None, None, *wl)
        xs, k_s, v_s, ki_s, s_s, c_s = decoder_layer(xs, pos_s, False, state_gla[l], state_conv[l], cache_k[l],
                                                     cache_v[l], cache_idx_k[l], page_table, *wl)
        kp_l.append(kp); vp_l.append(vp); kip_l.append(kip); sp_l.append(sp); cp_l.append(cp)
        ks_l.append(k_s); vs_l.append(v_s); kis_l.append(ki_s); ss_l.append(s_s); cs_l.append(c_s)
    y_prompt = rmsnorm(xp, norm_final_g)[:, N_META:]
    y_sample = rmsnorm(xs, norm_final_g)
    return (y_prompt, y_sample,
            jnp.stack(kp_l), jnp.stack(vp_l), jnp.stack(kip_l), jnp.stack(sp_l), jnp.stack(cp_l),
            jnp.stack(ks_l), jnp.stack(vs_l), jnp.stack(kis_l), jnp.stack(ss_l), jnp.stack(cs_l))
```

```python
import functools
import math

import jax
import jax.numpy as jnp
from jax import lax
from jax.experimental import pallas as pl
from jax.experimental.pallas import tpu as pltpu

F32 = jnp.float32
BF16 = jnp.bfloat16
I32 = jnp.int32

N_META = 16
GLA_HEADS = 4
GLA_GATE_RANK = 16
GLA_GATE_NORM = 16.0
GLA_CHUNK = 64
GLA_SUB = 16
ATT_HEADS = 16
ATT_KV_HEADS = 4
IDX_HEADS = 16
IDX_DIM = 64
TOPK_MAX = 256
ROPE_THETA = 500000.0
CONV_W = 3
EPS = 1e-6
PAGE_SIZE = 128
WI_OFF = IDX_DIM + GLA_GATE_RANK

LANE = 128
FRONT = 128
ROW0 = FRONT - N_META
COL_TILE = 512
VMEM_LIMIT = 56 * 1024 * 1024
INT_MIN = -2 ** 31


def _cparams(sem):
    return pltpu.CompilerParams(dimension_semantics=sem, vmem_limit_bytes=VMEM_LIMIT)


def _sigmoid(x):
    return 1.0 / (1.0 + jnp.exp(-x))


def _sortable(x):
    i = pltpu.bitcast(x, I32)
    return jnp.where(i < 0, i ^ jnp.int32(0x7FFFFFFF), i)


def _kth_largest_key(key_ref, k_sel):
    rows = key_ref.shape[0]

    def body(it, t):
        bit = 31 - it
        cand = t + jnp.left_shift(jnp.int32(1), bit)
        cnt = jnp.sum((key_ref[...] >= cand).astype(I32), axis=1, keepdims=True)
        return jnp.where(cnt >= k_sel, cand, t)

    return lax.fori_loop(0, 32, body, jnp.full((rows, 1), INT_MIN, I32))


def _rope_block(xb, c, sa, sb, half):
    return xb * c + pltpu.roll(xb, LANE - half, 1) * sa + pltpu.roll(xb, half, 1) * sb


def _inproj_kernel(x_ref, g_ref, w_ref, tab_ref, o_ref, h_ref, *, j128, j64, jsm):
    j = pl.program_id(1)

    @pl.when(j == 0)
    def _():
        x = x_ref[...]
        ms = jnp.mean(x * x, axis=-1, keepdims=True)
        h_ref[...] = (x * lax.rsqrt(ms + EPS) * g_ref[...]).astype(BF16)

    acc = jnp.dot(h_ref[...], w_ref[...], preferred_element_type=F32)
    is128 = jnp.logical_and(j >= j128[0], j < j128[1])
    is64 = jnp.logical_and(j >= j64[0], j < j64[1])
    issm = j == jsm

    @pl.when(is128)
    def _():
        c, sa, sb = tab_ref[:, 0:128], tab_ref[:, 128:256], tab_ref[:, 256:384]
        for blk in range(COL_TILE // LANE):
            sl = slice(blk * LANE, (blk + 1) * LANE)
            o_ref[:, sl] = _rope_block(acc[:, sl], c, sa, sb, 16)

    @pl.when(is64)
    def _():
        c, sa, sb = tab_ref[:, 384:512], tab_ref[:, 512:640], tab_ref[:, 640:768]
        for blk in range(COL_TILE // LANE):
            sl = slice(blk * LANE, (blk + 1) * LANE)
            o_ref[:, sl] = _rope_block(acc[:, sl], c, sa, sb, 8)

    @pl.when(issm)
    def _():
        lane = lax.broadcasted_iota(I32, (1, LANE), 1)
        first = lane < IDX_DIM
        c = jnp.where(first, tab_ref[:, 384:512], 1.0)
        sa = jnp.where(first, tab_ref[:, 512:640], 0.0)
        sb = jnp.where(first, tab_ref[:, 640:768], 0.0)
        o_ref[:, 0:LANE] = _rope_block(acc[:, 0:LANE], c, sa, sb, 8)
        o_ref[:, LANE:] = acc[:, LANE:]

    @pl.when(jnp.logical_not(is128 | is64 | issm))
    def _():
        o_ref[...] = acc


def _log_alpha(sm, wau_ref, ba_ref):
    x = jnp.dot(sm.astype(BF16), wau_ref[...], preferred_element_type=F32) + ba_ref[...]
    return (jnp.minimum(x, 0.0) - jnp.log(1.0 + jnp.exp(-jnp.abs(x)))) * (1.0 / GLA_GATE_NORM)


def _row_to_col(row, n):
    eye = lax.broadcasted_iota(I32, (n, n), 0) == lax.broadcasted_iota(I32, (n, n), 1)
    return jnp.sum(jnp.where(eye, row, 0.0), axis=1, keepdims=True)


def _readout(o, r, gain):
    ms = jnp.mean(o * o, axis=-1, keepdims=True)
    return o * lax.rsqrt(ms + EPS) * gain * (r * _sigmoid(r))


def _gla_prompt_kernel(q_ref, k_ref, v_ref, r_ref, sm_ref, wau_ref, ba_ref, gain_ref,
                       o_ref, sfin_ref, s_ref, *, dk):
    c = pl.program_id(2)
    C = GLA_CHUNK

    @pl.when(c == 0)
    def _():
        s_ref[...] = jnp.zeros_like(s_ref)
        o_ref[...] = jnp.zeros_like(o_ref)

    @pl.when(c > 0)
    def _():
        q = q_ref[...] * (dk ** -0.5)
        k = k_ref[...]
        v = v_ref[...]
        vb = v.astype(BF16)
        g = _log_alpha(sm_ref[...], wau_ref, ba_ref)
        row = lax.broadcasted_iota(I32, g.shape, 0)
        b = g
        sh = 1
        while sh < C:
            b = b + jnp.where(row >= sh, pltpu.roll(b, sh, 0), 0.0)
            sh *= 2
        bl = b[C - 1:C, :]
        S = s_ref[...]
        o = jnp.dot((q * jnp.exp(b)).astype(BF16), S.astype(BF16), preferred_element_type=F32)

        tcol = lax.broadcasted_iota(I32, (GLA_SUB, C), 1)
        trow = lax.broadcasted_iota(I32, (GLA_SUB, C), 0)
        a_rows = []
        for blk in range(C // GLA_SUB):
            r0 = blk * GLA_SUB
            ref = b[r0 - 1:r0, :] if blk > 0 else jnp.zeros_like(bl)
            b_i = b[r0:r0 + GLA_SUB, :]
            q_i = q[r0:r0 + GLA_SUB, :]
            qe = q_i * jnp.exp(b_i - ref)
            kf = k * jnp.exp(jnp.minimum(ref - b, 0.0))
            a_off = lax.dot_general(qe.astype(BF16), kf.astype(BF16), (((1,), (1,)), ((), ())),
                                    preferred_element_type=F32)
            diag = jnp.zeros((GLA_SUB, C), F32)
            for sl in range(GLA_SUB):
                s = r0 + sl
                w = q_i * k[s:s + 1, :] * jnp.exp(jnp.minimum(b_i - b[s:s + 1, :], 0.0))
                diag = jnp.where(tcol == s, jnp.sum(w, axis=1, keepdims=True), diag)
            a_rows.append(jnp.where(tcol < r0, a_off, jnp.where(tcol <= trow + r0, diag, 0.0)))
        a = jnp.concatenate(a_rows, axis=0)
        o = o + jnp.dot(a.astype(BF16), vb, preferred_element_type=F32)

        kd = k * jnp.exp(bl - b)
        upd = lax.dot_general(kd.astype(BF16), vb, (((0,), (0,)), ((), ())),
                              preferred_element_type=F32)
        s_ref[...] = S * _row_to_col(jnp.exp(bl), dk) + upd
        o_ref[...] = _readout(o, r_ref[...], gain_ref[...]).astype(BF16)

    @pl.when(c == pl.num_programs(2) - 1)
    def _():
        sfin_ref[...] = s_ref[...]


def _gla_sample_kernel(alias_ref, q_ref, k_ref, v_ref, r_ref, sm_ref, wau_ref, ba_ref, gain_ref,
                       st_ref, o_ref, snew_ref, acc_ref, *, dk):
    del alias_ref
    d = pl.program_id(1)
    q = q_ref[...] * (dk ** -0.5)
    k = k_ref[...]
    v = v_ref[...]
    r = r_ref[...]
    g = _log_alpha(jnp.broadcast_to(sm_ref[...], (8, LANE)), wau_ref, ba_ref)[0:1, :]
    eg = jnp.exp(g)
    S = st_ref[...]
    qe = jnp.broadcast_to(q * eg, (8, dk))
    o = jnp.dot(qe.astype(BF16), S.astype(BF16), preferred_element_type=F32)[0:1, :]
    o = o + jnp.sum(q * k, axis=1, keepdims=True) * v
    snew_ref[...] = S * _row_to_col(eg, dk) + _row_to_col(k, dk) * v
    on = _readout(o, r, gain_ref[...])
    row = lax.broadcasted_iota(I32, acc_ref.shape, 0)

    @pl.when(d == 0)
    def _():
        acc_ref[...] = jnp.zeros_like(acc_ref)

    acc_ref[...] = jnp.where(row == d, on, acc_ref[...])

    @pl.when(d == pl.num_programs(1) - 1)
    def _():
        o_ref[...] = acc_ref[...].astype(BF16)


def _dsa_prompt_kernel(qi_ref, smq_ref, smk_ref, qb_ref, kb_ref, vb_ref, o_ref,
                       ka_ref, kbb_ref, kbf_ref, vbf_ref, key_ref, *, k_sel, hd):
    qb = pl.program_id(1)
    TQ = qi_ref.shape[0]
    TK = smk_ref.shape[0]

    @pl.when(qb == 0)
    def _():
        smk = smk_ref[...]
        lane = lax.broadcasted_iota(I32, (1, LANE), 1)
        ka_ref[...] = jnp.where(lane < IDX_DIM, smk, 0.0).astype(BF16)
        kbb_ref[...] = jnp.where(lane >= IDX_DIM, pltpu.roll(smk, IDX_DIM, 1), 0.0).astype(BF16)
        kbf_ref[...] = kb_ref[...].astype(BF16)
        vbf_ref[...] = vb_ref[...].astype(BF16)

    nt = (((1,), (1,)), ((), ()))
    score = jnp.zeros((TQ, TK), F32)
    for p in range(IDX_HEADS // 2):
        qp = qi_ref[:, p * LANE:(p + 1) * LANE].astype(BF16)
        for half, kref in ((0, ka_ref), (1, kbb_ref)):
            h = 2 * p + half
            s = lax.dot_general(qp, kref[...], nt, preferred_element_type=F32)
            w = smq_ref[:, WI_OFF + h:WI_OFF + h + 1] * (IDX_HEADS ** -0.5 * IDX_DIM ** -0.5)
            score = score + w * jnp.maximum(s, 0.0)

    qrow = qb * TQ + lax.broadcasted_iota(I32, (TQ, 1), 0)
    kcol = lax.broadcasted_iota(I32, (1, TK), 1)
    adm = jnp.logical_and(kcol <= qrow, kcol >= ROW0)
    key_ref[...] = jnp.where(adm, _sortable(score), INT_MIN)
    thr = jnp.maximum(_kth_largest_key(key_ref, k_sel), INT_MIN + 1)
    sel = key_ref[...] >= thr

    G = ATT_HEADS // ATT_KV_HEADS
    for n in range(ATT_KV_HEADS):
        q4 = jnp.concatenate(
            [qb_ref[:, (n * G + gq) * hd:(n * G + gq + 1) * hd].astype(BF16) for gq in range(G)], axis=0)
        s = lax.dot_general(q4, kbf_ref[:, n * hd:(n + 1) * hd], nt,
                            preferred_element_type=F32) * (hd ** -0.5)
        s = jnp.where(sel[None], s.reshape(G, TQ, TK), -jnp.inf).reshape(G * TQ, TK)
        m = jnp.max(s, axis=1, keepdims=True)
        m = jnp.where(m == -jnp.inf, 0.0, m)
        p = jnp.exp(s - m)
        l = jnp.sum(p, axis=1, keepdims=True)
        o = jnp.dot(p.astype(BF16), vbf_ref[:, n * hd:(n + 1) * hd], preferred_element_type=F32)
        o = jnp.where(l > 0.0, o / l, 0.0)
        for gq in range(G):
            o_ref[:, (n * G + gq) * hd:(n * G + gq + 1) * hd] = o[gq * TQ:(gq + 1) * TQ, :].astype(BF16)


def _idx_page_copy(cache_ref, buf_ref, sem, pt_ref, d, p):
    return pltpu.make_async_copy(cache_ref.at[pt_ref[d, p]], buf_ref.at[p], sem)


def _dsa_sample_score_kernel(pt_ref, cache_ref, qi_ref, wi_ref, kin_ref, sc_ref, scn_ref,
                             buf_ref, sem, *, n_pages):
    d = pl.program_id(0)

    def start(p, carry):
        _idx_page_copy(cache_ref, buf_ref, sem, pt_ref, d, p).start()
        return carry

    def wait(p, carry):
        _idx_page_copy(cache_ref, buf_ref, sem, pt_ref, d, p).wait()
        return carry

    lax.fori_loop(0, n_pages, start, 0)
    lax.fori_loop(0, n_pages, wait, 0)
    nt = (((1,), (1,)), ((), ()))
    qi = qi_ref[...].astype(BF16)
    w = wi_ref[...] * (IDX_HEADS ** -0.5 * IDX_DIM ** -0.5)
    kp = buf_ref[...].reshape(n_pages * PAGE_SIZE, IDX_DIM).astype(BF16)
    s = lax.dot_general(qi, kp, nt, preferred_element_type=F32)
    sc_ref[...] = jnp.sum(w * jnp.maximum(s, 0.0), axis=0, keepdims=True)
    kn = jnp.broadcast_to(kin_ref[...], (8, IDX_DIM)).astype(BF16)
    sn = lax.dot_general(qi, kn, nt, preferred_element_type=F32)[:, 0:1]
    scn_ref[...] = jnp.broadcast_to(jnp.sum(w * jnp.maximum(sn, 0.0), axis=0, keepdims=True), (1, LANE))


def _dsa_sample_select_kernel(sc_ref, scn_ref, hl_ref, sel_ref, pos_ref, key_ref, *, k_sel, chunk):
    DB, NP = sc_ref.shape
    nblk = NP // LANE
    lane1 = lax.broadcasted_iota(I32, (1, LANE), 1)
    key_ref[:, 0:NP] = _sortable(sc_ref[...])
    key_ref[:, NP:] = jnp.where(lane1 == 0, _sortable(scn_ref[...]), INT_MIN)
    thr = _kth_largest_key(key_ref, k_sel)
    keys = key_ref[...]
    gt = keys > thr
    eq = keys == thr
    n_gt = jnp.sum(gt.astype(I32), axis=1, keepdims=True).astype(F32)

    iu = lax.broadcasted_iota(I32, (LANE, LANE), 0)
    ju = lax.broadcasted_iota(I32, (LANE, LANE), 1)
    upper = jnp.where(iu < ju, 1.0, 0.0).astype(BF16)
    ones = jnp.ones((LANE, LANE), BF16)

    def excl_prefix(mask):
        mb = jnp.where(mask, 1.0, 0.0).astype(BF16)
        stacked = jnp.concatenate([mb[:, c * LANE:(c + 1) * LANE] for c in range(nblk + 1)], axis=0)
        within = jnp.dot(stacked, upper, preferred_element_type=F32)
        tot = jnp.dot(stacked, ones, preferred_element_type=F32)[:, 0:1]
        outs = []
        run = jnp.zeros((DB, 1), F32)
        for c in range(nblk + 1):
            outs.append(within[c * DB:(c + 1) * DB, :] + run)
            run = run + tot[c * DB:(c + 1) * DB, :]
        return jnp.concatenate(outs, axis=1)

    pos_gt = excl_prefix(gt)
    pos_eq = excl_prefix(eq) + n_gt
    keep_eq = jnp.logical_and(eq, pos_eq < k_sel)
    pos = jnp.where(gt, pos_gt, jnp.where(keep_eq, pos_eq, -1.0))
    for d in range(DB):
        pos_ref[d] = pos[d:d + 1, :]

    jrow = lax.broadcasted_iota(I32, (k_sel, 1), 0).astype(F32)
    width = NP + LANE
    def compact(d, carry):
        acc = jnp.zeros((k_sel, LANE), F32)
        for c0 in range(0, width, chunk):
            c1 = min(c0 + chunk, width)
            e = jnp.where(pos_ref[d, :, c0:c1] == jrow, 1.0, 0.0).astype(BF16)
            acc = acc + jnp.dot(e, hl_ref[c0:c1, :], preferred_element_type=F32)
        sel_ref[d] = acc.astype(I32)
        return carry

    lax.fori_loop(0, DB, compact, 0)


def _row_copy(cache_ref, buf_ref, sem, row, j):
    return pltpu.make_async_copy(cache_ref.at[pl.ds(row, 1)], buf_ref.at[pl.ds(j, 1)], sem)


def _dsa_sample_attend_kernel(sel_s_ref, pt_ref, alias_ref, ck_ref, cv_ref, selv_ref, q_ref,
                              kn_ref, vn_ref, o_ref, kbuf_ref, vbuf_ref, acc_ref, sem,
                              *, k_sel, n_pages, hd):
    del alias_ref
    d = pl.program_id(0)

    def rows(j):
        page = jnp.minimum(sel_s_ref[d, 2 * j], n_pages - 1)
        return pt_ref[d, page] * PAGE_SIZE + sel_s_ref[d, 2 * j + 1]

    def start(j, carry):
        row = rows(j)
        _row_copy(ck_ref, kbuf_ref, sem.at[0], row, j).start()
        _row_copy(cv_ref, vbuf_ref, sem.at[1], row, j).start()
        return carry

    def wait(j, carry):
        row = rows(j)
        _row_copy(ck_ref, kbuf_ref, sem.at[0], row, j).wait()
        _row_copy(cv_ref, vbuf_ref, sem.at[1], row, j).wait()
        return carry

    lax.fori_loop(0, k_sel, start, 0)
    lax.fori_loop(0, k_sel, wait, 0)
    is_new = selv_ref[:, 0:1] >= n_pages
    kk = jnp.where(is_new, kn_ref[...], kbuf_ref[...]).astype(BF16)
    vv = jnp.where(is_new, vn_ref[...], vbuf_ref[...]).astype(BF16)
    G = ATT_HEADS // ATT_KV_HEADS
    nt = (((1,), (1,)), ((), ()))
    q = q_ref[...]
    outs = []
    for n in range(ATT_KV_HEADS):
        qn = jnp.concatenate([q[n * G:(n + 1) * G, :], jnp.zeros((8 - G, hd), F32)], axis=0).astype(BF16)
        s = lax.dot_general(qn, kk[:, n * hd:(n + 1) * hd], nt, preferred_element_type=F32) * (hd ** -0.5)
        m = jnp.max(s, axis=1, keepdims=True)
        p = jnp.exp(s - m)
        l = jnp.sum(p, axis=1, keepdims=True)
        o = jnp.dot(p.astype(BF16), vv[:, n * hd:(n + 1) * hd], preferred_element_type=F32) / l
        outs.append(o[0:G, :])
    acc_ref[pl.ds(d, 1)] = jnp.concatenate(outs, axis=0)[None]

    @pl.when(d == pl.num_programs(0) - 1)
    def _():
        for h in range(ATT_HEADS):
            o_ref[:, h * hd:(h + 1) * hd] = acc_ref[:, h, :].astype(BF16)


def _mix_kernel(a_ref, b_ref, wa_ref, wb_ref, ga_ref, gb_ref, o_ref):
    ba = jnp.dot(a_ref[...], wa_ref[...], preferred_element_type=F32)
    bb = jnp.dot(b_ref[...], wb_ref[...], preferred_element_type=F32)
    o_ref[...] = (_sigmoid(ga_ref[...]) * ba + _sigmoid(gb_ref[...]) * bb).astype(BF16)


def _outproj_kernel(x_ref, m_ref, w_ref, g_ref, x1_ref, h_ref):
    x1 = x_ref[...] + jnp.dot(m_ref[...], w_ref[...], preferred_element_type=F32)
    x1_ref[...] = x1
    ms = jnp.mean(x1 * x1, axis=-1, keepdims=True)
    h_ref[...] = (x1 * lax.rsqrt(ms + EPS) * g_ref[...]).astype(BF16)


def _ffn_kernel(h_ref, halo_ref, x1_ref, wa_ref, wb_ref, wd_ref, cw_ref, cb_ref, st0_ref, st1_ref,
                gf_ref, y_ref, head_ref, tail_ref, acc_ref, s1_ref, s2_ref, *, n_dec):
    i = pl.program_id(0)
    j = pl.program_id(1)
    h = h_ref[...]
    a = jnp.dot(h, wa_ref[...], preferred_element_type=F32)
    b = jnp.dot(h, wb_ref[...], preferred_element_type=F32)
    ah = jnp.dot(halo_ref[...], wa_ref[...], preferred_element_type=F32)
    ah = jnp.where(i > 0, ah, 0.0)
    row = lax.broadcasted_iota(I32, a.shape, 0)
    hl = ah.shape[0]
    s1_ref[...] = jnp.where(row == 0, ah[hl - 1:hl, :], pltpu.roll(a, 1, 0))
    s2_ref[...] = jnp.where(row == 0, ah[hl - 2:hl - 1, :],
                            jnp.where(row == 1, ah[hl - 1:hl, :], pltpu.roll(a, 2, 0)))

    @pl.when(i == 0)
    def _():
        s1_ref[0:n_dec, :] = st1_ref[...]
        s2_ref[0:n_dec, :] = st0_ref[...]

    conv = cb_ref[...] + cw_ref[0:1, :] * s2_ref[...] + cw_ref[1:2, :] * s1_ref[...] + cw_ref[2:3, :] * a
    gate = (conv * _sigmoid(conv) * b).astype(BF16)
    part = jnp.dot(gate, wd_ref[...], preferred_element_type=F32)

    @pl.when(j == 0)
    def _():
        acc_ref[...] = part

    @pl.when(j > 0)
    def _():
        acc_ref[...] = acc_ref[...] + part

    head_ref[...] = a[0:head_ref.shape[0], :]
    tail_ref[...] = a[a.shape[0] - 8:, :]

    @pl.when(j == pl.num_programs(1) - 1)
    def _():
        x2 = x1_ref[...] + acc_ref[...]
        ms = jnp.mean(x2 * x2, axis=-1, keepdims=True)
        y_ref[...] = x2 * lax.rsqrt(ms + EPS) * gf_ref[...]


def _rope_tables(pos, hd, reps):
    rot = hd // 4
    half = rot // 2
    inv = jnp.exp(-math.log(ROPE_THETA) * jnp.arange(half, dtype=F32) * 2.0 / rot)
    ang = pos.astype(F32)[:, None] * inv[None, :]
    cos, sin = jnp.cos(ang), jnp.sin(ang)
    n = pos.shape[0]
    one = jnp.ones((n, hd - rot), F32)
    zero_r = jnp.zeros((n, hd - rot), F32)
    zero_h = jnp.zeros((n, half), F32)
    c = jnp.concatenate([cos, cos, one], axis=1)
    sa = jnp.concatenate([-sin, zero_h, zero_r], axis=1)
    sb = jnp.concatenate([zero_h, sin, zero_r], axis=1)
    return [jnp.tile(t, (1, reps)) for t in (c, sa, sb)]


def kernel(x_prompt, x_sample, cache_k, cache_v, cache_idx_k, state_gla, state_conv, page_table, meta_tokens, norm_mix_g, w_in, w_alpha_up, b_alpha, gla_norm_g, w_branch_a, w_branch_b, w_out, norm_ffn_g, w_up, conv_w, conv_b, w_down, norm_final_g):
    B, SEQ, D = x_prompt.shape
    DB = x_sample.shape[0]
    assert x_sample.shape[1] == 1 and w_in.shape[0] == 1
    n_pool = cache_k.shape[1]
    n_pages = page_table.shape[1]
    past = n_pages * PAGE_SIZE
    assert n_pages == LANE and PAGE_SIZE == LANE
    dff = w_down.shape[1]
    H = GLA_HEADS
    dk = D // 2 // H
    dv = D // H
    hd = D // ATT_HEADS
    kvw = ATT_KV_HEADS * hd
    T = SEQ + N_META
    TP = SEQ + FRONT
    R = B * TP
    assert DB <= GLA_CHUNK and DB % 16 == 0 and SEQ % LANE == 0 and dff % COL_TILE == 0
    k_sel_p = min(TOPK_MAX, T // 4)
    k_sel_s = min(TOPK_MAX, (past + 1) // 4)

    sizes = (H * dk, H * dk, H * dv, H * dv, GLA_GATE_RANK, ATT_HEADS * hd, kvw, kvw,
             IDX_HEADS * IDX_DIM, IDX_DIM, IDX_HEADS, D, D)
    offs = [0]
    for s_ in sizes:
        offs.append(offs[-1] + s_)
    wcol = lambda idx: w_in[0][:, offs[idx]:offs[idx + 1]]
    small_pad = COL_TILE - (IDX_DIM + GLA_GATE_RANK + IDX_HEADS)
    order = (0, 1, 2, 3, 5, 6, 7, 8, 11, 12, 9, 4, 10)
    w_cat = jnp.concatenate([wcol(i_) for i_ in order] + [jnp.zeros((D, small_pad), F32)], axis=1).astype(BF16)
    c_off = {}
    run = 0
    for i_ in order:
        c_off[i_] = run
        run += sizes[i_]
    NW = w_cat.shape[1]
    QA, KA, VA, RA, QB, KB, VB, QI, GA, GB, SM = (c_off[0], c_off[1], c_off[2], c_off[3], c_off[5], c_off[6],
                                                  c_off[7], c_off[8], c_off[11], c_off[12], c_off[9])
    assert all(v_ % COL_TILE == 0 for v_ in (QA, KA, VA, RA, QB, KB, VB, QI, GA, GB, SM))
    assert (KB - QB) % COL_TILE == 0 and VB - KB == COL_TILE and hd == LANE
    n_ct = NW // COL_TILE
    wau_pad = jnp.zeros((LANE, H * dk), F32).at[IDX_DIM:IDX_DIM + GLA_GATE_RANK].set(w_alpha_up[0]).astype(BF16)
    wa_bf = w_branch_a[0].astype(BF16)
    wb_bf = w_branch_b[0].astype(BF16)
    wo_bf = w_out[0].astype(BF16)
    wup_bf = w_up[0].astype(BF16)
    wdn_bf = w_down[0].astype(BF16)

    front = jnp.zeros((B, ROW0, D), F32).at[0, :DB].set(x_sample[:, 0])
    meta = jnp.broadcast_to(meta_tokens[None].astype(F32), (B, N_META, D))
    x_all = jnp.concatenate([front, meta, x_prompt], axis=1).reshape(R, D)
    rpos = jnp.maximum(jnp.arange(TP, dtype=jnp.int32) - ROW0, 0)
    pos = jnp.tile(rpos, B).at[:DB].set(past)
    tabs = jnp.concatenate(_rope_tables(pos, hd, 1) + _rope_tables(pos, IDX_DIM, LANE // IDX_DIM), axis=1)

    tm1 = TP // 2
    P = pl.pallas_call(
        functools.partial(_inproj_kernel, j128=(QB // COL_TILE, VB // COL_TILE),
                          j64=(QI // COL_TILE, (QI + IDX_HEADS * IDX_DIM) // COL_TILE), jsm=SM // COL_TILE),
        grid=(R // tm1, n_ct),
        in_specs=[pl.BlockSpec((tm1, D), lambda i, j: (i, 0)),
                  pl.BlockSpec((1, D), lambda i, j: (0, 0)),
                  pl.BlockSpec((D, COL_TILE), lambda i, j: (0, j)),
                  pl.BlockSpec((tm1, 6 * LANE), lambda i, j: (i, 0))],
        out_specs=pl.BlockSpec((tm1, COL_TILE), lambda i, j: (i, j)),
        out_shape=jax.ShapeDtypeStruct((R, NW), F32),
        scratch_shapes=[pltpu.VMEM((tm1, D), BF16)],
        compiler_params=_cparams(("parallel", "arbitrary")),
        name="inproj",
    )(x_all, norm_mix_g, w_cat, tabs)

    C = GLA_CHUNK
    ncb = TP // C
    ba2 = b_alpha.reshape(1, H * dk)
    gain2 = gla_norm_g.reshape(1, H * dv)
    smc = SM // LANE
    a_out, s_fin = pl.pallas_call(
        functools.partial(_gla_prompt_kernel, dk=dk),
        grid=(B, H, ncb),
        in_specs=[pl.BlockSpec((C, dk), lambda b, h, c: (b * ncb + c, QA // dk + h)),
                  pl.BlockSpec((C, dk), lambda b, h, c: (b * ncb + c, KA // dk + h)),
                  pl.BlockSpec((C, dv), lambda b, h, c: (b * ncb + c, VA // dv + h)),
                  pl.BlockSpec((C, dv), lambda b, h, c: (b * ncb + c, RA // dv + h)),
                  pl.BlockSpec((C, LANE), lambda b, h, c: (b * ncb + c, smc)),
                  pl.BlockSpec((LANE, dk), lambda b, h, c: (0, h)),
                  pl.BlockSpec((1, dk), lambda b, h, c: (0, h)),
                  pl.BlockSpec((1, dv), lambda b, h, c: (0, h))],
        out_specs=[pl.BlockSpec((C, dv), lambda b, h, c: (b * ncb + c, h)),
                   pl.BlockSpec((None, None, dk, dv), lambda b, h, c: (b, h, 0, 0))],
        out_shape=[jax.ShapeDtypeStruct((R, H * dv), BF16),
                   jax.ShapeDtypeStruct((B, H, dk, dv), F32)],
        scratch_shapes=[pltpu.VMEM((dk, dv), F32)],
        compiler_params=_cparams(("parallel", "parallel", "arbitrary")),
        name="gla_prompt",
    )(P, P, P, P, P, wau_pad, ba2, gain2)

    P_rows = P.reshape(R, 1, NW)
    a_out, s_new_s = pl.pallas_call(
        functools.partial(_gla_sample_kernel, dk=dk),
        grid=(H, DB),
        in_specs=[pl.BlockSpec(memory_space=pl.ANY),
                  pl.BlockSpec((None, 1, dk), lambda h, d: (d, 0, QA // dk + h)),
                  pl.BlockSpec((None, 1, dk), lambda h, d: (d, 0, KA // dk + h)),
                  pl.BlockSpec((None, 1, dv), lambda h, d: (d, 0, VA // dv + h)),
                  pl.BlockSpec((None, 1, dv), lambda h, d: (d, 0, RA // dv + h)),
                  pl.BlockSpec((None, 1, LANE), lambda h, d: (d, 0, smc)),
                  pl.BlockSpec((LANE, dk), lambda h, d: (0, h)),
                  pl.BlockSpec((1, dk), lambda h, d: (0, h)),
                  pl.BlockSpec((1, dv), lambda h, d: (0, h)),
                  pl.BlockSpec((None, None, dk, dv), lambda h, d: (d, h, 0, 0))],
        out_specs=[pl.BlockSpec((DB, dv), lambda h, d: (0, h)),
                   pl.BlockSpec((None, None, dk, dv), lambda h, d: (d, h, 0, 0))],
        out_shape=[jax.ShapeDtypeStruct((R, H * dv), BF16),
                   jax.ShapeDtypeStruct((DB, H, dk, dv), F32)],
        scratch_shapes=[pltpu.VMEM((DB, dv), F32)],
        input_output_aliases={0: 0},
        compiler_params=_cparams(("arbitrary", "arbitrary")),
        name="gla_sample",
    )(a_out, P_rows, P_rows, P_rows, P_rows, P_rows, wau_pad, ba2, gain2, state_gla[0])

    TQ = LANE
    nqb = TP // TQ
    o_b = pl.pallas_call(
        functools.partial(_dsa_prompt_kernel, k_sel=k_sel_p, hd=hd),
        grid=(B, nqb),
        in_specs=[pl.BlockSpec((TQ, IDX_HEADS * IDX_DIM), lambda b, q: (b * nqb + q, QI // (IDX_HEADS * IDX_DIM))),
                  pl.BlockSpec((TQ, LANE), lambda b, q: (b * nqb + q, smc)),
                  pl.BlockSpec((TP, LANE), lambda b, q: (b, smc)),
                  pl.BlockSpec((TQ, ATT_HEADS * hd), lambda b, q: (b * nqb + q, QB // (ATT_HEADS * hd))),
                  pl.BlockSpec((TP, kvw), lambda b, q: (b, KB // kvw)),
                  pl.BlockSpec((TP, kvw), lambda b, q: (b, VB // kvw))],
        out_specs=pl.BlockSpec((TQ, ATT_HEADS * hd), lambda b, q: (b * nqb + q, 0)),
        out_shape=jax.ShapeDtypeStruct((R, ATT_HEADS * hd), BF16),
        scratch_shapes=[pltpu.VMEM((TP, LANE), BF16), pltpu.VMEM((TP, LANE), BF16),
                        pltpu.VMEM((TP, kvw), BF16), pltpu.VMEM((TP, kvw), BF16),
                        pltpu.VMEM((TQ, TP), I32)],
        compiler_params=_cparams(("parallel", "arbitrary")),
        name="dsa_prompt",
    )(P, P, P, P, P, P)

    p_s = P[:DB]
    qi_s = p_s[:, QI:QI + IDX_HEADS * IDX_DIM].reshape(DB, IDX_HEADS, IDX_DIM)
    wi_s = p_s[:, SM + WI_OFF:SM + WI_OFF + IDX_HEADS].reshape(DB, IDX_HEADS, 1)
    ki_s = p_s[:, SM:SM + IDX_DIM].reshape(DB, 1, IDX_DIM)
    sc, scn = pl.pallas_call(
        functools.partial(_dsa_sample_score_kernel, n_pages=n_pages),
        grid_spec=pltpu.PrefetchScalarGridSpec(
            num_scalar_prefetch=1,
            grid=(DB,),
            in_specs=[pl.BlockSpec(memory_space=pl.ANY),
                      pl.BlockSpec((None, IDX_HEADS, IDX_DIM), lambda d, pt: (d, 0, 0)),
                      pl.BlockSpec((None, IDX_HEADS, 1), lambda d, pt: (d, 0, 0)),
                      pl.BlockSpec((None, 1, IDX_DIM), lambda d, pt: (d, 0, 0))],
            out_specs=[pl.BlockSpec((None, 1, past), lambda d, pt: (d, 0, 0)),
                       pl.BlockSpec((None, 1, LANE), lambda d, pt: (d, 0, 0))],
            scratch_shapes=[pltpu.VMEM((n_pages, PAGE_SIZE, IDX_DIM), F32), pltpu.SemaphoreType.DMA(())]),
        out_shape=[jax.ShapeDtypeStruct((DB, 1, past), F32), jax.ShapeDtypeStruct((DB, 1, LANE), F32)],
        compiler_params=_cparams(("arbitrary",)),
        name="dsa_sample_score",
    )(page_table, cache_idx_k[0], qi_s, wi_s, ki_s)

    sidx = jnp.arange(past + LANE, dtype=jnp.int32)
    hl = jnp.zeros((past + LANE, LANE), F32).at[:, 0].set((sidx // PAGE_SIZE).astype(F32))
    hl = hl.at[:, 1].set((sidx % PAGE_SIZE).astype(F32)).astype(BF16)
    sel = pl.pallas_call(
        functools.partial(_dsa_sample_select_kernel, k_sel=k_sel_s, chunk=2048),
        out_shape=jax.ShapeDtypeStruct((DB, k_sel_s, LANE), I32),
        scratch_shapes=[pltpu.VMEM((DB, 1, past + LANE), F32), pltpu.VMEM((DB, past + LANE), I32)],
        compiler_params=pltpu.CompilerParams(vmem_limit_bytes=VMEM_LIMIT),
        name="dsa_sample_select",
    )(sc.reshape(DB, past), scn.reshape(DB, LANE), hl)

    sel_s = sel[:, :, 0:2].reshape(DB, 2 * k_sel_s)
    q_s = p_s[:, QB:QB + ATT_HEADS * hd].reshape(DB, ATT_HEADS, hd)
    kn_s = p_s[:, KB:KB + kvw]
    vn_s = p_s[:, VB:VB + kvw]
    o_b = pl.pallas_call(
        functools.partial(_dsa_sample_attend_kernel, k_sel=k_sel_s, n_pages=n_pages, hd=hd),
        grid_spec=pltpu.PrefetchScalarGridSpec(
            num_scalar_prefetch=2,
            grid=(DB,),
            in_specs=[pl.BlockSpec(memory_space=pl.ANY),
                      pl.BlockSpec(memory_space=pl.ANY),
                      pl.BlockSpec(memory_space=pl.ANY),
                      pl.BlockSpec((None, k_sel_s, LANE), lambda d, s_, pt: (d, 0, 0)),
                      pl.BlockSpec((None, ATT_HEADS, hd), lambda d, s_, pt: (d, 0, 0)),
                      pl.BlockSpec((None, 1, kvw), lambda d, s_, pt: (d, 0, KB // kvw)),
                      pl.BlockSpec((None, 1, kvw), lambda d, s_, pt: (d, 0, VB // kvw))],
            out_specs=pl.BlockSpec((DB, ATT_HEADS * hd), lambda d, s_, pt: (0, 0)),
            scratch_shapes=[pltpu.VMEM((k_sel_s, kvw), F32), pltpu.VMEM((k_sel_s, kvw), F32),
                            pltpu.VMEM((DB, ATT_HEADS, hd), F32), pltpu.SemaphoreType.DMA((2,))]),
        out_shape=jax.ShapeDtypeStruct((R, ATT_HEADS * hd), BF16),
        input_output_aliases={2: 0},
        compiler_params=_cparams(("arbitrary",)),
        name="dsa_sample_attend",
    )(sel_s, page_table, o_b, cache_k[0].reshape(n_pool * PAGE_SIZE, kvw),
      cache_v[0].reshape(n_pool * PAGE_SIZE, kvw), sel, q_s, P_rows, P_rows)

    mix = pl.pallas_call(
        _mix_kernel,
        grid=(R // tm1, D // COL_TILE),
        in_specs=[pl.BlockSpec((tm1, H * dv), lambda i, j: (i, 0)),
                  pl.BlockSpec((tm1, ATT_HEADS * hd), lambda i, j: (i, 0)),
                  pl.BlockSpec((H * dv, COL_TILE), lambda i, j: (0, j)),
                  pl.BlockSpec((ATT_HEADS * hd, COL_TILE), lambda i, j: (0, j)),
                  pl.BlockSpec((tm1, COL_TILE), lambda i, j: (i, GA // COL_TILE + j)),
                  pl.BlockSpec((tm1, COL_TILE), lambda i, j: (i, GB // COL_TILE + j))],
        out_specs=pl.BlockSpec((tm1, COL_TILE), lambda i, j: (i, j)),
        out_shape=jax.ShapeDtypeStruct((R, D), BF16),
        compiler_params=_cparams(("parallel", "arbitrary")),
        name="mix",
    )(a_out, o_b, wa_bf, wb_bf, P, P)

    tm2 = TP // 4
    x1, h2 = pl.pallas_call(
        _outproj_kernel,
        grid=(R // tm2,),
        in_specs=[pl.BlockSpec((tm2, D), lambda i: (i, 0)),
                  pl.BlockSpec((tm2, D), lambda i: (i, 0)),
                  pl.BlockSpec((D, D), lambda i: (0, 0)),
                  pl.BlockSpec((1, D), lambda i: (0, 0))],
        out_specs=[pl.BlockSpec((tm2, D), lambda i: (i, 0)),
                   pl.BlockSpec((tm2, D), lambda i: (i, 0))],
        out_shape=[jax.ShapeDtypeStruct((R, D), F32), jax.ShapeDtypeStruct((R, D), BF16)],
        compiler_params=_cparams(("parallel",)),
        name="outproj",
    )(x_all, mix, wo_bf, norm_ffn_g)

    nrb = R // tm2
    nft = dff // COL_TILE
    y_all, a_head, a_tail = pl.pallas_call(
        functools.partial(_ffn_kernel, n_dec=DB),
        grid=(nrb, nft),
        in_specs=[pl.BlockSpec((tm2, D), lambda i, j: (i, 0)),
                  pl.BlockSpec((16, D), lambda i, j: (jnp.maximum(i * (tm2 // 16) - 1, 0), 0)),
                  pl.BlockSpec((tm2, D), lambda i, j: (i, 0)),
                  pl.BlockSpec((D, COL_TILE), lambda i, j: (0, j)),
                  pl.BlockSpec((D, COL_TILE), lambda i, j: (0, nft + j)),
                  pl.BlockSpec((COL_TILE, D), lambda i, j: (j, 0)),
                  pl.BlockSpec((CONV_W, COL_TILE), lambda i, j: (0, j)),
                  pl.BlockSpec((1, COL_TILE), lambda i, j: (0, j)),
                  pl.BlockSpec((DB, COL_TILE), lambda i, j: (0, j)),
                  pl.BlockSpec((DB, COL_TILE), lambda i, j: (0, j)),
                  pl.BlockSpec((1, D), lambda i, j: (0, 0))],
        out_specs=[pl.BlockSpec((tm2, D), lambda i, j: (i, 0)),
                   pl.BlockSpec((DB, COL_TILE), lambda i, j: (i, j)),
                   pl.BlockSpec((8, COL_TILE), lambda i, j: (i, j))],
        out_shape=[jax.ShapeDtypeStruct((R, D), F32),
                   jax.ShapeDtypeStruct((nrb * DB, dff), F32),
                   jax.ShapeDtypeStruct((nrb * 8, dff), F32)],
        scratch_shapes=[pltpu.VMEM((tm2, D), F32), pltpu.VMEM((tm2, COL_TILE), F32),
                        pltpu.VMEM((tm2, COL_TILE), F32)],
        compiler_params=_cparams(("parallel", "arbitrary")),
        name="ffn",
    )(h2, h2, x1, wup_bf, wup_bf, wdn_bf, conv_w[0], conv_b, state_conv[0, :, 0], state_conv[0, :, 1],
      norm_final_g.reshape(1, D))

    y3 = y_all.reshape(B, TP, D)
    y_prompt = y3[:, FRONT:]
    y_sample = y_all[:DB].reshape(DB, 1, D)
    P3 = P.reshape(B, TP, NW)
    new_k_p = P3[:, ROW0:, KB:KB + kvw].reshape(1, B, T, ATT_KV_HEADS, hd)
    new_v_p = P3[:, ROW0:, VB:VB + kvw].reshape(1, B, T, ATT_KV_HEADS, hd)
    new_ki_p = P3[:, ROW0:, SM:SM + IDX_DIM].reshape(1, B, T, IDX_DIM)
    new_gla_p = s_fin[None]
    tails = a_tail.reshape(B, nrb // B, 8, dff)[:, -1, 8 - (CONV_W - 1):, :]
    new_conv_p = tails[None]
    new_k_s = kn_s.reshape(1, DB, 1, ATT_KV_HEADS, hd)
    new_v_s = vn_s.reshape(1, DB, 1, ATT_KV_HEADS, hd)
    new_ki_s = ki_s.reshape(1, DB, 1, IDX_DIM)
    new_gla_s = s_new_s[None]
    new_conv_s = jnp.stack([state_conv[0, :, 1], a_head[:DB]], axis=1)[None]
    return (y_prompt, y_sample, new_k_p, new_v_p, new_ki_p, new_gla_p, new_conv_p,
            new_k_s, new_v_s, new_ki_s, new_gla_s, new_conv_s)
```

```python
import functools
import math

import jax
import jax.numpy as jnp
from jax import lax
from jax.experimental import pallas as pl
from jax.experimental.pallas import tpu as pltpu

F32 = jnp.float32
BF16 = jnp.bfloat16
I32 = jnp.int32

N_META = 16
GLA_HEADS = 4
GLA_GATE_RANK = 16
GLA_GATE_NORM = 16.0
GLA_CHUNK = 64
GLA_SUB = 16
ATT_HEADS = 16
ATT_KV_HEADS = 4
IDX_HEADS = 16
IDX_DIM = 64
TOPK_MAX = 256
ROPE_THETA = 500000.0
CONV_W = 3
EPS = 1e-6
PAGE_SIZE = 128
WI_OFF = IDX_DIM + GLA_GATE_RANK

LANE = 128
FRONT = 128
ROW0 = FRONT - N_META
COL_TILE = 512
QB_PER_CLASS = 4
VMEM_LIMIT = 56 * 1024 * 1024
INT_MIN = -2 ** 31


def _cparams(sem):
    return pltpu.CompilerParams(dimension_semantics=sem, vmem_limit_bytes=VMEM_LIMIT)


def _sigmoid(x):
    return 1.0 / (1.0 + jnp.exp(-x))


def _sortable(x):
    i = pltpu.bitcast(x, I32)
    return jnp.where(i < 0, i ^ jnp.int32(0x7FFFFFFF), i)


def _kth_largest_key(load_keys, rows, k_sel):
    def body(it, t):
        bit = 31 - it
        cand = t + jnp.left_shift(jnp.int32(1), bit)
        cnt = jnp.sum((load_keys() >= cand).astype(I32), axis=1, keepdims=True)
        return jnp.where(cnt >= k_sel, cand, t)

    return lax.fori_loop(0, 32, body, jnp.full((rows, 1), INT_MIN, I32))


def _rope_block(xb, c, sa, sb, half):
    return xb * c + pltpu.roll(xb, LANE - half, 1) * sa + pltpu.roll(xb, half, 1) * sb


def _inproj_kernel(x_ref, g_ref, wa_ref, wb_ref, wc_ref, ws_ref, tab_ref, o_ref, h_ref,
                   *, n_a, n_b, n_c, b128, b64):
    j = pl.program_id(1)

    @pl.when(j == 0)
    def _():
        x = x_ref[...]
        ms = jnp.mean(x * x, axis=-1, keepdims=True)
        h_ref[...] = (x * lax.rsqrt(ms + EPS) * g_ref[...]).astype(BF16)

    def proj(w_ref):
        return jnp.dot(h_ref[...], w_ref[...], preferred_element_type=F32)

    @pl.when(j < n_a)
    def _():
        o_ref[...] = proj(wa_ref)

    @pl.when(jnp.logical_and(j >= n_a, j < n_a + n_b))
    def _():
        acc = proj(wb_ref)
        jb = j - n_a
        is128 = jnp.logical_and(jb >= b128[0], jb < b128[1])
        is64 = jnp.logical_and(jb >= b64[0], jb < b64[1])

        @pl.when(is128)
        def _():
            c, sa, sb = tab_ref[:, 0:128], tab_ref[:, 128:256], tab_ref[:, 256:384]
            for blk in range(COL_TILE // LANE):
                sl = slice(blk * LANE, (blk + 1) * LANE)
                o_ref[:, sl] = _rope_block(acc[:, sl], c, sa, sb, 16)

        @pl.when(is64)
        def _():
            c, sa, sb = tab_ref[:, 384:512], tab_ref[:, 512:640], tab_ref[:, 640:768]
            for blk in range(COL_TILE // LANE):
                sl = slice(blk * LANE, (blk + 1) * LANE)
                o_ref[:, sl] = _rope_block(acc[:, sl], c, sa, sb, 8)

        @pl.when(jnp.logical_not(is128 | is64))
        def _():
            o_ref[...] = acc

    @pl.when(jnp.logical_and(j >= n_a + n_b, j < n_a + n_b + n_c))
    def _():
        o_ref[...] = proj(wc_ref)

    @pl.when(j == n_a + n_b + n_c)
    def _():
        acc = proj(ws_ref)
        lane = lax.broadcasted_iota(I32, (1, LANE), 1)
        first = lane < IDX_DIM
        c = jnp.where(first, tab_ref[:, 384:512], 1.0)
        sa = jnp.where(first, tab_ref[:, 512:640], 0.0)
        sb = jnp.where(first, tab_ref[:, 640:768], 0.0)
        o_ref[:, 0:LANE] = _rope_block(acc[:, 0:LANE], c, sa, sb, 8)
        o_ref[:, LANE:] = acc[:, LANE:]


def _log_alpha(sm, wau_ref, ba_ref):
    x = jnp.dot(sm.astype(BF16), wau_ref[...], preferred_element_type=F32) + ba_ref[...]
    return (jnp.minimum(x, 0.0) - jnp.log(1.0 + jnp.exp(-jnp.abs(x)))) * (1.0 / GLA_GATE_NORM)


def _row_to_col(row, n):
    eye = lax.broadcasted_iota(I32, (n, n), 0) == lax.broadcasted_iota(I32, (n, n), 1)
    return jnp.sum(jnp.where(eye, row, 0.0), axis=1, keepdims=True)


def _readout(o, r, gain):
    ms = jnp.mean(o * o, axis=-1, keepdims=True)
    return o * lax.rsqrt(ms + EPS) * gain * (r * _sigmoid(r))


def _gla_prompt_kernel(q_ref, k_ref, v_ref, r_ref, sm_ref, wau_ref, ba_ref, gain_ref,
                       o_ref, sfin_ref, s_ref, *, dk):
    c = pl.program_id(2)
    C = GLA_CHUNK

    @pl.when(c == 0)
    def _():
        s_ref[...] = jnp.zeros_like(s_ref)
        o_ref[...] = jnp.zeros_like(o_ref)

    @pl.when(c > 0)
    def _():
        q = q_ref[...] * (dk ** -0.5)
        k = k_ref[...]
        v = v_ref[...]
        vb = v.astype(BF16)
        g = _log_alpha(sm_ref[...], wau_ref, ba_ref)
        row = lax.broadcasted_iota(I32, g.shape, 0)
        b = g
        sh = 1
        while sh < C:
            b = b + jnp.where(row >= sh, pltpu.roll(b, sh, 0), 0.0)
            sh *= 2
        bl = b[C - 1:C, :]
        S = s_ref[...]
        o = jnp.dot((q * jnp.exp(b)).astype(BF16), S.astype(BF16), preferred_element_type=F32)

        tcol = lax.broadcasted_iota(I32, (GLA_SUB, C), 1)
        trow = lax.broadcasted_iota(I32, (GLA_SUB, C), 0)
        a_rows = []
        for blk in range(C // GLA_SUB):
            r0 = blk * GLA_SUB
            ref = b[r0 - 1:r0, :] if blk > 0 else jnp.zeros_like(bl)
            b_i = b[r0:r0 + GLA_SUB, :]
            q_i = q[r0:r0 + GLA_SUB, :]
            qe = q_i * jnp.exp(b_i - ref)
            kf = k * jnp.exp(jnp.minimum(ref - b, 0.0))
            a_off = lax.dot_general(qe.astype(BF16), kf.astype(BF16), (((1,), (1,)), ((), ())),
                                    preferred_element_type=F32)
            diag = jnp.zeros((GLA_SUB, C), F32)
            for sl in range(GLA_SUB):
                s = r0 + sl
                w = q_i * k[s:s + 1, :] * jnp.exp(jnp.minimum(b_i - b[s:s + 1, :], 0.0))
                diag = jnp.where(tcol == s, jnp.sum(w, axis=1, keepdims=True), diag)
            a_rows.append(jnp.where(tcol < r0, a_off, jnp.where(tcol <= trow + r0, diag, 0.0)))
        a = jnp.concatenate(a_rows, axis=0)
        o = o + jnp.dot(a.astype(BF16), vb, preferred_element_type=F32)

        kd = k * jnp.exp(bl - b)
        upd = lax.dot_general(kd.astype(BF16), vb, (((0,), (0,)), ((), ())),
                              preferred_element_type=F32)
        s_ref[...] = S * _row_to_col(jnp.exp(bl), dk) + upd
        o_ref[...] = _readout(o, r_ref[...], gain_ref[...]).astype(BF16)

    @pl.when(c == pl.num_programs(2) - 1)
    def _():
        sfin_ref[...] = s_ref[...]


def _gla_sample_kernel(alias_ref, q_ref, k_ref, v_ref, r_ref, sm_ref, wau_ref, ba_ref, gain_ref,
                       st_ref, o_ref, snew_ref, acc_ref, *, dk):
    del alias_ref
    d = pl.program_id(1)
    q = q_ref[...] * (dk ** -0.5)
    k = k_ref[...]
    v = v_ref[...]
    r = r_ref[...]
    g = _log_alpha(jnp.broadcast_to(sm_ref[...], (8, LANE)), wau_ref, ba_ref)[0:1, :]
    eg = jnp.exp(g)
    S = st_ref[...]
    qe = jnp.broadcast_to(q * eg, (8, dk))
    o = jnp.dot(qe.astype(BF16), S.astype(BF16), preferred_element_type=F32)[0:1, :]
    o = o + jnp.sum(q * k, axis=1, keepdims=True) * v
    snew_ref[...] = S * _row_to_col(eg, dk) + _row_to_col(k, dk) * v
    on = _readout(o, r, gain_ref[...])
    row = lax.broadcasted_iota(I32, acc_ref.shape, 0)

    @pl.when(d == 0)
    def _():
        acc_ref[...] = jnp.zeros_like(acc_ref)

    acc_ref[...] = jnp.where(row == d, on, acc_ref[...])

    @pl.when(d == pl.num_programs(1) - 1)
    def _():
        o_ref[...] = acc_ref[...].astype(BF16)


def _dsa_prompt_kernel(qi_ref, smq_ref, smk_ref, qb_ref, kb_ref, vb_ref, o_ref,
                       ka_ref, kbb_ref, kbf_ref, vbf_ref, key_ref, *, k_sel, hd, widths):
    qb = pl.program_id(1)
    TQ = qi_ref.shape[0]

    @pl.when(qb == 0)
    def _():
        smk = smk_ref[...]
        lane = lax.broadcasted_iota(I32, (1, LANE), 1)
        ka_ref[...] = jnp.where(lane < IDX_DIM, smk, 0.0).astype(BF16)
        kbb_ref[...] = jnp.where(lane >= IDX_DIM, pltpu.roll(smk, IDX_DIM, 1), 0.0).astype(BF16)
        kbf_ref[...] = kb_ref[...].astype(BF16)
        vbf_ref[...] = vb_ref[...].astype(BF16)

    nt = (((1,), (1,)), ((), ()))
    G = ATT_HEADS // ATT_KV_HEADS

    def body(TK):
        score = jnp.zeros((TQ, TK), F32)
        for p in range(IDX_HEADS // 2):
            qp = qi_ref[:, p * LANE:(p + 1) * LANE].astype(BF16)
            for half, kref in ((0, ka_ref), (1, kbb_ref)):
                h = 2 * p + half
                s = lax.dot_general(qp, kref[0:TK, :], nt, preferred_element_type=F32)
                w = smq_ref[:, WI_OFF + h:WI_OFF + h + 1] * (IDX_HEADS ** -0.5 * IDX_DIM ** -0.5)
                score = score + w * jnp.maximum(s, 0.0)

        qrow = qb * TQ + lax.broadcasted_iota(I32, (TQ, 1), 0)
        kcol = lax.broadcasted_iota(I32, (1, TK), 1)
        adm = jnp.logical_and(kcol <= qrow, kcol >= ROW0)
        key_ref[:, 0:TK] = jnp.where(adm, _sortable(score), INT_MIN)
        thr = jnp.maximum(_kth_largest_key(lambda: key_ref[:, 0:TK], TQ, k_sel), INT_MIN + 1)
        sel = key_ref[:, 0:TK] >= thr

        for n in range(ATT_KV_HEADS):
            q4 = jnp.concatenate(
                [(qb_ref[:, (n * G + gq) * hd:(n * G + gq + 1) * hd] * (hd ** -0.5)).astype(BF16)
                 for gq in range(G)], axis=0)
            s = lax.dot_general(q4, kbf_ref[0:TK, n * hd:(n + 1) * hd], nt, preferred_element_type=F32)
            s = jnp.where(sel[None], s.reshape(G, TQ, TK), -jnp.inf).reshape(G * TQ, TK)
            m = jnp.max(s, axis=1, keepdims=True)
            m = jnp.where(m == -jnp.inf, 0.0, m)
            p = jnp.exp(s - m)
            l = jnp.sum(p, axis=1, keepdims=True)
            o = jnp.dot(p.astype(BF16), vbf_ref[0:TK, n * hd:(n + 1) * hd], preferred_element_type=F32)
            o = jnp.where(l > 0.0, o / l, 0.0)
            for gq in range(G):
                o_ref[:, (n * G + gq) * hd:(n * G + gq + 1) * hd] = o[gq * TQ:(gq + 1) * TQ, :].astype(BF16)

    for c, width in enumerate(widths):
        pl.when(jnp.logical_and(qb >= c * QB_PER_CLASS, qb < (c + 1) * QB_PER_CLASS))(
            functools.partial(body, width))


def _idx_page_copy(cache_ref, buf_ref, sem, page, slot, p):
    return pltpu.make_async_copy(cache_ref.at[page], buf_ref.at[slot, p], sem.at[slot])


def _dsa_sample_score_kernel(pt_ref, cache_ref, qi_ref, wi_ref, kin_ref, sc_ref, scn_ref,
                             buf_ref, sem, *, n_pages):
    d = pl.program_id(0)
    slot = d % 2

    def fetch(dd, sl):
        def start(p, carry):
            _idx_page_copy(cache_ref, buf_ref, sem, pt_ref[dd, p], sl, p).start()
            return carry
        lax.fori_loop(0, n_pages, start, 0)

    @pl.when(d == 0)
    def _():
        fetch(0, 0)

    @pl.when(d + 1 < pl.num_programs(0))
    def _():
        fetch(d + 1, 1 - slot)

    def wait(p, carry):
        _idx_page_copy(cache_ref, buf_ref, sem, 0, slot, p).wait()
        return carry

    lax.fori_loop(0, n_pages, wait, 0)
    nt = (((1,), (1,)), ((), ()))
    qi = qi_ref[...].astype(BF16)
    w = wi_ref[...] * (IDX_HEADS ** -0.5 * IDX_DIM ** -0.5)
    kp = buf_ref[slot].astype(BF16)
    qib = jnp.broadcast_to(qi[None], (n_pages, IDX_HEADS, IDX_DIM))
    s = lax.dot_general(qib, kp, (((2,), (1,)), ((0,), (0,))), preferred_element_type=F32)
    sc_ref[...] = jnp.sum(w[None] * jnp.maximum(s, 0.0), axis=1)
    kn = jnp.broadcast_to(kin_ref[...], (8, IDX_DIM)).astype(BF16)
    sn = lax.dot_general(qi, kn, nt, preferred_element_type=F32)[:, 0:1]
    scn_ref[...] = jnp.broadcast_to(jnp.sum(w * jnp.maximum(sn, 0.0), axis=0, keepdims=True), (1, LANE))


def _dsa_sample_select_kernel(sc_ref, scn_ref, hl_ref, sel_ref, pos_ref, key_ref, *, k_sel, chunk):
    DB, NP = sc_ref.shape
    nblk = NP // LANE
    lane1 = lax.broadcasted_iota(I32, (1, LANE), 1)
    key_ref[:, 0:NP] = _sortable(sc_ref[...])
    key_ref[:, NP:] = jnp.where(lane1 == 0, _sortable(scn_ref[...]), INT_MIN)
    thr = _kth_largest_key(lambda: key_ref[...], DB, k_sel)
    keys = key_ref[...]
    gt = keys > thr
    eq = keys == thr
    n_gt = jnp.sum(gt.astype(I32), axis=1, keepdims=True).astype(F32)

    iu = lax.broadcasted_iota(I32, (LANE, LANE), 0)
    ju = lax.broadcasted_iota(I32, (LANE, LANE), 1)
    upper = jnp.where(iu < ju, 1.0, 0.0).astype(BF16)
    ones = jnp.ones((LANE, LANE), BF16)

    def excl_prefix(mask):
        mb = jnp.where(mask, 1.0, 0.0).astype(BF16)
        stacked = jnp.concatenate([mb[:, c * LANE:(c + 1) * LANE] for c in range(nblk + 1)], axis=0)
        within = jnp.dot(stacked, upper, preferred_element_type=F32)
        tot = jnp.dot(stacked, ones, preferred_element_type=F32)[:, 0:1]
        outs = []
        run = jnp.zeros((DB, 1), F32)
        for c in range(nblk + 1):
            outs.append(within[c * DB:(c + 1) * DB, :] + run)
            run = run + tot[c * DB:(c + 1) * DB, :]
        return jnp.concatenate(outs, axis=1)

    pos_gt = excl_prefix(gt)
    pos_eq = excl_prefix(eq) + n_gt
    keep_eq = jnp.logical_and(eq, pos_eq < k_sel)
    pos = jnp.where(gt, pos_gt, jnp.where(keep_eq, pos_eq, -1.0))
    for d in range(DB):
        pos_ref[d] = pos[d:d + 1, :]

    jrow = lax.broadcasted_iota(I32, (k_sel, 1), 0).astype(F32)
    width = NP + LANE
    def compact(d, carry):
        acc = jnp.zeros((k_sel, LANE), F32)
        for c0 in range(0, width, chunk):
            c1 = min(c0 + chunk, width)
            e = jnp.where(pos_ref[d, :, c0:c1] == jrow, 1.0, 0.0).astype(BF16)
            acc = acc + jnp.dot(e, hl_ref[c0:c1, :], preferred_element_type=F32)
        sel_ref[d] = acc.astype(I32)
        return carry

    lax.fori_loop(0, DB, compact, 0)


def _kv_copies(ck_ref, cv_ref, kbuf_ref, vbuf_ref, sem, page, slot, buf, j):
    ck = pltpu.make_async_copy(ck_ref.at[page, slot], kbuf_ref.at[buf, j], sem.at[0, buf])
    cv = pltpu.make_async_copy(cv_ref.at[page, slot], vbuf_ref.at[buf, j], sem.at[1, buf])
    return ck, cv


def _dsa_sample_attend_kernel(sel_s_ref, pt_ref, alias_ref, ck_ref, cv_ref, selv_ref, q_ref,
                              kn_ref, vn_ref, o_ref, kbuf_ref, vbuf_ref, acc_ref, sem,
                              *, k_sel, n_pages, hd):
    del alias_ref
    d = pl.program_id(0)
    buf = d % 2

    def fetch(dd, bb):
        def start(j, carry):
            page = jnp.minimum(sel_s_ref[dd, 2 * j], n_pages - 1)
            ck, cv = _kv_copies(ck_ref, cv_ref, kbuf_ref, vbuf_ref, sem, pt_ref[dd, page],
                                sel_s_ref[dd, 2 * j + 1], bb, j)
            ck.start()
            cv.start(priority=1)
            return carry
        lax.fori_loop(0, k_sel, start, 0)

    @pl.when(d == 0)
    def _():
        fetch(0, 0)

    @pl.when(d + 1 < pl.num_programs(0))
    def _():
        fetch(d + 1, 1 - buf)

    def wait(j, carry):
        ck, cv = _kv_copies(ck_ref, cv_ref, kbuf_ref, vbuf_ref, sem, 0, 0, buf, j)
        ck.wait()
        cv.wait()
        return carry

    lax.fori_loop(0, k_sel, wait, 0)
    is_new = selv_ref[:, 0:1] >= n_pages
    G = ATT_HEADS // ATT_KV_HEADS
    nt = (((1,), (1,)), ((), ()))
    q = q_ref[...] * (hd ** -0.5)
    outs = []
    for n in range(ATT_KV_HEADS):
        hs = slice(n * hd, (n + 1) * hd)
        kk = jnp.where(is_new, kn_ref[:, hs], kbuf_ref[buf, :, n, :]).astype(BF16)
        vv = jnp.where(is_new, vn_ref[:, hs], vbuf_ref[buf, :, n, :]).astype(BF16)
        qn = jnp.concatenate([q[n * G:(n + 1) * G, :], jnp.zeros((8 - G, hd), F32)], axis=0).astype(BF16)
        s = lax.dot_general(qn, kk, nt, preferred_element_type=F32)
        m = jnp.max(s, axis=1, keepdims=True)
        p = jnp.exp(s - m)
        l = jnp.sum(p, axis=1, keepdims=True)
        o = jnp.dot(p.astype(BF16), vv, preferred_element_type=F32) / l
        outs.append(o[0:G, :])
    acc_ref[pl.ds(d, 1)] = jnp.concatenate(outs, axis=0)[None]

    @pl.when(d == pl.num_programs(0) - 1)
    def _():
        for h in range(ATT_HEADS):
            o_ref[:, h * hd:(h + 1) * hd] = acc_ref[:, h, :].astype(BF16)


def _mix_kernel(a_ref, b_ref, wa_ref, wb_ref, ga_ref, gb_ref, o_ref):
    ba = jnp.dot(a_ref[...], wa_ref[...], preferred_element_type=F32)
    bb = jnp.dot(b_ref[...], wb_ref[...], preferred_element_type=F32)
    o_ref[...] = (_sigmoid(ga_ref[...]) * ba + _sigmoid(gb_ref[...]) * bb).astype(BF16)


def _outproj_kernel(x_ref, m_ref, w_ref, g_ref, x1_ref, h_ref):
    x1 = x_ref[...] + jnp.dot(m_ref[...], w_ref[...], preferred_element_type=F32)
    x1_ref[...] = x1
    ms = jnp.mean(x1 * x1, axis=-1, keepdims=True)
    h_ref[...] = (x1 * lax.rsqrt(ms + EPS) * g_ref[...]).astype(BF16)


def _ffn_kernel(h_ref, halo_ref, x1_ref, wa_ref, wb_ref, wd_ref, cw_ref, cb_ref, st0_ref, st1_ref,
                gf_ref, y_ref, head_ref, tail_ref, acc_ref, s1_ref, s2_ref, *, n_dec):
    i = pl.program_id(0)
    j = pl.program_id(1)
    h = h_ref[...]
    a = jnp.dot(h, wa_ref[...], preferred_element_type=F32)
    b = jnp.dot(h, wb_ref[...], preferred_element_type=F32)
    ah = jnp.dot(halo_ref[...], wa_ref[...], preferred_element_type=F32)
    ah = jnp.where(i > 0, ah, 0.0)
    row = lax.broadcasted_iota(I32, a.shape, 0)
    hl = ah.shape[0]
    s1_ref[...] = jnp.where(row == 0, ah[hl - 1:hl, :], pltpu.roll(a, 1, 0))
    s2_ref[...] = jnp.where(row == 0, ah[hl - 2:hl - 1, :],
                            jnp.where(row == 1, ah[hl - 1:hl, :], pltpu.roll(a, 2, 0)))

    @pl.when(i == 0)
    def _():
        s1_ref[0:n_dec, :] = st1_ref[...]
        s2_ref[0:n_dec, :] = st0_ref[...]

    conv = cb_ref[...] + cw_ref[0:1, :] * s2_ref[...] + cw_ref[1:2, :] * s1_ref[...] + cw_ref[2:3, :] * a
    gate = (conv * _sigmoid(conv) * b).astype(BF16)
    part = jnp.dot(gate, wd_ref[...], preferred_element_type=F32)

    @pl.when(j == 0)
    def _():
        acc_ref[...] = part

    @pl.when(j > 0)
    def _():
        acc_ref[...] = acc_ref[...] + part

    head_ref[...] = a[0:head_ref.shape[0], :]
    tail_ref[...] = a[a.shape[0] - 8:, :]

    @pl.when(j == pl.num_programs(1) - 1)
    def _():
        x2 = x1_ref[...] + acc_ref[...]
        ms = jnp.mean(x2 * x2, axis=-1, keepdims=True)
        y_ref[...] = x2 * lax.rsqrt(ms + EPS) * gf_ref[...]


def _rope_tables(pos, hd, reps):
    rot = hd // 4
    half = rot // 2
    inv = jnp.exp(-math.log(ROPE_THETA) * jnp.arange(half, dtype=F32) * 2.0 / rot)
    ang = pos.astype(F32)[:, None] * inv[None, :]
    cos, sin = jnp.cos(ang), jnp.sin(ang)
    n = pos.shape[0]
    one = jnp.ones((n, hd - rot), F32)
    zero_r = jnp.zeros((n, hd - rot), F32)
    zero_h = jnp.zeros((n, half), F32)
    c = jnp.concatenate([cos, cos, one], axis=1)
    sa = jnp.concatenate([-sin, zero_h, zero_r], axis=1)
    sb = jnp.concatenate([zero_h, sin, zero_r], axis=1)
    return [jnp.tile(t, (1, reps)) for t in (c, sa, sb)]


def kernel(x_prompt, x_sample, cache_k, cache_v, cache_idx_k, state_gla, state_conv, page_table, meta_tokens, norm_mix_g, w_in, w_alpha_up, b_alpha, gla_norm_g, w_branch_a, w_branch_b, w_out, norm_ffn_g, w_up, conv_w, conv_b, w_down, norm_final_g):
    B, SEQ, D = x_prompt.shape
    DB = x_sample.shape[0]
    assert x_sample.shape[1] == 1 and w_in.shape[0] == 1
    n_pool = cache_k.shape[1]
    n_pages = page_table.shape[1]
    past = n_pages * PAGE_SIZE
    assert n_pages == LANE and PAGE_SIZE == LANE
    dff = w_down.shape[1]
    H = GLA_HEADS
    dk = D // 2 // H
    dv = D // H
    hd = D // ATT_HEADS
    kvw = ATT_KV_HEADS * hd
    T = SEQ + N_META
    TP = SEQ + FRONT
    R = B * TP
    assert DB <= GLA_CHUNK and DB % 16 == 0 and SEQ % LANE == 0 and dff % COL_TILE == 0
    k_sel_p = min(TOPK_MAX, T // 4)
    k_sel_s = min(TOPK_MAX, (past + 1) // 4)

    sizes = (H * dk, H * dk, H * dv, H * dv, GLA_GATE_RANK, ATT_HEADS * hd, kvw, kvw,
             IDX_HEADS * IDX_DIM, IDX_DIM, IDX_HEADS, D, D)
    offs = [0]
    for s_ in sizes:
        offs.append(offs[-1] + s_)
    w0 = w_in[0]
    wg_a = w0[:, offs[0]:offs[4]].astype(BF16)
    wg_b = w0[:, offs[5]:offs[9]].astype(BF16)
    wg_c = w0[:, offs[11]:offs[13]].astype(BF16)
    small_pad = COL_TILE - (IDX_DIM + GLA_GATE_RANK + IDX_HEADS)
    wg_s = jnp.concatenate([w0[:, offs[9]:offs[10]], w0[:, offs[4]:offs[5]], w0[:, offs[10]:offs[11]],
                            jnp.zeros((D, small_pad), F32)], axis=1).astype(BF16)
    n_a, n_b, n_c = (wg_a.shape[1] // COL_TILE, wg_b.shape[1] // COL_TILE, wg_c.shape[1] // COL_TILE)
    assert all(w_.shape[1] % COL_TILE == 0 for w_ in (wg_a, wg_b, wg_c)) and hd == LANE
    QA, KA, VA, RA = 0, sizes[0], sizes[0] + sizes[1], sizes[0] + sizes[1] + sizes[2]
    QB = n_a * COL_TILE
    KB, VB, QI = QB + sizes[5], QB + sizes[5] + sizes[6], QB + sizes[5] + sizes[6] + sizes[7]
    GA = (n_a + n_b) * COL_TILE
    GB = GA + D
    SM = (n_a + n_b + n_c) * COL_TILE
    NW = SM + COL_TILE
    assert all(v_ % COL_TILE == 0 for v_ in (QA, KA, VA, RA, QB, KB, VB, QI, GA, GB))
    assert VB - KB == COL_TILE
    n_ct = NW // COL_TILE
    wau_pad = jnp.zeros((LANE, H * dk), F32).at[IDX_DIM:IDX_DIM + GLA_GATE_RANK].set(w_alpha_up[0]).astype(BF16)
    wa_bf = w_branch_a[0].astype(BF16)
    wb_bf = w_branch_b[0].astype(BF16)
    wo_bf = w_out[0].astype(BF16)
    wup_bf = w_up[0].astype(BF16)
    wdn_bf = w_down[0].astype(BF16)

    front = jnp.zeros((B, ROW0, D), F32).at[0, :DB].set(x_sample[:, 0])
    meta = jnp.broadcast_to(meta_tokens[None].astype(F32), (B, N_META, D))
    x_all = jnp.concatenate([front, meta, x_prompt], axis=1).reshape(R, D)
    rpos = jnp.maximum(jnp.arange(TP, dtype=jnp.int32) - ROW0, 0)
    pos = jnp.concatenate([rpos.at[:DB].set(past), rpos])
    tabs = jnp.concatenate(_rope_tables(pos, hd, 1) + _rope_tables(pos, IDX_DIM, LANE // IDX_DIM), axis=1)

    tm1 = TP // 2
    bpb = TP // tm1
    P = pl.pallas_call(
        functools.partial(_inproj_kernel, n_a=n_a, n_b=n_b, n_c=n_c,
                          b128=(0, (VB - QB) // COL_TILE),
                          b64=((QI - QB) // COL_TILE, (QI - QB + IDX_HEADS * IDX_DIM) // COL_TILE)),
        grid=(R // tm1, n_ct),
        in_specs=[pl.BlockSpec((tm1, D), lambda i, j: (i, 0)),
                  pl.BlockSpec((1, D), lambda i, j: (0, 0)),
                  pl.BlockSpec((D, COL_TILE), lambda i, j: (0, jnp.minimum(j, n_a - 1))),
                  pl.BlockSpec((D, COL_TILE), lambda i, j: (0, jnp.clip(j - n_a, 0, n_b - 1))),
                  pl.BlockSpec((D, COL_TILE), lambda i, j: (0, jnp.clip(j - n_a - n_b, 0, n_c - 1))),
                  pl.BlockSpec((D, COL_TILE), lambda i, j: (0, 0)),
                  pl.BlockSpec((tm1, 6 * LANE), lambda i, j: (jnp.where(i < bpb, i, bpb + i % bpb), 0))],
        out_specs=pl.BlockSpec((tm1, COL_TILE), lambda i, j: (i, j)),
        out_shape=jax.ShapeDtypeStruct((R, NW), F32),
        scratch_shapes=[pltpu.VMEM((tm1, D), BF16)],
        compiler_params=_cparams(("parallel", "arbitrary")),
        name="inproj",
    )(x_all, norm_mix_g, wg_a, wg_b, wg_c, wg_s, tabs)

    C = GLA_CHUNK
    ncb = TP // C
    ba2 = b_alpha.reshape(1, H * dk)
    gain2 = gla_norm_g.reshape(1, H * dv)
    smc = SM // LANE
    a_out, s_fin = pl.pallas_call(
        functools.partial(_gla_prompt_kernel, dk=dk),
        grid=(B, H, ncb),
        in_specs=[pl.BlockSpec((C, dk), lambda b, h, c: (b * ncb + c, QA // dk + h)),
                  pl.BlockSpec((C, dk), lambda b, h, c: (b * ncb + c, KA // dk + h)),
                  pl.BlockSpec((C, dv), lambda b, h, c: (b * ncb + c, VA // dv + h)),
                  pl.BlockSpec((C, dv), lambda b, h, c: (b * ncb + c, RA // dv + h)),
                  pl.BlockSpec((C, LANE), lambda b, h, c: (b * ncb + c, smc)),
                  pl.BlockSpec((LANE, dk), lambda b, h, c: (0, h)),
                  pl.BlockSpec((1, dk), lambda b, h, c: (0, h)),
                  pl.BlockSpec((1, dv), lambda b, h, c: (0, h))],
        out_specs=[pl.BlockSpec((C, dv), lambda b, h, c: (b * ncb + c, h)),
                   pl.BlockSpec((None, None, dk, dv), lambda b, h, c: (b, h, 0, 0))],
        out_shape=[jax.ShapeDtypeStruct((R, H * dv), BF16),
                   jax.ShapeDtypeStruct((B, H, dk, dv), F32)],
        scratch_shapes=[pltpu.VMEM((dk, dv), F32)],
        compiler_params=_cparams(("parallel", "parallel", "arbitrary")),
        name="gla_prompt",
    )(P, P, P, P, P, wau_pad, ba2, gain2)

    p_s = P[:DB]
    P_rows = p_s.reshape(DB, 1, NW)
    a_out, s_new_s = pl.pallas_call(
        functools.partial(_gla_sample_kernel, dk=dk),
        grid=(H, DB),
        in_specs=[pl.BlockSpec(memory_space=pl.ANY),
                  pl.BlockSpec((None, 1, dk), lambda h, d: (d, 0, QA // dk + h)),
                  pl.BlockSpec((None, 1, dk), lambda h, d: (d, 0, KA // dk + h)),
                  pl.BlockSpec((None, 1, dv), lambda h, d: (d, 0, VA // dv + h)),
                  pl.BlockSpec((None, 1, dv), lambda h, d: (d, 0, RA // dv + h)),
                  pl.BlockSpec((None, 1, LANE), lambda h, d: (d, 0, smc)),
                  pl.BlockSpec((LANE, dk), lambda h, d: (0, h)),
                  pl.BlockSpec((1, dk), lambda h, d: (0, h)),
                  pl.BlockSpec((1, dv), lambda h, d: (0, h)),
                  pl.BlockSpec((None, None, dk, dv), lambda h, d: (d, h, 0, 0))],
        out_specs=[pl.BlockSpec((DB, dv), lambda h, d: (0, h)),
                   pl.BlockSpec((None, None, dk, dv), lambda h, d: (d, h, 0, 0))],
        out_shape=[jax.ShapeDtypeStruct((R, H * dv), BF16),
                   jax.ShapeDtypeStruct((DB, H, dk, dv), F32)],
        scratch_shapes=[pltpu.VMEM((DB, dv), F32)],
        input_output_aliases={0: 0},
        compiler_params=_cparams(("arbitrary", "arbitrary")),
        name="gla_sample",
    )(a_out, P_rows, P_rows, P_rows, P_rows, P_rows, wau_pad, ba2, gain2, state_gla[0])

    TQ = LANE
    nqb = TP // TQ
    n_cls = -(-nqb // QB_PER_CLASS)
    widths = tuple(min((c + 1) * QB_PER_CLASS * TQ, TP) for c in range(n_cls))
    o_b = pl.pallas_call(
        functools.partial(_dsa_prompt_kernel, k_sel=k_sel_p, hd=hd, widths=widths),
        grid=(B, nqb),
        in_specs=[pl.BlockSpec((TQ, IDX_HEADS * IDX_DIM), lambda b, q: (b * nqb + q, QI // (IDX_HEADS * IDX_DIM))),
                  pl.BlockSpec((TQ, LANE), lambda b, q: (b * nqb + q, smc)),
                  pl.BlockSpec((TP, LANE), lambda b, q: (b, smc)),
                  pl.BlockSpec((TQ, ATT_HEADS * hd), lambda b, q: (b * nqb + q, QB // (ATT_HEADS * hd))),
                  pl.BlockSpec((TP, kvw), lambda b, q: (b, KB // kvw)),
                  pl.BlockSpec((TP, kvw), lambda b, q: (b, VB // kvw))],
        out_specs=pl.BlockSpec((TQ, ATT_HEADS * hd), lambda b, q: (b * nqb + q, 0)),
        out_shape=jax.ShapeDtypeStruct((R, ATT_HEADS * hd), BF16),
        scratch_shapes=[pltpu.VMEM((TP, LANE), BF16), pltpu.VMEM((TP, LANE), BF16),
                        pltpu.VMEM((TP, kvw), BF16), pltpu.VMEM((TP, kvw), BF16),
                        pltpu.VMEM((TQ, TP), I32)],
        compiler_params=_cparams(("parallel", "arbitrary")),
        name="dsa_prompt",
    )(P, P, P, P, P, P)

    qi_s = p_s[:, QI:QI + IDX_HEADS * IDX_DIM].reshape(DB, IDX_HEADS, IDX_DIM)
    wi_s = p_s[:, SM + WI_OFF:SM + WI_OFF + IDX_HEADS].reshape(DB, IDX_HEADS, 1)
    ki_s = p_s[:, SM:SM + IDX_DIM].reshape(DB, 1, IDX_DIM)
    sc, scn = pl.pallas_call(
        functools.partial(_dsa_sample_score_kernel, n_pages=n_pages),
        grid_spec=pltpu.PrefetchScalarGridSpec(
            num_scalar_prefetch=1,
            grid=(DB,),
            in_specs=[pl.BlockSpec(memory_space=pl.ANY),
                      pl.BlockSpec((None, IDX_HEADS, IDX_DIM), lambda d, pt: (d, 0, 0)),
                      pl.BlockSpec((None, IDX_HEADS, 1), lambda d, pt: (d, 0, 0)),
                      pl.BlockSpec((None, 1, IDX_DIM), lambda d, pt: (d, 0, 0))],
            out_specs=[pl.BlockSpec((None, n_pages, PAGE_SIZE), lambda d, pt: (d, 0, 0)),
                       pl.BlockSpec((None, 1, LANE), lambda d, pt: (d, 0, 0))],
            scratch_shapes=[pltpu.VMEM((2, n_pages, IDX_DIM, PAGE_SIZE), F32),
                            pltpu.SemaphoreType.DMA((2,))]),
        out_shape=[jax.ShapeDtypeStruct((DB, n_pages, PAGE_SIZE), F32),
                   jax.ShapeDtypeStruct((DB, 1, LANE), F32)],
        compiler_params=_cparams(("arbitrary",)),
        name="dsa_sample_score",
    )(page_table, jnp.swapaxes(cache_idx_k[0], 1, 2), qi_s, wi_s, ki_s)

    sidx = jnp.arange(past + LANE, dtype=jnp.int32)
    hl = jnp.zeros((past + LANE, LANE), F32).at[:, 0].set((sidx // PAGE_SIZE).astype(F32))
    hl = hl.at[:, 1].set((sidx % PAGE_SIZE).astype(F32)).astype(BF16)
    sel = pl.pallas_call(
        functools.partial(_dsa_sample_select_kernel, k_sel=k_sel_s, chunk=2048),
        out_shape=jax.ShapeDtypeStruct((DB, k_sel_s, LANE), I32),
        scratch_shapes=[pltpu.VMEM((DB, 1, past + LANE), F32), pltpu.VMEM((DB, past + LANE), I32)],
        compiler_params=pltpu.CompilerParams(vmem_limit_bytes=VMEM_LIMIT),
        name="dsa_sample_select",
    )(sc.reshape(DB, past), scn.reshape(DB, LANE), hl)

    sel_s = sel[:, :, 0:2].reshape(DB, 2 * k_sel_s)
    q_s = p_s[:, QB:QB + ATT_HEADS * hd].reshape(DB, ATT_HEADS, hd)
    kn_s = p_s[:, KB:KB + kvw]
    vn_s = p_s[:, VB:VB + kvw]
    o_b = pl.pallas_call(
        functools.partial(_dsa_sample_attend_kernel, k_sel=k_sel_s, n_pages=n_pages, hd=hd),
        grid_spec=pltpu.PrefetchScalarGridSpec(
            num_scalar_prefetch=2,
            grid=(DB,),
            in_specs=[pl.BlockSpec(memory_space=pl.ANY),
                      pl.BlockSpec(memory_space=pl.ANY),
                      pl.BlockSpec(memory_space=pl.ANY),
                      pl.BlockSpec((None, k_sel_s, LANE), lambda d, s_, pt: (d, 0, 0)),
                      pl.BlockSpec((None, ATT_HEADS, hd), lambda d, s_, pt: (d, 0, 0)),
                      pl.BlockSpec((None, 1, kvw), lambda d, s_, pt: (d, 0, KB // kvw)),
                      pl.BlockSpec((None, 1, kvw), lambda d, s_, pt: (d, 0, VB // kvw))],
            out_specs=pl.BlockSpec((DB, ATT_HEADS * hd), lambda d, s_, pt: (0, 0)),
            scratch_shapes=[pltpu.VMEM((2, k_sel_s, ATT_KV_HEADS, hd), F32),
                            pltpu.VMEM((2, k_sel_s, ATT_KV_HEADS, hd), F32),
                            pltpu.VMEM((DB, ATT_HEADS, hd), F32), pltpu.SemaphoreType.DMA((2, 2))]),
        out_shape=jax.ShapeDtypeStruct((R, ATT_HEADS * hd), BF16),
        input_output_aliases={2: 0},
        compiler_params=_cparams(("arbitrary",)),
        name="dsa_sample_attend",
    )(sel_s, page_table, o_b, cache_k[0], cache_v[0], sel, q_s, P_rows, P_rows)

    mix = pl.pallas_call(
        _mix_kernel,
        grid=(R // tm1, D // COL_TILE),
        in_specs=[pl.BlockSpec((tm1, H * dv), lambda i, j: (i, 0)),
                  pl.BlockSpec((tm1, ATT_HEADS * hd), lambda i, j: (i, 0)),
                  pl.BlockSpec((H * dv, COL_TILE), lambda i, j: (0, j)),
                  pl.BlockSpec((ATT_HEADS * hd, COL_TILE), lambda i, j: (0, j)),
                  pl.BlockSpec((tm1, COL_TILE), lambda i, j: (i, GA // COL_TILE + j)),
                  pl.BlockSpec((tm1, COL_TILE), lambda i, j: (i, GB // COL_TILE + j))],
        out_specs=pl.BlockSpec((tm1, COL_TILE), lambda i, j: (i, j)),
        out_shape=jax.ShapeDtypeStruct((R, D), BF16),
        compiler_params=_cparams(("parallel", "arbitrary")),
        name="mix",
    )(a_out, o_b, wa_bf, wb_bf, P, P)

    tm2 = TP // 4
    x1, h2 = pl.pallas_call(
        _outproj_kernel,
        grid=(R // tm2,),
        in_specs=[pl.BlockSpec((tm2, D), lambda i: (i, 0)),
                  pl.BlockSpec((tm2, D), lambda i: (i, 0)),
                  pl.BlockSpec((D, D), lambda i: (0, 0)),
                  pl.BlockSpec((1, D), lambda i: (0, 0))],
        out_specs=[pl.BlockSpec((tm2, D), lambda i: (i, 0)),
                   pl.BlockSpec((tm2, D), lambda i: (i, 0))],
        out_shape=[jax.ShapeDtypeStruct((R, D), F32), jax.ShapeDtypeStruct((R, D), BF16)],
        compiler_params=_cparams(("parallel",)),
        name="outproj",
    )(x_all, mix, wo_bf, norm_ffn_g)

    nrb = R // tm2
    nft = dff // COL_TILE
    y_all, a_head, a_tail = pl.pallas_call(
        functools.partial(_ffn_kernel, n_dec=DB),
        grid=(nrb, nft),
        in_specs=[pl.BlockSpec((tm2, D), lambda i, j: (i, 0)),
                  pl.BlockSpec((16, D), lambda i, j: (jnp.maximum(i * (tm2 // 16) - 1, 0), 0)),
                  pl.BlockSpec((tm2, D), lambda i, j: (i, 0)),
                  pl.BlockSpec((D, COL_TILE), lambda i, j: (0, j)),
                  pl.BlockSpec((D, COL_TILE), lambda i, j: (0, nft + j)),
                  pl.BlockSpec((COL_TILE, D), lambda i, j: (j, 0)),
                  pl.BlockSpec((CONV_W, COL_TILE), lambda i, j: (0, j)),
                  pl.BlockSpec((1, COL_TILE), lambda i, j: (0, j)),
                  pl.BlockSpec((DB, COL_TILE), lambda i, j: (0, j)),
                  pl.BlockSpec((DB, COL_TILE), lambda i, j: (0, j)),
                  pl.BlockSpec((1, D), lambda i, j: (0, 0))],
        out_specs=[pl.BlockSpec((tm2, D), lambda i, j: (i, 0)),
                   pl.BlockSpec((DB, COL_TILE), lambda i, j: (i, j)),
                   pl.BlockSpec((8, COL_TILE), lambda i, j: (i, j))],
        out_shape=[jax.ShapeDtypeStruct((R, D), F32),
                   jax.ShapeDtypeStruct((nrb * DB, dff), F32),
                   jax.ShapeDtypeStruct((nrb * 8, dff), F32)],
        scratch_shapes=[pltpu.VMEM((tm2, D), F32), pltpu.VMEM((tm2, COL_TILE), F32),
                        pltpu.VMEM((tm2, COL_TILE), F32)],
        compiler_params=_cparams(("parallel", "arbitrary")),
        name="ffn",
    )(h2, h2, x1, wup_bf, wup_bf, wdn_bf, conv_w[0], conv_b, state_conv[0, :, 0], state_conv[0, :, 1],
      norm_final_g.reshape(1, D))

    y3 = y_all.reshape(B, TP, D)
    y_prompt = y3[:, FRONT:]
    y_sample = y_all[:DB].reshape(DB, 1, D)
    P3 = P.reshape(B, TP, NW)
    new_k_p = P3[:, ROW0:, KB:KB + kvw].reshape(1, B, T, ATT_KV_HEADS, hd)
    new_v_p = P3[:, ROW0:, VB:VB + kvw].reshape(1, B, T, ATT_KV_HEADS, hd)
    new_ki_p = P3[:, ROW0:, SM:SM + IDX_DIM].reshape(1, B, T, IDX_DIM)
    new_gla_p = s_fin[None]
    tails = a_tail.reshape(B, nrb // B, 8, dff)[:, -1, 8 - (CONV_W - 1):, :]
    new_conv_p = tails[None]
    new_k_s = kn_s.reshape(1, DB, 1, ATT_KV_HEADS, hd)
    new_v_s = vn_s.reshape(1, DB, 1, ATT_KV_HEADS, hd)
    new_ki_s = ki_s.reshape(1, DB, 1, IDX_DIM)
    new_gla_s = s_new_s[None]
    new_conv_s = jnp.stack([state_conv[0, :, 1], a_head[:DB]], axis=1)[None]
    return (y_prompt, y_sample, new_k_p, new_v_p, new_ki_p, new_gla_p, new_conv_p,
            new_k_s, new_v_s, new_ki_s, new_gla_s, new_conv_s)
```

```python
import functools
import math

import jax
import jax.numpy as jnp
from jax import lax
from jax.experimental import pallas as pl
from jax.experimental.pallas import tpu as pltpu

F32 = jnp.float32
BF16 = jnp.bfloat16
I32 = jnp.int32

N_META = 16
GLA_HEADS = 4
GLA_GATE_RANK = 16
GLA_GATE_NORM = 16.0
GLA_CHUNK = 64
GLA_SUB = 16
ATT_HEADS = 16
ATT_KV_HEADS = 4
IDX_HEADS = 16
IDX_DIM = 64
TOPK_MAX = 256
ROPE_THETA = 500000.0
CONV_W = 3
EPS = 1e-6
PAGE_SIZE = 128
WI_OFF = IDX_DIM + GLA_GATE_RANK

LANE = 128
FRONT = 128
ROW0 = FRONT - N_META
COL_TILE = 512
QB_PER_CLASS = 4
VMEM_LIMIT = 56 * 1024 * 1024
INT_MIN = -2 ** 31


def _cparams(sem):
    return pltpu.CompilerParams(dimension_semantics=sem, vmem_limit_bytes=VMEM_LIMIT)


def _sigmoid(x):
    return 1.0 / (1.0 + jnp.exp(-x))


def _sortable(x):
    i = pltpu.bitcast(x, I32)
    return jnp.where(i < 0, i ^ jnp.int32(0x7FFFFFFF), i)


def _kth_largest_key(load_keys, rows, k_sel):
    def body(it, t):
        bit = 31 - it
        cand = t + jnp.left_shift(jnp.int32(1), bit)
        cnt = jnp.sum((load_keys() >= cand).astype(I32), axis=1, keepdims=True)
        return jnp.where(cnt >= k_sel, cand, t)

    return lax.fori_loop(0, 32, body, jnp.full((rows, 1), INT_MIN, I32))


def _rope_block(xb, c, sa, sb, half):
    return xb * c + pltpu.roll(xb, LANE - half, 1) * sa + pltpu.roll(xb, half, 1) * sb


def _inproj_kernel(x_ref, g_ref, wa_ref, wb_ref, wc_ref, ws_ref, tab_ref, o_ref, h_ref,
                   *, n_a, n_b, n_c, b128, b64):
    j = pl.program_id(1)

    @pl.when(j == 0)
    def _():
        x = x_ref[...]
        ms = jnp.mean(x * x, axis=-1, keepdims=True)
        h_ref[...] = (x * lax.rsqrt(ms + EPS) * g_ref[...]).astype(BF16)

    def proj(w_ref):
        return jnp.dot(h_ref[...], w_ref[...], preferred_element_type=F32)

    @pl.when(j < n_a)
    def _():
        o_ref[...] = proj(wa_ref)

    @pl.when(jnp.logical_and(j >= n_a, j < n_a + n_b))
    def _():
        acc = proj(wb_ref)
        jb = j - n_a
        is128 = jnp.logical_and(jb >= b128[0], jb < b128[1])
        is64 = jnp.logical_and(jb >= b64[0], jb < b64[1])

        @pl.when(is128)
        def _():
            c, sa, sb = tab_ref[:, 0:128], tab_ref[:, 128:256], tab_ref[:, 256:384]
            for blk in range(COL_TILE // LANE):
                sl = slice(blk * LANE, (blk + 1) * LANE)
                o_ref[:, sl] = _rope_block(acc[:, sl], c, sa, sb, 16)

        @pl.when(is64)
        def _():
            c, sa, sb = tab_ref[:, 384:512], tab_ref[:, 512:640], tab_ref[:, 640:768]
            for blk in range(COL_TILE // LANE):
                sl = slice(blk * LANE, (blk + 1) * LANE)
                o_ref[:, sl] = _rope_block(acc[:, sl], c, sa, sb, 8)

        @pl.when(jnp.logical_not(is128 | is64))
        def _():
            o_ref[...] = acc

    @pl.when(jnp.logical_and(j >= n_a + n_b, j < n_a + n_b + n_c))
    def _():
        o_ref[...] = proj(wc_ref)

    @pl.when(j == n_a + n_b + n_c)
    def _():
        acc = proj(ws_ref)
        lane = lax.broadcasted_iota(I32, (1, LANE), 1)
        first = lane < IDX_DIM
        c = jnp.where(first, tab_ref[:, 384:512], 1.0)
        sa = jnp.where(first, tab_ref[:, 512:640], 0.0)
        sb = jnp.where(first, tab_ref[:, 640:768], 0.0)
        o_ref[:, 0:LANE] = _rope_block(acc[:, 0:LANE], c, sa, sb, 8)
        o_ref[:, LANE:] = acc[:, LANE:]


def _log_alpha(sm, wau_ref, ba_ref):
    x = jnp.dot(sm.astype(BF16), wau_ref[...], preferred_element_type=F32) + ba_ref[...]
    return (jnp.minimum(x, 0.0) - jnp.log(1.0 + jnp.exp(-jnp.abs(x)))) * (1.0 / GLA_GATE_NORM)


def _row_to_col(row, n):
    eye = lax.broadcasted_iota(I32, (n, n), 0) == lax.broadcasted_iota(I32, (n, n), 1)
    return jnp.sum(jnp.where(eye, row, 0.0), axis=1, keepdims=True)


def _readout(o, r, gain):
    ms = jnp.mean(o * o, axis=-1, keepdims=True)
    return o * lax.rsqrt(ms + EPS) * gain * (r * _sigmoid(r))


def _gla_prompt_kernel(q_ref, k_ref, v_ref, r_ref, sm_ref, wau_ref, ba_ref, gain_ref,
                       o_ref, sfin_ref, s_ref, *, dk):
    c = pl.program_id(2)
    C = GLA_CHUNK

    @pl.when(c == 0)
    def _():
        s_ref[...] = jnp.zeros_like(s_ref)
        o_ref[...] = jnp.zeros_like(o_ref)

    @pl.when(c > 0)
    def _():
        q = q_ref[...] * (dk ** -0.5)
        k = k_ref[...]
        v = v_ref[...]
        vb = v.astype(BF16)
        g = _log_alpha(sm_ref[...], wau_ref, ba_ref)
        row = lax.broadcasted_iota(I32, g.shape, 0)
        b = g
        sh = 1
        while sh < C:
            b = b + jnp.where(row >= sh, pltpu.roll(b, sh, 0), 0.0)
            sh *= 2
        bl = b[C - 1:C, :]
        S = s_ref[...]
        o = jnp.dot((q * jnp.exp(b)).astype(BF16), S.astype(BF16), preferred_element_type=F32)

        tcol = lax.broadcasted_iota(I32, (GLA_SUB, C), 1)
        trow = lax.broadcasted_iota(I32, (GLA_SUB, C), 0)
        a_rows = []
        for blk in range(C // GLA_SUB):
            r0 = blk * GLA_SUB
            ref = b[r0 - 1:r0, :] if blk > 0 else jnp.zeros_like(bl)
            b_i = b[r0:r0 + GLA_SUB, :]
            q_i = q[r0:r0 + GLA_SUB, :]
            qe = q_i * jnp.exp(b_i - ref)
            kf = k * jnp.exp(jnp.minimum(ref - b, 0.0))
            a_off = lax.dot_general(qe.astype(BF16), kf.astype(BF16), (((1,), (1,)), ((), ())),
                                    preferred_element_type=F32)
            diag = jnp.zeros((GLA_SUB, C), F32)
            for sl in range(GLA_SUB):
                s = r0 + sl
                w = q_i * k[s:s + 1, :] * jnp.exp(jnp.minimum(b_i - b[s:s + 1, :], 0.0))
                diag = jnp.where(tcol == s, jnp.sum(w, axis=1, keepdims=True), diag)
            a_rows.append(jnp.where(tcol < r0, a_off, jnp.where(tcol <= trow + r0, diag, 0.0)))
        a = jnp.concatenate(a_rows, axis=0)
        o = o + jnp.dot(a.astype(BF16), vb, preferred_element_type=F32)

        kd = k * jnp.exp(bl - b)
        upd = lax.dot_general(kd.astype(BF16), vb, (((0,), (0,)), ((), ())),
                              preferred_element_type=F32)
        s_ref[...] = S * _row_to_col(jnp.exp(bl), dk) + upd
        o_ref[...] = _readout(o, r_ref[...], gain_ref[...]).astype(BF16)

    @pl.when(c == pl.num_programs(2) - 1)
    def _():
        sfin_ref[...] = s_ref[...]


def _gla_sample_kernel(alias_ref, q_ref, k_ref, v_ref, r_ref, sm_ref, wau_ref, ba_ref, gain_ref,
                       st_ref, o_ref, snew_ref, acc_ref, *, dk):
    del alias_ref
    d = pl.program_id(1)
    q = q_ref[...] * (dk ** -0.5)
    k = k_ref[...]
    v = v_ref[...]
    r = r_ref[...]
    g = _log_alpha(jnp.broadcast_to(sm_ref[...], (8, LANE)), wau_ref, ba_ref)[0:1, :]
    eg = jnp.exp(g)
    S = st_ref[...]
    qe = jnp.broadcast_to(q * eg, (8, dk))
    o = jnp.dot(qe.astype(BF16), S.astype(BF16), preferred_element_type=F32)[0:1, :]
    o = o + jnp.sum(q * k, axis=1, keepdims=True) * v
    snew_ref[...] = S * _row_to_col(eg, dk) + _row_to_col(k, dk) * v
    on = _readout(o, r, gain_ref[...])
    row = lax.broadcasted_iota(I32, acc_ref.shape, 0)

    @pl.when(d == 0)
    def _():
        acc_ref[...] = jnp.zeros_like(acc_ref)

    acc_ref[...] = jnp.where(row == d, on, acc_ref[...])

    @pl.when(d == pl.num_programs(1) - 1)
    def _():
        o_ref[...] = acc_ref[...].astype(BF16)


def _dsa_prompt_kernel(qi_ref, smq_ref, smk_ref, qb_ref, kb_ref, vb_ref, o_ref,
                       ka_ref, kbb_ref, kbf_ref, vbf_ref, key_ref, *, k_sel, hd, widths):
    qb = pl.program_id(0)
    TQ = qi_ref.shape[0]
    nt = (((1,), (1,)), ((), ()))
    G = ATT_HEADS // ATT_KV_HEADS

    def body(TK):
        smk = smk_ref[0:TK, :]
        lane = lax.broadcasted_iota(I32, (1, LANE), 1)
        ka_ref[0:TK, :] = jnp.where(lane < IDX_DIM, smk, 0.0).astype(BF16)
        kbb_ref[0:TK, :] = jnp.where(lane >= IDX_DIM, pltpu.roll(smk, IDX_DIM, 1), 0.0).astype(BF16)
        kbf_ref[0:TK, :] = kb_ref[0:TK, :].astype(BF16)
        vbf_ref[0:TK, :] = vb_ref[0:TK, :].astype(BF16)
        score = jnp.zeros((TQ, TK), F32)
        for p in range(IDX_HEADS // 2):
            qp = qi_ref[:, p * LANE:(p + 1) * LANE].astype(BF16)
            for half, kref in ((0, ka_ref), (1, kbb_ref)):
                h = 2 * p + half
                s = lax.dot_general(qp, kref[0:TK, :], nt, preferred_element_type=F32)
                w = smq_ref[:, WI_OFF + h:WI_OFF + h + 1] * (IDX_HEADS ** -0.5 * IDX_DIM ** -0.5)
                score = score + w * jnp.maximum(s, 0.0)

        qrow = qb * TQ + lax.broadcasted_iota(I32, (TQ, 1), 0)
        kcol = lax.broadcasted_iota(I32, (1, TK), 1)
        adm = jnp.logical_and(kcol <= qrow, kcol >= ROW0)
        key_ref[:, 0:TK] = jnp.where(adm, _sortable(score), INT_MIN)
        thr = jnp.maximum(_kth_largest_key(lambda: key_ref[:, 0:TK], TQ, k_sel), INT_MIN + 1)
        sel = key_ref[:, 0:TK] >= thr

        for n in range(ATT_KV_HEADS):
            q4 = jnp.concatenate(
                [(qb_ref[:, (n * G + gq) * hd:(n * G + gq + 1) * hd] * (hd ** -0.5)).astype(BF16)
                 for gq in range(G)], axis=0)
            s = lax.dot_general(q4, kbf_ref[0:TK, n * hd:(n + 1) * hd], nt, preferred_element_type=F32)
            s = jnp.where(sel[None], s.reshape(G, TQ, TK), -jnp.inf).reshape(G * TQ, TK)
            m = jnp.max(s, axis=1, keepdims=True)
            m = jnp.where(m == -jnp.inf, 0.0, m)
            p = jnp.exp(s - m)
            l = jnp.sum(p, axis=1, keepdims=True)
            o = jnp.dot(p.astype(BF16), vbf_ref[0:TK, n * hd:(n + 1) * hd], preferred_element_type=F32)
            o = jnp.where(l > 0.0, o / l, 0.0)
            for gq in range(G):
                o_ref[:, (n * G + gq) * hd:(n * G + gq + 1) * hd] = o[gq * TQ:(gq + 1) * TQ, :].astype(BF16)

    for c, width in enumerate(widths):
        pl.when(jnp.logical_and(qb >= c * QB_PER_CLASS, qb < (c + 1) * QB_PER_CLASS))(
            functools.partial(body, width))


def _idx_page_copy(cache_ref, buf_ref, sem, page, slot, p):
    return pltpu.make_async_copy(cache_ref.at[page], buf_ref.at[slot, p], sem.at[slot])


def _dsa_sample_score_kernel(pt_ref, cache_ref, qi_ref, wi_ref, kin_ref, sc_ref, scn_ref,
                             buf_ref, sem, *, n_pages):
    d = pl.program_id(0)
    slot = d % 2

    def fetch(dd, sl):
        def start(p, carry):
            _idx_page_copy(cache_ref, buf_ref, sem, pt_ref[dd, p], sl, p).start()
            return carry
        lax.fori_loop(0, n_pages, start, 0)

    @pl.when(d == 0)
    def _():
        fetch(0, 0)

    @pl.when(d + 1 < pl.num_programs(0))
    def _():
        fetch(d + 1, 1 - slot)

    def wait(p, carry):
        _idx_page_copy(cache_ref, buf_ref, sem, 0, slot, p).wait()
        return carry

    lax.fori_loop(0, n_pages, wait, 0)
    nt = (((1,), (1,)), ((), ()))
    qi = qi_ref[...].astype(BF16)
    w = wi_ref[...] * (IDX_HEADS ** -0.5 * IDX_DIM ** -0.5)
    kp = buf_ref[slot].astype(BF16)
    qib = jnp.broadcast_to(qi[None], (n_pages, IDX_HEADS, IDX_DIM))
    s = lax.dot_general(qib, kp, (((2,), (1,)), ((0,), (0,))), preferred_element_type=F32)
    sc_ref[...] = jnp.sum(w[None] * jnp.maximum(s, 0.0), axis=1)
    kn = jnp.broadcast_to(kin_ref[...], (8, IDX_DIM)).astype(BF16)
    sn = lax.dot_general(qi, kn, nt, preferred_element_type=F32)[:, 0:1]
    scn_ref[...] = jnp.broadcast_to(jnp.sum(w * jnp.maximum(sn, 0.0), axis=0, keepdims=True), (1, LANE))


def _dsa_sample_select_kernel(sc_ref, scn_ref, hl_ref, sel_ref, pos_ref, key_ref, *, k_sel, chunk):
    DB, NP = sc_ref.shape
    nblk = NP // LANE
    lane1 = lax.broadcasted_iota(I32, (1, LANE), 1)
    key_ref[:, 0:NP] = _sortable(sc_ref[...])
    key_ref[:, NP:] = jnp.where(lane1 == 0, _sortable(scn_ref[...]), INT_MIN)
    thr = _kth_largest_key(lambda: key_ref[...], DB, k_sel)
    keys = key_ref[...]
    gt = keys > thr
    eq = keys == thr
    n_gt = jnp.sum(gt.astype(I32), axis=1, keepdims=True).astype(F32)

    iu = lax.broadcasted_iota(I32, (LANE, LANE), 0)
    ju = lax.broadcasted_iota(I32, (LANE, LANE), 1)
    upper = jnp.where(iu < ju, 1.0, 0.0).astype(BF16)
    ones = jnp.ones((LANE, LANE), BF16)

    def excl_prefix(mask):
        mb = jnp.where(mask, 1.0, 0.0).astype(BF16)
        stacked = jnp.concatenate([mb[:, c * LANE:(c + 1) * LANE] for c in range(nblk + 1)], axis=0)
        within = jnp.dot(stacked, upper, preferred_element_type=F32)
        tot = jnp.dot(stacked, ones, preferred_element_type=F32)[:, 0:1]
        outs = []
        run = jnp.zeros((DB, 1), F32)
        for c in range(nblk + 1):
            outs.append(within[c * DB:(c + 1) * DB, :] + run)
            run = run + tot[c * DB:(c + 1) * DB, :]
        return jnp.concatenate(outs, axis=1)

    pos_gt = excl_prefix(gt)
    pos_eq = excl_prefix(eq) + n_gt
    keep_eq = jnp.logical_and(eq, pos_eq < k_sel)
    pos = jnp.where(gt, pos_gt, jnp.where(keep_eq, pos_eq, -1.0))
    for d in range(DB):
        pos_ref[d] = pos[d:d + 1, :]

    jrow = lax.broadcasted_iota(I32, (k_sel, 1), 0).astype(F32)
    width = NP + LANE
    def compact(d, carry):
        acc = jnp.zeros((k_sel, LANE), F32)
        for c0 in range(0, width, chunk):
            c1 = min(c0 + chunk, width)
            e = jnp.where(pos_ref[d, :, c0:c1] == jrow, 1.0, 0.0).astype(BF16)
            acc = acc + jnp.dot(e, hl_ref[c0:c1, :], preferred_element_type=F32)
        sel_ref[d] = acc.astype(I32)
        return carry

    lax.fori_loop(0, DB, compact, 0)


def _kv_copies(ck_ref, cv_ref, kbuf_ref, vbuf_ref, sem, page, slot, buf, j):
    ck = pltpu.make_async_copy(ck_ref.at[page, slot], kbuf_ref.at[buf, j], sem.at[0, buf])
    cv = pltpu.make_async_copy(cv_ref.at[page, slot], vbuf_ref.at[buf, j], sem.at[1, buf])
    return ck, cv


def _dsa_sample_attend_kernel(sel_s_ref, pt_ref, alias_ref, ck_ref, cv_ref, selv_ref, q_ref,
                              kn_ref, vn_ref, o_ref, kbuf_ref, vbuf_ref, acc_ref, sem,
                              *, k_sel, n_pages, hd):
    del alias_ref
    d = pl.program_id(0)
    buf = d % 2

    def fetch(dd, bb):
        def start(j, carry):
            page = jnp.minimum(sel_s_ref[dd, 2 * j], n_pages - 1)
            ck, cv = _kv_copies(ck_ref, cv_ref, kbuf_ref, vbuf_ref, sem, pt_ref[dd, page],
                                sel_s_ref[dd, 2 * j + 1], bb, j)
            ck.start()
            cv.start(priority=1)
            return carry
        lax.fori_loop(0, k_sel, start, 0)

    @pl.when(d == 0)
    def _():
        fetch(0, 0)

    @pl.when(d + 1 < pl.num_programs(0))
    def _():
        fetch(d + 1, 1 - buf)

    def wait(j, carry):
        ck, cv = _kv_copies(ck_ref, cv_ref, kbuf_ref, vbuf_ref, sem, 0, 0, buf, j)
        ck.wait()
        cv.wait()
        return carry

    lax.fori_loop(0, k_sel, wait, 0)
    is_new = selv_ref[:, 0:1] >= n_pages
    G = ATT_HEADS // ATT_KV_HEADS
    nt = (((1,), (1,)), ((), ()))
    q = q_ref[...] * (hd ** -0.5)
    outs = []
    for n in range(ATT_KV_HEADS):
        hs = slice(n * hd, (n + 1) * hd)
        kk = jnp.where(is_new, kn_ref[:, hs], kbuf_ref[buf, :, n, :]).astype(BF16)
        vv = jnp.where(is_new, vn_ref[:, hs], vbuf_ref[buf, :, n, :]).astype(BF16)
        qn = jnp.concatenate([q[n * G:(n + 1) * G, :], jnp.zeros((8 - G, hd), F32)], axis=0).astype(BF16)
        s = lax.dot_general(qn, kk, nt, preferred_element_type=F32)
        m = jnp.max(s, axis=1, keepdims=True)
        p = jnp.exp(s - m)
        l = jnp.sum(p, axis=1, keepdims=True)
        o = jnp.dot(p.astype(BF16), vv, preferred_element_type=F32) / l
        outs.append(o[0:G, :])
    acc_ref[pl.ds(d, 1)] = jnp.concatenate(outs, axis=0)[None]

    @pl.when(d == pl.num_programs(0) - 1)
    def _():
        for h in range(ATT_HEADS):
            o_ref[:, h * hd:(h + 1) * hd] = acc_ref[:, h, :].astype(BF16)


def _mix_kernel(a_ref, b_ref, wa_ref, wb_ref, ga_ref, gb_ref, o_ref):
    ba = jnp.dot(a_ref[...], wa_ref[...], preferred_element_type=F32)
    bb = jnp.dot(b_ref[...], wb_ref[...], preferred_element_type=F32)
    o_ref[...] = (_sigmoid(ga_ref[...]) * ba + _sigmoid(gb_ref[...]) * bb).astype(BF16)


def _outproj_kernel(x_ref, m_ref, w_ref, g_ref, x1_ref, h_ref):
    x1 = x_ref[...] + jnp.dot(m_ref[...], w_ref[...], preferred_element_type=F32)
    x1_ref[...] = x1
    ms = jnp.mean(x1 * x1, axis=-1, keepdims=True)
    h_ref[...] = (x1 * lax.rsqrt(ms + EPS) * g_ref[...]).astype(BF16)


def _ffn_kernel(h_ref, halo_ref, x1_ref, wa_ref, wb_ref, wd_ref, cw_ref, cb_ref, st0_ref, st1_ref,
                gf_ref, y_ref, head_ref, tail_ref, acc_ref, s1_ref, s2_ref, *, n_dec):
    i = pl.program_id(0)
    j = pl.program_id(1)
    h = h_ref[...]
    a = jnp.dot(h, wa_ref[...], preferred_element_type=F32)
    b = jnp.dot(h, wb_ref[...], preferred_element_type=F32)
    ah = jnp.dot(halo_ref[...], wa_ref[...], preferred_element_type=F32)
    ah = jnp.where(i > 0, ah, 0.0)
    row = lax.broadcasted_iota(I32, a.shape, 0)
    hl = ah.shape[0]
    s1_ref[...] = jnp.where(row == 0, ah[hl - 1:hl, :], pltpu.roll(a, 1, 0))
    s2_ref[...] = jnp.where(row == 0, ah[hl - 2:hl - 1, :],
                            jnp.where(row == 1, ah[hl - 1:hl, :], pltpu.roll(a, 2, 0)))

    @pl.when(i == 0)
    def _():
        s1_ref[0:n_dec, :] = st1_ref[...]
        s2_ref[0:n_dec, :] = st0_ref[...]

    conv = cb_ref[...] + cw_ref[0:1, :] * s2_ref[...] + cw_ref[1:2, :] * s1_ref[...] + cw_ref[2:3, :] * a
    gate = (conv * _sigmoid(conv) * b).astype(BF16)
    part = jnp.dot(gate, wd_ref[...], preferred_element_type=F32)

    @pl.when(j == 0)
    def _():
        acc_ref[...] = part

    @pl.when(j > 0)
    def _():
        acc_ref[...] = acc_ref[...] + part

    head_ref[...] = a[0:head_ref.shape[0], :]
    tail_ref[...] = a[a.shape[0] - 8:, :]

    @pl.when(j == pl.num_programs(1) - 1)
    def _():
        x2 = x1_ref[...] + acc_ref[...]
        ms = jnp.mean(x2 * x2, axis=-1, keepdims=True)
        y_ref[...] = x2 * lax.rsqrt(ms + EPS) * gf_ref[...]


def _rope_tables(pos, hd, reps):
    rot = hd // 4
    half = rot // 2
    inv = jnp.exp(-math.log(ROPE_THETA) * jnp.arange(half, dtype=F32) * 2.0 / rot)
    ang = pos.astype(F32)[:, None] * inv[None, :]
    cos, sin = jnp.cos(ang), jnp.sin(ang)
    n = pos.shape[0]
    one = jnp.ones((n, hd - rot), F32)
    zero_r = jnp.zeros((n, hd - rot), F32)
    zero_h = jnp.zeros((n, half), F32)
    c = jnp.concatenate([cos, cos, one], axis=1)
    sa = jnp.concatenate([-sin, zero_h, zero_r], axis=1)
    sb = jnp.concatenate([zero_h, sin, zero_r], axis=1)
    return [jnp.tile(t, (1, reps)) for t in (c, sa, sb)]


def kernel(x_prompt, x_sample, cache_k, cache_v, cache_idx_k, state_gla, state_conv, page_table, meta_tokens, norm_mix_g, w_in, w_alpha_up, b_alpha, gla_norm_g, w_branch_a, w_branch_b, w_out, norm_ffn_g, w_up, conv_w, conv_b, w_down, norm_final_g):
    B, SEQ, D = x_prompt.shape
    DB = x_sample.shape[0]
    assert x_sample.shape[1] == 1 and w_in.shape[0] == 1
    n_pool = cache_k.shape[1]
    n_pages = page_table.shape[1]
    past = n_pages * PAGE_SIZE
    assert n_pages == LANE and PAGE_SIZE == LANE
    dff = w_down.shape[1]
    H = GLA_HEADS
    dk = D // 2 // H
    dv = D // H
    hd = D // ATT_HEADS
    kvw = ATT_KV_HEADS * hd
    T = SEQ + N_META
    TP = SEQ + FRONT
    R = B * TP
    assert DB <= GLA_CHUNK and DB % 16 == 0 and SEQ % LANE == 0 and dff % COL_TILE == 0
    k_sel_p = min(TOPK_MAX, T // 4)
    k_sel_s = min(TOPK_MAX, (past + 1) // 4)

    sizes = (H * dk, H * dk, H * dv, H * dv, GLA_GATE_RANK, ATT_HEADS * hd, kvw, kvw,
             IDX_HEADS * IDX_DIM, IDX_DIM, IDX_HEADS, D, D)
    offs = [0]
    for s_ in sizes:
        offs.append(offs[-1] + s_)
    w0 = w_in[0]
    wg_a = w0[:, offs[0]:offs[4]].astype(BF16)
    wg_b = w0[:, offs[5]:offs[9]].astype(BF16)
    wg_c = w0[:, offs[11]:offs[13]].astype(BF16)
    small_pad = COL_TILE - (IDX_DIM + GLA_GATE_RANK + IDX_HEADS)
    wg_s = jnp.concatenate([w0[:, offs[9]:offs[10]], w0[:, offs[4]:offs[5]], w0[:, offs[10]:offs[11]],
                            jnp.zeros((D, small_pad), F32)], axis=1).astype(BF16)
    n_a, n_b, n_c = (wg_a.shape[1] // COL_TILE, wg_b.shape[1] // COL_TILE, wg_c.shape[1] // COL_TILE)
    assert all(w_.shape[1] % COL_TILE == 0 for w_ in (wg_a, wg_b, wg_c)) and hd == LANE
    QA, KA, VA, RA = 0, sizes[0], sizes[0] + sizes[1], sizes[0] + sizes[1] + sizes[2]
    QB = n_a * COL_TILE
    KB, VB, QI = QB + sizes[5], QB + sizes[5] + sizes[6], QB + sizes[5] + sizes[6] + sizes[7]
    GA = (n_a + n_b) * COL_TILE
    GB = GA + D
    SM = (n_a + n_b + n_c) * COL_TILE
    NW = SM + COL_TILE
    assert all(v_ % COL_TILE == 0 for v_ in (QA, KA, VA, RA, QB, KB, VB, QI, GA, GB))
    assert VB - KB == COL_TILE
    n_ct = NW // COL_TILE
    wau_pad = jnp.zeros((LANE, H * dk), F32).at[IDX_DIM:IDX_DIM + GLA_GATE_RANK].set(w_alpha_up[0]).astype(BF16)
    wa_bf = w_branch_a[0].astype(BF16)
    wb_bf = w_branch_b[0].astype(BF16)
    wo_bf = w_out[0].astype(BF16)
    wup_bf = w_up[0].astype(BF16)
    wdn_bf = w_down[0].astype(BF16)

    front = jnp.zeros((B, ROW0, D), F32).at[0, :DB].set(x_sample[:, 0])
    meta = jnp.broadcast_to(meta_tokens[None].astype(F32), (B, N_META, D))
    x_all = jnp.concatenate([front, meta, x_prompt], axis=1).reshape(R, D)
    rpos = jnp.maximum(jnp.arange(TP, dtype=jnp.int32) - ROW0, 0)
    pos = jnp.concatenate([rpos.at[:DB].set(past), rpos])
    tabs = jnp.concatenate(_rope_tables(pos, hd, 1) + _rope_tables(pos, IDX_DIM, LANE // IDX_DIM), axis=1)

    tm1 = TP // 2
    bpb = TP // tm1
    P = pl.pallas_call(
        functools.partial(_inproj_kernel, n_a=n_a, n_b=n_b, n_c=n_c,
                          b128=(0, (VB - QB) // COL_TILE),
                          b64=((QI - QB) // COL_TILE, (QI - QB + IDX_HEADS * IDX_DIM) // COL_TILE)),
        grid=(R // tm1, n_ct),
        in_specs=[pl.BlockSpec((tm1, D), lambda i, j: (i, 0)),
                  pl.BlockSpec((1, D), lambda i, j: (0, 0)),
                  pl.BlockSpec((D, COL_TILE), lambda i, j: (0, jnp.minimum(j, n_a - 1))),
                  pl.BlockSpec((D, COL_TILE), lambda i, j: (0, jnp.clip(j - n_a, 0, n_b - 1))),
                  pl.BlockSpec((D, COL_TILE), lambda i, j: (0, jnp.clip(j - n_a - n_b, 0, n_c - 1))),
                  pl.BlockSpec((D, COL_TILE), lambda i, j: (0, 0)),
                  pl.BlockSpec((tm1, 6 * LANE), lambda i, j: (jnp.where(i < bpb, i, bpb + i % bpb), 0))],
        out_specs=pl.BlockSpec((tm1, COL_TILE), lambda i, j: (i, j)),
        out_shape=jax.ShapeDtypeStruct((R, NW), F32),
        scratch_shapes=[pltpu.VMEM((tm1, D), BF16)],
        compiler_params=_cparams(("parallel", "arbitrary")),
        name="inproj",
    )(x_all, norm_mix_g, wg_a, wg_b, wg_c, wg_s, tabs)

    C = GLA_CHUNK
    ncb = TP // C
    ba2 = b_alpha.reshape(1, H * dk)
    gain2 = gla_norm_g.reshape(1, H * dv)
    smc = SM // LANE
    a_out, s_fin = pl.pallas_call(
        functools.partial(_gla_prompt_kernel, dk=dk),
        grid=(B, H, ncb),
        in_specs=[pl.BlockSpec((C, dk), lambda b, h, c: (b * ncb + c, QA // dk + h)),
                  pl.BlockSpec((C, dk), lambda b, h, c: (b * ncb + c, KA // dk + h)),
                  pl.BlockSpec((C, dv), lambda b, h, c: (b * ncb + c, VA // dv + h)),
                  pl.BlockSpec((C, dv), lambda b, h, c: (b * ncb + c, RA // dv + h)),
                  pl.BlockSpec((C, LANE), lambda b, h, c: (b * ncb + c, smc)),
                  pl.BlockSpec((LANE, dk), lambda b, h, c: (0, h)),
                  pl.BlockSpec((1, dk), lambda b, h, c: (0, h)),
                  pl.BlockSpec((1, dv), lambda b, h, c: (0, h))],
        out_specs=[pl.BlockSpec((C, dv), lambda b, h, c: (b * ncb + c, h)),
                   pl.BlockSpec((None, None, dk, dv), lambda b, h, c: (b, h, 0, 0))],
        out_shape=[jax.ShapeDtypeStruct((R, H * dv), BF16),
                   jax.ShapeDtypeStruct((B, H, dk, dv), F32)],
        scratch_shapes=[pltpu.VMEM((dk, dv), F32)],
        compiler_params=_cparams(("parallel", "parallel", "arbitrary")),
        name="gla_prompt",
    )(P, P, P, P, P, wau_pad, ba2, gain2)

    p_s = P[:DB]
    P_rows = p_s.reshape(DB, 1, NW)
    a_out, s_new_s = pl.pallas_call(
        functools.partial(_gla_sample_kernel, dk=dk),
        grid=(H, DB),
        in_specs=[pl.BlockSpec(memory_space=pl.ANY),
                  pl.BlockSpec((None, 1, dk), lambda h, d: (d, 0, QA // dk + h)),
                  pl.BlockSpec((None, 1, dk), lambda h, d: (d, 0, KA // dk + h)),
                  pl.BlockSpec((None, 1, dv), lambda h, d: (d, 0, VA // dv + h)),
                  pl.BlockSpec((None, 1, dv), lambda h, d: (d, 0, RA // dv + h)),
                  pl.BlockSpec((None, 1, LANE), lambda h, d: (d, 0, smc)),
                  pl.BlockSpec((LANE, dk), lambda h, d: (0, h)),
                  pl.BlockSpec((1, dk), lambda h, d: (0, h)),
                  pl.BlockSpec((1, dv), lambda h, d: (0, h)),
                  pl.BlockSpec((None, None, dk, dv), lambda h, d: (d, h, 0, 0))],
        out_specs=[pl.BlockSpec((DB, dv), lambda h, d: (0, h)),
                   pl.BlockSpec((None, None, dk, dv), lambda h, d: (d, h, 0, 0))],
        out_shape=[jax.ShapeDtypeStruct((R, H * dv), BF16),
                   jax.ShapeDtypeStruct((DB, H, dk, dv), F32)],
        scratch_shapes=[pltpu.VMEM((DB, dv), F32)],
        input_output_aliases={0: 0},
        compiler_params=_cparams(("arbitrary", "arbitrary")),
        name="gla_sample",
    )(a_out, P_rows, P_rows, P_rows, P_rows, P_rows, wau_pad, ba2, gain2, state_gla[0])

    TQ = LANE
    nqb = TP // TQ
    n_cls = -(-nqb // QB_PER_CLASS)
    widths = tuple(min((c + 1) * QB_PER_CLASS * TQ, TP) for c in range(n_cls))
    o_b = pl.pallas_call(
        functools.partial(_dsa_prompt_kernel, k_sel=k_sel_p, hd=hd, widths=widths),
        grid=(nqb, B),
        in_specs=[pl.BlockSpec((TQ, IDX_HEADS * IDX_DIM), lambda q, b: (b * nqb + q, QI // (IDX_HEADS * IDX_DIM))),
                  pl.BlockSpec((TQ, LANE), lambda q, b: (b * nqb + q, smc)),
                  pl.BlockSpec((TP, LANE), lambda q, b: (b, smc)),
                  pl.BlockSpec((TQ, ATT_HEADS * hd), lambda q, b: (b * nqb + q, QB // (ATT_HEADS * hd))),
                  pl.BlockSpec((TP, kvw), lambda q, b: (b, KB // kvw)),
                  pl.BlockSpec((TP, kvw), lambda q, b: (b, VB // kvw))],
        out_specs=pl.BlockSpec((TQ, ATT_HEADS * hd), lambda q, b: (b * nqb + q, 0)),
        out_shape=jax.ShapeDtypeStruct((R, ATT_HEADS * hd), BF16),
        scratch_shapes=[pltpu.VMEM((TP, LANE), BF16), pltpu.VMEM((TP, LANE), BF16),
                        pltpu.VMEM((TP, kvw), BF16), pltpu.VMEM((TP, kvw), BF16),
                        pltpu.VMEM((TQ, TP), I32)],
        compiler_params=_cparams(("arbitrary", "arbitrary")),
        name="dsa_prompt",
    )(P, P, P, P, P, P)

    qi_s = p_s[:, QI:QI + IDX_HEADS * IDX_DIM].reshape(DB, IDX_HEADS, IDX_DIM)
    wi_s = p_s[:, SM + WI_OFF:SM + WI_OFF + IDX_HEADS].reshape(DB, IDX_HEADS, 1)
    ki_s = p_s[:, SM:SM + IDX_DIM].reshape(DB, 1, IDX_DIM)
    sc, scn = pl.pallas_call(
        functools.partial(_dsa_sample_score_kernel, n_pages=n_pages),
        grid_spec=pltpu.PrefetchScalarGridSpec(
            num_scalar_prefetch=1,
            grid=(DB,),
            in_specs=[pl.BlockSpec(memory_space=pl.ANY),
                      pl.BlockSpec((None, IDX_HEADS, IDX_DIM), lambda d, pt: (d, 0, 0)),
                      pl.BlockSpec((None, IDX_HEADS, 1), lambda d, pt: (d, 0, 0)),
                      pl.BlockSpec((None, 1, IDX_DIM), lambda d, pt: (d, 0, 0))],
            out_specs=[pl.BlockSpec((None, n_pages, PAGE_SIZE), lambda d, pt: (d, 0, 0)),
                       pl.BlockSpec((None, 1, LANE), lambda d, pt: (d, 0, 0))],
            scratch_shapes=[pltpu.VMEM((2, n_pages, IDX_DIM, PAGE_SIZE), F32),
                            pltpu.SemaphoreType.DMA((2,))]),
        out_shape=[jax.ShapeDtypeStruct((DB, n_pages, PAGE_SIZE), F32),
                   jax.ShapeDtypeStruct((DB, 1, LANE), F32)],
        compiler_params=_cparams(("arbitrary",)),
        name="dsa_sample_score",
    )(page_table, jnp.swapaxes(cache_idx_k[0], 1, 2), qi_s, wi_s, ki_s)

    sidx = lax.broadcasted_iota(jnp.int32, (past + LANE, LANE), 0)
    hl_lane = lax.broadcasted_iota(jnp.int32, (past + LANE, LANE), 1)
    hl = jnp.where(hl_lane == 0, sidx // PAGE_SIZE, jnp.where(hl_lane == 1, sidx % PAGE_SIZE, 0)).astype(BF16)
    sel = pl.pallas_call(
        functools.partial(_dsa_sample_select_kernel, k_sel=k_sel_s, chunk=2048),
        out_shape=jax.ShapeDtypeStruct((DB, k_sel_s, LANE), I32),
        scratch_shapes=[pltpu.VMEM((DB, 1, past + LANE), F32), pltpu.VMEM((DB, past + LANE), I32)],
        compiler_params=pltpu.CompilerParams(vmem_limit_bytes=VMEM_LIMIT),
        name="dsa_sample_select",
    )(sc.reshape(DB, past), scn.reshape(DB, LANE), hl)

    sel_s = sel[:, :, 0:2].reshape(DB, 2 * k_sel_s)
    q_s = p_s[:, QB:QB + ATT_HEADS * hd].reshape(DB, ATT_HEADS, hd)
    kn_s = p_s[:, KB:KB + kvw]
    vn_s = p_s[:, VB:VB + kvw]
    o_b = pl.pallas_call(
        functools.partial(_dsa_sample_attend_kernel, k_sel=k_sel_s, n_pages=n_pages, hd=hd),
        grid_spec=pltpu.PrefetchScalarGridSpec(
            num_scalar_prefetch=2,
            grid=(DB,),
            in_specs=[pl.BlockSpec(memory_space=pl.ANY),
                      pl.BlockSpec(memory_space=pl.ANY),
                      pl.BlockSpec(memory_space=pl.ANY),
                      pl.BlockSpec((None, k_sel_s, LANE), lambda d, s_, pt: (d, 0, 0)),
                      pl.BlockSpec((None, ATT_HEADS, hd), lambda d, s_, pt: (d, 0, 0)),
                      pl.BlockSpec((None, 1, kvw), lambda d, s_, pt: (d, 0, KB // kvw)),
                      pl.BlockSpec((None, 1, kvw), lambda d, s_, pt: (d, 0, VB // kvw))],
            out_specs=pl.BlockSpec((DB, ATT_HEADS * hd), lambda d, s_, pt: (0, 0)),
            scratch_shapes=[pltpu.VMEM((2, k_sel_s, ATT_KV_HEADS, hd), F32),
                            pltpu.VMEM((2, k_sel_s, ATT_KV_HEADS, hd), F32),
                            pltpu.VMEM((DB, ATT_HEADS, hd), F32), pltpu.SemaphoreType.DMA((2, 2))]),
        out_shape=jax.ShapeDtypeStruct((R, ATT_HEADS * hd), BF16),
        input_output_aliases={2: 0},
        compiler_params=_cparams(("arbitrary",)),
        name="dsa_sample_attend",
    )(sel_s, page_table, o_b, cache_k[0], cache_v[0], sel, q_s, P_rows, P_rows)

    mix = pl.pallas_call(
        _mix_kernel,
        grid=(R // tm1, D // COL_TILE),
        in_specs=[pl.BlockSpec((tm1, H * dv), lambda i, j: (i, 0)),
                  pl.BlockSpec((tm1, ATT_HEADS * hd), lambda i, j: (i, 0)),
                  pl.BlockSpec((H * dv, COL_TILE), lambda i, j: (0, j)),
                  pl.BlockSpec((ATT_HEADS * hd, COL_TILE), lambda i, j: (0, j)),
                  pl.BlockSpec((tm1, COL_TILE), lambda i, j: (i, GA // COL_TILE + j)),
                  pl.BlockSpec((tm1, COL_TILE), lambda i, j: (i, GB // COL_TILE + j))],
        out_specs=pl.BlockSpec((tm1, COL_TILE), lambda i, j: (i, j)),
        out_shape=jax.ShapeDtypeStruct((R, D), BF16),
        compiler_params=_cparams(("parallel", "arbitrary")),
        name="mix",
    )(a_out, o_b, wa_bf, wb_bf, P, P)

    tm2 = TP // 4
    x1, h2 = pl.pallas_call(
        _outproj_kernel,
        grid=(R // tm2,),
        in_specs=[pl.BlockSpec((tm2, D), lambda i: (i, 0)),
                  pl.BlockSpec((tm2, D), lambda i: (i, 0)),
                  pl.BlockSpec((D, D), lambda i: (0, 0)),
                  pl.BlockSpec((1, D), lambda i: (0, 0))],
        out_specs=[pl.BlockSpec((tm2, D), lambda i: (i, 0)),
                   pl.BlockSpec((tm2, D), lambda i: (i, 0))],
        out_shape=[jax.ShapeDtypeStruct((R, D), F32), jax.ShapeDtypeStruct((R, D), BF16)],
        compiler_params=_cparams(("parallel",)),
        name="outproj",
    )(x_all, mix, wo_bf, norm_ffn_g)

    nrb = R // tm2
    nft = dff // COL_TILE
    y_all, a_head, a_tail = pl.pallas_call(
        functools.partial(_ffn_kernel, n_dec=DB),
        grid=(nrb, nft),
        in_specs=[pl.BlockSpec((tm2, D), lambda i, j: (i, 0)),
                  pl.BlockSpec((16, D), lambda i, j: (jnp.maximum(i * (tm2 // 16) - 1, 0), 0)),
                  pl.BlockSpec((tm2, D), lambda i, j: (i, 0)),
                  pl.BlockSpec((D, COL_TILE), lambda i, j: (0, j)),
                  pl.BlockSpec((D, COL_TILE), lambda i, j: (0, nft + j)),
                  pl.BlockSpec((COL_TILE, D), lambda i, j: (j, 0)),
                  pl.BlockSpec((CONV_W, COL_TILE), lambda i, j: (0, j)),
                  pl.BlockSpec((1, COL_TILE), lambda i, j: (0, j)),
                  pl.BlockSpec((DB, COL_TILE), lambda i, j: (0, j)),
                  pl.BlockSpec((DB, COL_TILE), lambda i, j: (0, j)),
                  pl.BlockSpec((1, D), lambda i, j: (0, 0))],
        out_specs=[pl.BlockSpec((tm2, D), lambda i, j: (i, 0)),
                   pl.BlockSpec((DB, COL_TILE), lambda i, j: (i, j)),
                   pl.BlockSpec((8, COL_TILE), lambda i, j: (i, j))],
        out_shape=[jax.ShapeDtypeStruct((R, D), F32),
                   jax.ShapeDtypeStruct((nrb * DB, dff), F32),
                   jax.ShapeDtypeStruct((nrb * 8, dff), F32)],
        scratch_shapes=[pltpu.VMEM((tm2, D), F32), pltpu.VMEM((tm2, COL_TILE), F32),
                        pltpu.VMEM((tm2, COL_TILE), F32)],
        compiler_params=_cparams(("parallel", "arbitrary")),
        name="ffn",
    )(h2, h2, x1, wup_bf, wup_bf, wdn_bf, conv_w[0], conv_b, state_conv[0, :, 0], state_conv[0, :, 1],
      norm_final_g.reshape(1, D))

    y3 = y_all.reshape(B, TP, D)
    y_prompt = y3[:, FRONT:]
    y_sample = y_all[:DB].reshape(DB, 1, D)
    P3 = P.reshape(B, TP, NW)
    new_k_p = P3[:, ROW0:, KB:KB + kvw].reshape(1, B, T, ATT_KV_HEADS, hd)
    new_v_p = P3[:, ROW0:, VB:VB + kvw].reshape(1, B, T, ATT_KV_HEADS, hd)
    new_ki_p = P3[:, ROW0:, SM:SM + IDX_DIM].reshape(1, B, T, IDX_DIM)
    new_gla_p = s_fin[None]
    tails = a_tail.reshape(B, nrb // B, 8, dff)[:, -1, 8 - (CONV_W - 1):, :]
    new_conv_p = tails[None]
    new_k_s = kn_s.reshape(1, DB, 1, ATT_KV_HEADS, hd)
    new_v_s = vn_s.reshape(1, DB, 1, ATT_KV_HEADS, hd)
    new_ki_s = ki_s.reshape(1, DB, 1, IDX_DIM)
    new_gla_s = s_new_s[None]
    new_conv_s = jnp.stack([state_conv[0, :, 1], a_head[:DB]], axis=1)[None]
    return (y_prompt, y_sample, new_k_p, new_v_p, new_ki_p, new_gla_p, new_conv_p,
            new_k_s, new_v_s, new_ki_s, new_gla_s, new_conv_s)
```

```python
import functools
import math

import jax
import jax.numpy as jnp
from jax import lax
from jax.experimental import pallas as pl
from jax.experimental.pallas import tpu as pltpu

F32 = jnp.float32
BF16 = jnp.bfloat16
I32 = jnp.int32

N_META = 16
GLA_HEADS = 4
GLA_GATE_RANK = 16
GLA_GATE_NORM = 16.0
GLA_CHUNK = 64
GLA_SUB = 16
ATT_HEADS = 16
ATT_KV_HEADS = 4
IDX_HEADS = 16
IDX_DIM = 64
TOPK_MAX = 256
ROPE_THETA = 500000.0
CONV_W = 3
EPS = 1e-6
PAGE_SIZE = 128
WI_OFF = IDX_DIM + GLA_GATE_RANK

LANE = 128
FRONT = 128
ROW0 = FRONT - N_META
COL_TILE = 512
QB_PER_CLASS = 4
VMEM_LIMIT = 56 * 1024 * 1024
INT_MIN = -2 ** 31


def _cparams(sem):
    return pltpu.CompilerParams(dimension_semantics=sem, vmem_limit_bytes=VMEM_LIMIT)


def _sigmoid(x):
    return 1.0 / (1.0 + jnp.exp(-x))


def _sortable(x):
    i = pltpu.bitcast(x, I32)
    return jnp.where(i < 0, i ^ jnp.int32(0x7FFFFFFF), i)


def _kth_largest_key(load_keys, rows, k_sel):
    def body(it, t):
        bit = 31 - it
        cand = t + jnp.left_shift(jnp.int32(1), bit)
        cnt = jnp.sum((load_keys() >= cand).astype(I32), axis=1, keepdims=True)
        return jnp.where(cnt >= k_sel, cand, t)

    return lax.fori_loop(0, 32, body, jnp.full((rows, 1), INT_MIN, I32))


def _rope_block(xb, c, sa, sb, half):
    return xb * c + pltpu.roll(xb, LANE - half, 1) * sa + pltpu.roll(xb, half, 1) * sb


def _inproj_kernel(x_ref, g_ref, wa_ref, wb_ref, wc_ref, ws_ref, tab_ref, o_ref, h_ref,
                   *, n_a, n_b, n_c, b128, b64):
    j = pl.program_id(1)

    @pl.when(j == 0)
    def _():
        x = x_ref[...]
        ms = jnp.mean(x * x, axis=-1, keepdims=True)
        h_ref[...] = (x * lax.rsqrt(ms + EPS) * g_ref[...]).astype(BF16)

    def proj(w_ref):
        return jnp.dot(h_ref[...], w_ref[...], preferred_element_type=F32)

    @pl.when(j < n_a)
    def _():
        o_ref[...] = proj(wa_ref)

    @pl.when(jnp.logical_and(j >= n_a, j < n_a + n_b))
    def _():
        acc = proj(wb_ref)
        jb = j - n_a
        is128 = jnp.logical_and(jb >= b128[0], jb < b128[1])
        is64 = jnp.logical_and(jb >= b64[0], jb < b64[1])

        @pl.when(is128)
        def _():
            c, sa, sb = tab_ref[:, 0:128], tab_ref[:, 128:256], tab_ref[:, 256:384]
            for blk in range(COL_TILE // LANE):
                sl = slice(blk * LANE, (blk + 1) * LANE)
                o_ref[:, sl] = _rope_block(acc[:, sl], c, sa, sb, 16)

        @pl.when(is64)
        def _():
            c, sa, sb = tab_ref[:, 384:512], tab_ref[:, 512:640], tab_ref[:, 640:768]
            for blk in range(COL_TILE // LANE):
                sl = slice(blk * LANE, (blk + 1) * LANE)
                o_ref[:, sl] = _rope_block(acc[:, sl], c, sa, sb, 8)

        @pl.when(jnp.logical_not(is128 | is64))
        def _():
            o_ref[...] = acc

    @pl.when(jnp.logical_and(j >= n_a + n_b, j < n_a + n_b + n_c))
    def _():
        o_ref[...] = proj(wc_ref)

    @pl.when(j == n_a + n_b + n_c)
    def _():
        acc = proj(ws_ref)
        lane = lax.broadcasted_iota(I32, (1, LANE), 1)
        first = lane < IDX_DIM
        c = jnp.where(first, tab_ref[:, 384:512], 1.0)
        sa = jnp.where(first, tab_ref[:, 512:640], 0.0)
        sb = jnp.where(first, tab_ref[:, 640:768], 0.0)
        o_ref[:, 0:LANE] = _rope_block(acc[:, 0:LANE], c, sa, sb, 8)
        o_ref[:, LANE:] = acc[:, LANE:]


def _log_alpha(sm, wau_ref, ba_ref):
    x = jnp.dot(sm.astype(BF16), wau_ref[...], preferred_element_type=F32) + ba_ref[...]
    return (jnp.minimum(x, 0.0) - jnp.log(1.0 + jnp.exp(-jnp.abs(x)))) * (1.0 / GLA_GATE_NORM)


def _row_to_col(row, n):
    eye = lax.broadcasted_iota(I32, (n, n), 0) == lax.broadcasted_iota(I32, (n, n), 1)
    return jnp.sum(jnp.where(eye, row, 0.0), axis=1, keepdims=True)


def _readout(o, r, gain):
    ms = jnp.mean(o * o, axis=-1, keepdims=True)
    return o * lax.rsqrt(ms + EPS) * gain * (r * _sigmoid(r))


def _gla_prompt_kernel(q_ref, k_ref, v_ref, r_ref, sm_ref, wau_ref, ba_ref, gain_ref,
                       o_ref, sfin_ref, s_ref, *, dk):
    c = pl.program_id(2)
    C = GLA_CHUNK

    @pl.when(c == 0)
    def _():
        s_ref[...] = jnp.zeros_like(s_ref)
        o_ref[...] = jnp.zeros_like(o_ref)

    @pl.when(c > 0)
    def _():
        q = q_ref[...] * (dk ** -0.5)
        k = k_ref[...]
        v = v_ref[...]
        vb = v.astype(BF16)
        g = _log_alpha(sm_ref[...], wau_ref, ba_ref)
        row = lax.broadcasted_iota(I32, g.shape, 0)
        b = g
        sh = 1
        while sh < C:
            b = b + jnp.where(row >= sh, pltpu.roll(b, sh, 0), 0.0)
            sh *= 2
        bl = b[C - 1:C, :]
        S = s_ref[...]
        o = jnp.dot((q * jnp.exp(b)).astype(BF16), S.astype(BF16), preferred_element_type=F32)

        tcol = lax.broadcasted_iota(I32, (GLA_SUB, C), 1)
        trow = lax.broadcasted_iota(I32, (GLA_SUB, C), 0)
        a_rows = []
        for blk in range(C // GLA_SUB):
            r0 = blk * GLA_SUB
            ref = b[r0 - 1:r0, :] if blk > 0 else jnp.zeros_like(bl)
            b_i = b[r0:r0 + GLA_SUB, :]
            q_i = q[r0:r0 + GLA_SUB, :]
            qe = q_i * jnp.exp(b_i - ref)
            kf = k * jnp.exp(jnp.minimum(ref - b, 0.0))
            a_off = lax.dot_general(qe.astype(BF16), kf.astype(BF16), (((1,), (1,)), ((), ())),
                                    preferred_element_type=F32)
            diag = jnp.zeros((GLA_SUB, C), F32)
            for sl in range(GLA_SUB):
                s = r0 + sl
                w = q_i * k[s:s + 1, :] * jnp.exp(jnp.minimum(b_i - b[s:s + 1, :], 0.0))
                diag = jnp.where(tcol == s, jnp.sum(w, axis=1, keepdims=True), diag)
            a_rows.append(jnp.where(tcol < r0, a_off, jnp.where(tcol <= trow + r0, diag, 0.0)))
        a = jnp.concatenate(a_rows, axis=0)
        o = o + jnp.dot(a.astype(BF16), vb, preferred_element_type=F32)

        kd = k * jnp.exp(bl - b)
        upd = lax.dot_general(kd.astype(BF16), vb, (((0,), (0,)), ((), ())),
                              preferred_element_type=F32)
        s_ref[...] = S * _row_to_col(jnp.exp(bl), dk) + upd
        o_ref[...] = _readout(o, r_ref[...], gain_ref[...]).astype(BF16)

    @pl.when(c == pl.num_programs(2) - 1)
    def _():
        sfin_ref[...] = s_ref[...]


def _gla_sample_kernel(alias_ref, q_ref, k_ref, v_ref, r_ref, sm_ref, wau_ref, ba_ref, gain_ref,
                       st_ref, o_ref, snew_ref, acc_ref, *, dk):
    del alias_ref
    d = pl.program_id(1)
    q = q_ref[...] * (dk ** -0.5)
    k = k_ref[...]
    v = v_ref[...]
    r = r_ref[...]
    g = _log_alpha(jnp.broadcast_to(sm_ref[...], (8, LANE)), wau_ref, ba_ref)[0:1, :]
    eg = jnp.exp(g)
    S = st_ref[...]
    qe = jnp.broadcast_to(q * eg, (8, dk))
    o = jnp.dot(qe.astype(BF16), S.astype(BF16), preferred_element_type=F32)[0:1, :]
    o = o + jnp.sum(q * k, axis=1, keepdims=True) * v
    snew_ref[...] = S * _row_to_col(eg, dk) + _row_to_col(k, dk) * v
    on = _readout(o, r, gain_ref[...])
    row = lax.broadcasted_iota(I32, acc_ref.shape, 0)

    @pl.when(d == 0)
    def _():
        acc_ref[...] = jnp.zeros_like(acc_ref)

    acc_ref[...] = jnp.where(row == d, on, acc_ref[...])

    @pl.when(d == pl.num_programs(1) - 1)
    def _():
        o_ref[...] = acc_ref[...].astype(BF16)


def _dsa_prompt_kernel(alias_ref, qi_ref, smq_ref, smk_ref, qb_ref, kb_ref, vb_ref, o_ref,
                       ka_ref, kbb_ref, kbf_ref, vbf_ref, key_ref, *, k_sel, hd, width, q_lo):
    del alias_ref
    qb = q_lo + pl.program_id(1)
    TQ = qi_ref.shape[0]
    nt = (((1,), (1,)), ((), ()))
    G = ATT_HEADS // ATT_KV_HEADS

    @pl.when(pl.program_id(1) == 0)
    def _():
        smk = smk_ref[0:width, :]
        lane = lax.broadcasted_iota(I32, (1, LANE), 1)
        ka_ref[...] = jnp.where(lane < IDX_DIM, smk, 0.0).astype(BF16)
        kbb_ref[...] = jnp.where(lane >= IDX_DIM, pltpu.roll(smk, IDX_DIM, 1), 0.0).astype(BF16)
        kbf_ref[...] = kb_ref[0:width, :].astype(BF16)
        vbf_ref[...] = vb_ref[0:width, :].astype(BF16)

    def body(TK):
        score = jnp.zeros((TQ, TK), F32)
        for p in range(IDX_HEADS // 2):
            qp = qi_ref[:, p * LANE:(p + 1) * LANE].astype(BF16)
            for half, kref in ((0, ka_ref), (1, kbb_ref)):
                h = 2 * p + half
                s = lax.dot_general(qp, kref[0:TK, :], nt, preferred_element_type=F32)
                w = smq_ref[:, WI_OFF + h:WI_OFF + h + 1] * (IDX_HEADS ** -0.5 * IDX_DIM ** -0.5)
                score = score + w * jnp.maximum(s, 0.0)

        qrow = qb * TQ + lax.broadcasted_iota(I32, (TQ, 1), 0)
        kcol = lax.broadcasted_iota(I32, (1, TK), 1)
        adm = jnp.logical_and(kcol <= qrow, kcol >= ROW0)
        key_ref[:, 0:TK] = jnp.where(adm, _sortable(score), INT_MIN)
        thr = jnp.maximum(_kth_largest_key(lambda: key_ref[:, 0:TK], TQ, k_sel), INT_MIN + 1)
        sel = key_ref[:, 0:TK] >= thr

        for n in range(ATT_KV_HEADS):
            q4 = jnp.concatenate(
                [(qb_ref[:, (n * G + gq) * hd:(n * G + gq + 1) * hd] * (hd ** -0.5)).astype(BF16)
                 for gq in range(G)], axis=0)
            s = lax.dot_general(q4, kbf_ref[0:TK, n * hd:(n + 1) * hd], nt, preferred_element_type=F32)
            s = jnp.where(sel[None], s.reshape(G, TQ, TK), -jnp.inf).reshape(G * TQ, TK)
            m = jnp.max(s, axis=1, keepdims=True)
            m = jnp.where(m == -jnp.inf, 0.0, m)
            p = jnp.exp(s - m)
            l = jnp.sum(p, axis=1, keepdims=True)
            o = jnp.dot(p.astype(BF16), vbf_ref[0:TK, n * hd:(n + 1) * hd], preferred_element_type=F32)
            o = jnp.where(l > 0.0, o / l, 0.0)
            for gq in range(G):
                o_ref[:, (n * G + gq) * hd:(n * G + gq + 1) * hd] = o[gq * TQ:(gq + 1) * TQ, :].astype(BF16)

    body(width)


def _idx_page_copy(cache_ref, buf_ref, sem, page, slot, p):
    return pltpu.make_async_copy(cache_ref.at[page], buf_ref.at[slot, p], sem.at[slot])


def _dsa_sample_score_kernel(pt_ref, cache_ref, qi_ref, wi_ref, kin_ref, sc_ref, scn_ref,
                             buf_ref, sem, *, n_pages):
    d = pl.program_id(0)
    slot = d % 2

    def fetch(dd, sl):
        def start(p, carry):
            _idx_page_copy(cache_ref, buf_ref, sem, pt_ref[dd, p], sl, p).start()
            return carry
        lax.fori_loop(0, n_pages, start, 0, unroll=8)

    @pl.when(d == 0)
    def _():
        fetch(0, 0)

    @pl.when(d + 1 < pl.num_programs(0))
    def _():
        fetch(d + 1, 1 - slot)

    def wait(p, carry):
        _idx_page_copy(cache_ref, buf_ref, sem, 0, slot, p).wait()
        return carry

    lax.fori_loop(0, n_pages, wait, 0, unroll=8)
    nt = (((1,), (1,)), ((), ()))
    qi = qi_ref[...].astype(BF16)
    w = wi_ref[...] * (IDX_HEADS ** -0.5 * IDX_DIM ** -0.5)
    kp = buf_ref[slot].astype(BF16)
    qib = jnp.broadcast_to(qi[None], (n_pages, IDX_HEADS, IDX_DIM))
    s = lax.dot_general(qib, kp, (((2,), (1,)), ((0,), (0,))), preferred_element_type=F32)
    sc_ref[...] = jnp.sum(w[None] * jnp.maximum(s, 0.0), axis=1)
    kn = jnp.broadcast_to(kin_ref[...], (8, IDX_DIM)).astype(BF16)
    sn = lax.dot_general(qi, kn, nt, preferred_element_type=F32)[:, 0:1]
    scn_ref[...] = jnp.broadcast_to(jnp.sum(w * jnp.maximum(sn, 0.0), axis=0, keepdims=True), (1, LANE))


def _dsa_sample_select_kernel(sc_ref, scn_ref, hl_ref, sel_ref, pos_ref, key_ref, *, k_sel, chunk):
    DB, NP = sc_ref.shape
    nblk = NP // LANE
    lane1 = lax.broadcasted_iota(I32, (1, LANE), 1)
    key_ref[:, 0:NP] = _sortable(sc_ref[...])
    key_ref[:, NP:] = jnp.where(lane1 == 0, _sortable(scn_ref[...]), INT_MIN)
    thr = _kth_largest_key(lambda: key_ref[...], DB, k_sel)
    keys = key_ref[...]
    gt = keys > thr
    eq = keys == thr
    n_gt = jnp.sum(gt.astype(I32), axis=1, keepdims=True).astype(F32)

    iu = lax.broadcasted_iota(I32, (LANE, LANE), 0)
    ju = lax.broadcasted_iota(I32, (LANE, LANE), 1)
    upper = jnp.where(iu < ju, 1.0, 0.0).astype(BF16)
    ones = jnp.ones((LANE, LANE), BF16)

    def excl_prefix(mask):
        mb = jnp.where(mask, 1.0, 0.0).astype(BF16)
        stacked = jnp.concatenate([mb[:, c * LANE:(c + 1) * LANE] for c in range(nblk + 1)], axis=0)
        within = jnp.dot(stacked, upper, preferred_element_type=F32)
        tot = jnp.dot(stacked, ones, preferred_element_type=F32)[:, 0:1]
        outs = []
        run = jnp.zeros((DB, 1), F32)
        for c in range(nblk + 1):
            outs.append(within[c * DB:(c + 1) * DB, :] + run)
            run = run + tot[c * DB:(c + 1) * DB, :]
        return jnp.concatenate(outs, axis=1)

    pos_gt = excl_prefix(gt)
    pos_eq = excl_prefix(eq) + n_gt
    keep_eq = jnp.logical_and(eq, pos_eq < k_sel)
    pos = jnp.where(gt, pos_gt, jnp.where(keep_eq, pos_eq, -1.0))
    for d in range(DB):
        pos_ref[d] = pos[d:d + 1, :]

    jrow = lax.broadcasted_iota(I32, (k_sel, 1), 0).astype(F32)
    width = NP + LANE
    def compact(d, carry):
        acc = jnp.zeros((k_sel, LANE), F32)
        for c0 in range(0, width, chunk):
            c1 = min(c0 + chunk, width)
            e = jnp.where(pos_ref[d, :, c0:c1] == jrow, 1.0, 0.0).astype(BF16)
            acc = acc + jnp.dot(e, hl_ref[c0:c1, :], preferred_element_type=F32)
        sel_ref[d] = acc.astype(I32)
        return carry

    lax.fori_loop(0, DB, compact, 0)


def _kv_copies(ck_ref, cv_ref, kbuf_ref, vbuf_ref, sem, page, slot, buf, j):
    ck = pltpu.make_async_copy(ck_ref.at[page, slot], kbuf_ref.at[buf, j], sem.at[0, buf])
    cv = pltpu.make_async_copy(cv_ref.at[page, slot], vbuf_ref.at[buf, j], sem.at[1, buf])
    return ck, cv


def _dsa_sample_attend_kernel(sel_s_ref, pt_ref, alias_ref, ck_ref, cv_ref, selv_ref, q_ref,
                              kn_ref, vn_ref, o_ref, kbuf_ref, vbuf_ref, acc_ref, sem,
                              *, k_sel, n_pages, hd):
    del alias_ref
    d = pl.program_id(0)
    buf = d % 2

    def fetch(dd, bb):
        def start(j, carry):
            page = jnp.minimum(sel_s_ref[dd, 2 * j], n_pages - 1)
            ck, cv = _kv_copies(ck_ref, cv_ref, kbuf_ref, vbuf_ref, sem, pt_ref[dd, page],
                                sel_s_ref[dd, 2 * j + 1], bb, j)
            ck.start()
            cv.start(priority=1)
            return carry
        lax.fori_loop(0, k_sel, start, 0, unroll=8)

    @pl.when(d == 0)
    def _():
        fetch(0, 0)

    @pl.when(d + 1 < pl.num_programs(0))
    def _():
        fetch(d + 1, 1 - buf)

    def wait(j, carry):
        ck, cv = _kv_copies(ck_ref, cv_ref, kbuf_ref, vbuf_ref, sem, 0, 0, buf, j)
        ck.wait()
        cv.wait()
        return carry

    lax.fori_loop(0, k_sel, wait, 0, unroll=8)
    is_new = selv_ref[:, 0:1] >= n_pages
    G = ATT_HEADS // ATT_KV_HEADS
    nt = (((1,), (1,)), ((), ()))
    q = q_ref[...] * (hd ** -0.5)
    outs = []
    for n in range(ATT_KV_HEADS):
        hs = slice(n * hd, (n + 1) * hd)
        kk = jnp.where(is_new, kn_ref[:, hs], kbuf_ref[buf, :, n, :]).astype(BF16)
        vv = jnp.where(is_new, vn_ref[:, hs], vbuf_ref[buf, :, n, :]).astype(BF16)
        qn = jnp.concatenate([q[n * G:(n + 1) * G, :], jnp.zeros((8 - G, hd), F32)], axis=0).astype(BF16)
        s = lax.dot_general(qn, kk, nt, preferred_element_type=F32)
        m = jnp.max(s, axis=1, keepdims=True)
        p = jnp.exp(s - m)
        l = jnp.sum(p, axis=1, keepdims=True)
        o = jnp.dot(p.astype(BF16), vv, preferred_element_type=F32) / l
        outs.append(o[0:G, :])
    acc_ref[pl.ds(d, 1)] = jnp.concatenate(outs, axis=0)[None]

    @pl.when(d == pl.num_programs(0) - 1)
    def _():
        for h in range(ATT_HEADS):
            o_ref[:, h * hd:(h + 1) * hd] = acc_ref[:, h, :].astype(BF16)


def _mix_kernel(a_ref, b_ref, wa_ref, wb_ref, ga_ref, gb_ref, o_ref):
    ba = jnp.dot(a_ref[...], wa_ref[...], preferred_element_type=F32)
    bb = jnp.dot(b_ref[...], wb_ref[...], preferred_element_type=F32)
    o_ref[...] = (_sigmoid(ga_ref[...]) * ba + _sigmoid(gb_ref[...]) * bb).astype(BF16)


def _outproj_kernel(x_ref, m_ref, w_ref, g_ref, x1_ref, h_ref):
    x1 = x_ref[...] + jnp.dot(m_ref[...], w_ref[...], preferred_element_type=F32)
    x1_ref[...] = x1
    ms = jnp.mean(x1 * x1, axis=-1, keepdims=True)
    h_ref[...] = (x1 * lax.rsqrt(ms + EPS) * g_ref[...]).astype(BF16)


def _ffn_kernel(h_ref, halo_ref, x1_ref, wa_ref, wb_ref, wd_ref, cw_ref, cb_ref, st0_ref, st1_ref,
                gf_ref, y_ref, head_ref, tail_ref, acc_ref, s1_ref, s2_ref, *, n_dec):
    i = pl.program_id(0)
    j = pl.program_id(1)
    h = h_ref[...]
    a = jnp.dot(h, wa_ref[...], preferred_element_type=F32)
    b = jnp.dot(h, wb_ref[...], preferred_element_type=F32)
    ah = jnp.dot(halo_ref[...], wa_ref[...], preferred_element_type=F32)
    ah = jnp.where(i > 0, ah, 0.0)
    row = lax.broadcasted_iota(I32, a.shape, 0)
    hl = ah.shape[0]
    s1_ref[...] = jnp.where(row == 0, ah[hl - 1:hl, :], pltpu.roll(a, 1, 0))
    s2_ref[...] = jnp.where(row == 0, ah[hl - 2:hl - 1, :],
                            jnp.where(row == 1, ah[hl - 1:hl, :], pltpu.roll(a, 2, 0)))

    @pl.when(i == 0)
    def _():
        s1_ref[0:n_dec, :] = st1_ref[...]
        s2_ref[0:n_dec, :] = st0_ref[...]

    conv = cb_ref[...] + cw_ref[0:1, :] * s2_ref[...] + cw_ref[1:2, :] * s1_ref[...] + cw_ref[2:3, :] * a
    gate = (conv * _sigmoid(conv) * b).astype(BF16)
    part = jnp.dot(gate, wd_ref[...], preferred_element_type=F32)

    @pl.when(j == 0)
    def _():
        acc_ref[...] = part

    @pl.when(j > 0)
    def _():
        acc_ref[...] = acc_ref[...] + part

    head_ref[...] = a[0:head_ref.shape[0], :]
    tail_ref[...] = a[a.shape[0] - 8:, :]

    @pl.when(j == pl.num_programs(1) - 1)
    def _():
        x2 = x1_ref[...] + acc_ref[...]
        ms = jnp.mean(x2 * x2, axis=-1, keepdims=True)
        y_ref[...] = x2 * lax.rsqrt(ms + EPS) * gf_ref[...]


def _rope_tables(pos, hd, reps):
    rot = hd // 4
    half = rot // 2
    inv = jnp.exp(-math.log(ROPE_THETA) * jnp.arange(half, dtype=F32) * 2.0 / rot)
    ang = pos.astype(F32)[:, None] * inv[None, :]
    cos, sin = jnp.cos(ang), jnp.sin(ang)
    n = pos.shape[0]
    one = jnp.ones((n, hd - rot), F32)
    zero_r = jnp.zeros((n, hd - rot), F32)
    zero_h = jnp.zeros((n, half), F32)
    c = jnp.concatenate([cos, cos, one], axis=1)
    sa = jnp.concatenate([-sin, zero_h, zero_r], axis=1)
    sb = jnp.concatenate([zero_h, sin, zero_r], axis=1)
    return [jnp.tile(t, (1, reps)) for t in (c, sa, sb)]


def kernel(x_prompt, x_sample, cache_k, cache_v, cache_idx_k, state_gla, state_conv, page_table, meta_tokens, norm_mix_g, w_in, w_alpha_up, b_alpha, gla_norm_g, w_branch_a, w_branch_b, w_out, norm_ffn_g, w_up, conv_w, conv_b, w_down, norm_final_g):
    B, SEQ, D = x_prompt.shape
    DB = x_sample.shape[0]
    assert x_sample.shape[1] == 1 and w_in.shape[0] == 1
    n_pool = cache_k.shape[1]
    n_pages = page_table.shape[1]
    past = n_pages * PAGE_SIZE
    assert n_pages == LANE and PAGE_SIZE == LANE
    dff = w_down.shape[1]
    H = GLA_HEADS
    dk = D // 2 // H
    dv = D // H
    hd = D // ATT_HEADS
    kvw = ATT_KV_HEADS * hd
    T = SEQ + N_META
    TP = SEQ + FRONT
    R = B * TP
    assert DB <= GLA_CHUNK and DB % 16 == 0 and SEQ % LANE == 0 and dff % COL_TILE == 0
    k_sel_p = min(TOPK_MAX, T // 4)
    k_sel_s = min(TOPK_MAX, (past + 1) // 4)

    sizes = (H * dk, H * dk, H * dv, H * dv, GLA_GATE_RANK, ATT_HEADS * hd, kvw, kvw,
             IDX_HEADS * IDX_DIM, IDX_DIM, IDX_HEADS, D, D)
    offs = [0]
    for s_ in sizes:
        offs.append(offs[-1] + s_)
    w0 = w_in[0]
    wg_a = w0[:, offs[0]:offs[4]].astype(BF16)
    wg_b = w0[:, offs[5]:offs[9]].astype(BF16)
    wg_c = w0[:, offs[11]:offs[13]].astype(BF16)
    small_pad = COL_TILE - (IDX_DIM + GLA_GATE_RANK + IDX_HEADS)
    wg_s = jnp.concatenate([w0[:, offs[9]:offs[10]], w0[:, offs[4]:offs[5]], w0[:, offs[10]:offs[11]],
                            jnp.zeros((D, small_pad), F32)], axis=1).astype(BF16)
    n_a, n_b, n_c = (wg_a.shape[1] // COL_TILE, wg_b.shape[1] // COL_TILE, wg_c.shape[1] // COL_TILE)
    assert all(w_.shape[1] % COL_TILE == 0 for w_ in (wg_a, wg_b, wg_c)) and hd == LANE
    QA, KA, VA, RA = 0, sizes[0], sizes[0] + sizes[1], sizes[0] + sizes[1] + sizes[2]
    QB = n_a * COL_TILE
    KB, VB, QI = QB + sizes[5], QB + sizes[5] + sizes[6], QB + sizes[5] + sizes[6] + sizes[7]
    GA = (n_a + n_b) * COL_TILE
    GB = GA + D
    SM = (n_a + n_b + n_c) * COL_TILE
    NW = SM + COL_TILE
    assert all(v_ % COL_TILE == 0 for v_ in (QA, KA, VA, RA, QB, KB, VB, QI, GA, GB))
    assert VB - KB == COL_TILE
    n_ct = NW // COL_TILE
    wau_pad = jnp.zeros((LANE, H * dk), F32).at[IDX_DIM:IDX_DIM + GLA_GATE_RANK].set(w_alpha_up[0]).astype(BF16)
    wa_bf = w_branch_a[0].astype(BF16)
    wb_bf = w_branch_b[0].astype(BF16)
    wo_bf = w_out[0].astype(BF16)
    wup_bf = w_up[0].astype(BF16)
    wdn_bf = w_down[0].astype(BF16)

    front = jnp.zeros((B, ROW0, D), F32).at[0, :DB].set(x_sample[:, 0])
    meta = jnp.broadcast_to(meta_tokens[None].astype(F32), (B, N_META, D))
    x_all = jnp.concatenate([front, meta, x_prompt], axis=1).reshape(R, D)
    rpos = jnp.maximum(jnp.arange(TP, dtype=jnp.int32) - ROW0, 0)
    pos = jnp.concatenate([rpos.at[:DB].set(past), rpos])
    tabs = jnp.concatenate(_rope_tables(pos, hd, 1) + _rope_tables(pos, IDX_DIM, LANE // IDX_DIM), axis=1)

    tm1 = TP // 2
    bpb = TP // tm1
    P = pl.pallas_call(
        functools.partial(_inproj_kernel, n_a=n_a, n_b=n_b, n_c=n_c,
                          b128=(0, (VB - QB) // COL_TILE),
                          b64=((QI - QB) // COL_TILE, (QI - QB + IDX_HEADS * IDX_DIM) // COL_TILE)),
        grid=(R // tm1, n_ct),
        in_specs=[pl.BlockSpec((tm1, D), lambda i, j: (i, 0)),
                  pl.BlockSpec((1, D), lambda i, j: (0, 0)),
                  pl.BlockSpec((D, COL_TILE), lambda i, j: (0, jnp.minimum(j, n_a - 1))),
                  pl.BlockSpec((D, COL_TILE), lambda i, j: (0, jnp.clip(j - n_a, 0, n_b - 1))),
                  pl.BlockSpec((D, COL_TILE), lambda i, j: (0, jnp.clip(j - n_a - n_b, 0, n_c - 1))),
                  pl.BlockSpec((D, COL_TILE), lambda i, j: (0, 0)),
                  pl.BlockSpec((tm1, 6 * LANE), lambda i, j: (jnp.where(i < bpb, i, bpb + i % bpb), 0))],
        out_specs=pl.BlockSpec((tm1, COL_TILE), lambda i, j: (i, j)),
        out_shape=jax.ShapeDtypeStruct((R, NW), F32),
        scratch_shapes=[pltpu.VMEM((tm1, D), BF16)],
        compiler_params=_cparams(("parallel", "arbitrary")),
        name="inproj",
    )(x_all, norm_mix_g, wg_a, wg_b, wg_c, wg_s, tabs)

    C = GLA_CHUNK
    ncb = TP // C
    ba2 = b_alpha.reshape(1, H * dk)
    gain2 = gla_norm_g.reshape(1, H * dv)
    smc = SM // LANE
    a_out, s_fin = pl.pallas_call(
        functools.partial(_gla_prompt_kernel, dk=dk),
        grid=(B, H, ncb),
        in_specs=[pl.BlockSpec((C, dk), lambda b, h, c: (b * ncb + c, QA // dk + h)),
                  pl.BlockSpec((C, dk), lambda b, h, c: (b * ncb + c, KA // dk + h)),
                  pl.BlockSpec((C, dv), lambda b, h, c: (b * ncb + c, VA // dv + h)),
                  pl.BlockSpec((C, dv), lambda b, h, c: (b * ncb + c, RA // dv + h)),
                  pl.BlockSpec((C, LANE), lambda b, h, c: (b * ncb + c, smc)),
                  pl.BlockSpec((LANE, dk), lambda b, h, c: (0, h)),
                  pl.BlockSpec((1, dk), lambda b, h, c: (0, h)),
                  pl.BlockSpec((1, dv), lambda b, h, c: (0, h))],
        out_specs=[pl.BlockSpec((C, dv), lambda b, h, c: (b * ncb + c, h)),
                   pl.BlockSpec((None, None, dk, dv), lambda b, h, c: (b, h, 0, 0))],
        out_shape=[jax.ShapeDtypeStruct((R, H * dv), BF16),
                   jax.ShapeDtypeStruct((B, H, dk, dv), F32)],
        scratch_shapes=[pltpu.VMEM((dk, dv), F32)],
        compiler_params=_cparams(("parallel", "parallel", "arbitrary")),
        name="gla_prompt",
    )(P, P, P, P, P, wau_pad, ba2, gain2)

    p_s = P[:DB]
    P_rows = p_s.reshape(DB, 1, NW)
    a_out, s_new_s = pl.pallas_call(
        functools.partial(_gla_sample_kernel, dk=dk),
        grid=(H, DB),
        in_specs=[pl.BlockSpec(memory_space=pl.ANY),
                  pl.BlockSpec((None, 1, dk), lambda h, d: (d, 0, QA // dk + h)),
                  pl.BlockSpec((None, 1, dk), lambda h, d: (d, 0, KA // dk + h)),
                  pl.BlockSpec((None, 1, dv), lambda h, d: (d, 0, VA // dv + h)),
                  pl.BlockSpec((None, 1, dv), lambda h, d: (d, 0, RA // dv + h)),
                  pl.BlockSpec((None, 1, LANE), lambda h, d: (d, 0, smc)),
                  pl.BlockSpec((LANE, dk), lambda h, d: (0, h)),
                  pl.BlockSpec((1, dk), lambda h, d: (0, h)),
                  pl.BlockSpec((1, dv), lambda h, d: (0, h)),
                  pl.BlockSpec((None, None, dk, dv), lambda h, d: (d, h, 0, 0))],
        out_specs=[pl.BlockSpec((DB, dv), lambda h, d: (0, h)),
                   pl.BlockSpec((None, None, dk, dv), lambda h, d: (d, h, 0, 0))],
        out_shape=[jax.ShapeDtypeStruct((R, H * dv), BF16),
                   jax.ShapeDtypeStruct((DB, H, dk, dv), F32)],
        scratch_shapes=[pltpu.VMEM((DB, dv), F32)],
        input_output_aliases={0: 0},
        compiler_params=_cparams(("arbitrary", "arbitrary")),
        name="gla_sample",
    )(a_out, P_rows, P_rows, P_rows, P_rows, P_rows, wau_pad, ba2, gain2, state_gla[0])

    TQ = LANE
    nqb = TP // TQ
    n_cls = -(-nqb // QB_PER_CLASS)
    widths = tuple(min((c + 1) * QB_PER_CLASS * TQ, TP) for c in range(n_cls))
    o_b = jnp.zeros((R, ATT_HEADS * hd), BF16)
    for c, width in enumerate(widths):
        q_lo = c * QB_PER_CLASS
        n_q = min(QB_PER_CLASS, nqb - q_lo)
        row_blk = lambda b, q, q_lo=q_lo: b * nqb + q_lo + q
        specs = [pl.BlockSpec((TQ, IDX_HEADS * IDX_DIM), lambda b, q, r_=row_blk: (r_(b, q), QI // (IDX_HEADS * IDX_DIM))),
                 pl.BlockSpec((TQ, LANE), lambda b, q, r_=row_blk: (r_(b, q), smc)),
                 pl.BlockSpec((TP, LANE), lambda b, q: (b, smc)),
                 pl.BlockSpec((TQ, ATT_HEADS * hd), lambda b, q, r_=row_blk: (r_(b, q), QB // (ATT_HEADS * hd))),
                 pl.BlockSpec((TP, kvw), lambda b, q: (b, KB // kvw)),
                 pl.BlockSpec((TP, kvw), lambda b, q: (b, VB // kvw))]
        o_b = pl.pallas_call(
            functools.partial(_dsa_prompt_kernel, k_sel=k_sel_p, hd=hd, width=width, q_lo=q_lo),
            grid=(B, n_q),
            in_specs=[pl.BlockSpec(memory_space=pl.ANY)] + specs,
            out_specs=pl.BlockSpec((TQ, ATT_HEADS * hd), lambda b, q, r_=row_blk: (r_(b, q), 0)),
            out_shape=jax.ShapeDtypeStruct((R, ATT_HEADS * hd), BF16),
            scratch_shapes=[pltpu.VMEM((width, LANE), BF16), pltpu.VMEM((width, LANE), BF16),
                            pltpu.VMEM((width, kvw), BF16), pltpu.VMEM((width, kvw), BF16),
                            pltpu.VMEM((TQ, width), I32)],
            input_output_aliases={0: 0},
            compiler_params=_cparams(("parallel", "arbitrary")),
            name=f"dsa_prompt_w{width}",
        )(o_b, P, P, P, P, P, P)

    qi_s = p_s[:, QI:QI + IDX_HEADS * IDX_DIM].reshape(DB, IDX_HEADS, IDX_DIM)
    wi_s = p_s[:, SM + WI_OFF:SM + WI_OFF + IDX_HEADS].reshape(DB, IDX_HEADS, 1)
    ki_s = p_s[:, SM:SM + IDX_DIM].reshape(DB, 1, IDX_DIM)
    sc, scn = pl.pallas_call(
        functools.partial(_dsa_sample_score_kernel, n_pages=n_pages),
        grid_spec=pltpu.PrefetchScalarGridSpec(
            num_scalar_prefetch=1,
            grid=(DB,),
            in_specs=[pl.BlockSpec(memory_space=pl.ANY),
                      pl.BlockSpec((None, IDX_HEADS, IDX_DIM), lambda d, pt: (d, 0, 0)),
                      pl.BlockSpec((None, IDX_HEADS, 1), lambda d, pt: (d, 0, 0)),
                      pl.BlockSpec((None, 1, IDX_DIM), lambda d, pt: (d, 0, 0))],
            out_specs=[pl.BlockSpec((None, n_pages, PAGE_SIZE), lambda d, pt: (d, 0, 0)),
                       pl.BlockSpec((None, 1, LANE), lambda d, pt: (d, 0, 0))],
            scratch_shapes=[pltpu.VMEM((2, n_pages, IDX_DIM, PAGE_SIZE), F32),
                            pltpu.SemaphoreType.DMA((2,))]),
        out_shape=[jax.ShapeDtypeStruct((DB, n_pages, PAGE_SIZE), F32),
                   jax.ShapeDtypeStruct((DB, 1, LANE), F32)],
        compiler_params=_cparams(("arbitrary",)),
        name="dsa_sample_score",
    )(page_table, jnp.swapaxes(cache_idx_k[0], 1, 2), qi_s, wi_s, ki_s)

    sidx = lax.broadcasted_iota(jnp.int32, (past + LANE, LANE), 0)
    hl_lane = lax.broadcasted_iota(jnp.int32, (past + LANE, LANE), 1)
    hl = jnp.where(hl_lane == 0, sidx // PAGE_SIZE, jnp.where(hl_lane == 1, sidx % PAGE_SIZE, 0)).astype(BF16)
    sel = pl.pallas_call(
        functools.partial(_dsa_sample_select_kernel, k_sel=k_sel_s, chunk=2048),
        out_shape=jax.ShapeDtypeStruct((DB, k_sel_s, LANE), I32),
        scratch_shapes=[pltpu.VMEM((DB, 1, past + LANE), F32), pltpu.VMEM((DB, past + LANE), I32)],
        compiler_params=pltpu.CompilerParams(vmem_limit_bytes=VMEM_LIMIT),
        name="dsa_sample_select",
    )(sc.reshape(DB, past), scn.reshape(DB, LANE), hl)

    sel_s = sel[:, :, 0:2].reshape(DB, 2 * k_sel_s)
    q_s = p_s[:, QB:QB + ATT_HEADS * hd].reshape(DB, ATT_HEADS, hd)
    kn_s = p_s[:, KB:KB + kvw]
    vn_s = p_s[:, VB:VB + kvw]
    o_b = pl.pallas_call(
        functools.partial(_dsa_sample_attend_kernel, k_sel=k_sel_s, n_pages=n_pages, hd=hd),
        grid_spec=pltpu.PrefetchScalarGridSpec(
            num_scalar_prefetch=2,
            grid=(DB,),
            in_specs=[pl.BlockSpec(memory_space=pl.ANY),
                      pl.BlockSpec(memory_space=pl.ANY),
                      pl.BlockSpec(memory_space=pl.ANY),
                      pl.BlockSpec((None, k_sel_s, LANE), lambda d, s_, pt: (d, 0, 0)),
                      pl.BlockSpec((None, ATT_HEADS, hd), lambda d, s_, pt: (d, 0, 0)),
                      pl.BlockSpec((None, 1, kvw), lambda d, s_, pt: (d, 0, KB // kvw)),
                      pl.BlockSpec((None, 1, kvw), lambda d, s_, pt: (d, 0, VB // kvw))],
            out_specs=pl.BlockSpec((DB, ATT_HEADS * hd), lambda d, s_, pt: (0, 0)),
            scratch_shapes=[pltpu.VMEM((2, k_sel_s, ATT_KV_HEADS, hd), F32),
                            pltpu.VMEM((2, k_sel_s, ATT_KV_HEADS, hd), F32),
                            pltpu.VMEM((DB, ATT_HEADS, hd), F32), pltpu.SemaphoreType.DMA((2, 2))]),
        out_shape=jax.ShapeDtypeStruct((R, ATT_HEADS * hd), BF16),
        input_output_aliases={2: 0},
        compiler_params=_cparams(("arbitrary",)),
        name="dsa_sample_attend",
    )(sel_s, page_table, o_b, cache_k[0], cache_v[0], sel, q_s, P_rows, P_rows)

    mix = pl.pallas_call(
        _mix_kernel,
        grid=(R // tm1, D // COL_TILE),
        in_specs=[pl.BlockSpec((tm1, H * dv), lambda i, j: (i, 0)),
                  pl.BlockSpec((tm1, ATT_HEADS * hd), lambda i, j: (i, 0)),
                  pl.BlockSpec((H * dv, COL_TILE), lambda i, j: (0, j)),
                  pl.BlockSpec((ATT_HEADS * hd, COL_TILE), lambda i, j: (0, j)),
                  pl.BlockSpec((tm1, COL_TILE), lambda i, j: (i, GA // COL_TILE + j)),
                  pl.BlockSpec((tm1, COL_TILE), lambda i, j: (i, GB // COL_TILE + j))],
        out_specs=pl.BlockSpec((tm1, COL_TILE), lambda i, j: (i, j)),
        out_shape=jax.ShapeDtypeStruct((R, D), BF16),
        compiler_params=_cparams(("parallel", "arbitrary")),
        name="mix",
    )(a_out, o_b, wa_bf, wb_bf, P, P)

    tm2 = TP // 4
    x1, h2 = pl.pallas_call(
        _outproj_kernel,
        grid=(R // tm2,),
        in_specs=[pl.BlockSpec((tm2, D), lambda i: (i, 0)),
                  pl.BlockSpec((tm2, D), lambda i: (i, 0)),
                  pl.BlockSpec((D, D), lambda i: (0, 0)),
                  pl.BlockSpec((1, D), lambda i: (0, 0))],
        out_specs=[pl.BlockSpec((tm2, D), lambda i: (i, 0)),
                   pl.BlockSpec((tm2, D), lambda i: (i, 0))],
        out_shape=[jax.ShapeDtypeStruct((R, D), F32), jax.ShapeDtypeStruct((R, D), BF16)],
        compiler_params=_cparams(("parallel",)),
        name="outproj",
    )(x_all, mix, wo_bf, norm_ffn_g)

    nrb = R // tm2
    nft = dff // COL_TILE
    y_all, a_head, a_tail = pl.pallas_call(
        functools.partial(_ffn_kernel, n_dec=DB),
        grid=(nrb, nft),
        in_specs=[pl.BlockSpec((tm2, D), lambda i, j: (i, 0)),
                  pl.BlockSpec((16, D), lambda i, j: (jnp.maximum(i * (tm2 // 16) - 1, 0), 0)),
                  pl.BlockSpec((tm2, D), lambda i, j: (i, 0)),
                  pl.BlockSpec((D, COL_TILE), lambda i, j: (0, j)),
                  pl.BlockSpec((D, COL_TILE), lambda i, j: (0, nft + j)),
                  pl.BlockSpec((COL_TILE, D), lambda i, j: (j, 0)),
                  pl.BlockSpec((CONV_W, COL_TILE), lambda i, j: (0, j)),
                  pl.BlockSpec((1, COL_TILE), lambda i, j: (0, j)),
                  pl.BlockSpec((DB, COL_TILE), lambda i, j: (0, j)),
                  pl.BlockSpec((DB, COL_TILE), lambda i, j: (0, j)),
                  pl.BlockSpec((1, D), lambda i, j: (0, 0))],
        out_specs=[pl.BlockSpec((tm2, D), lambda i, j: (i, 0)),
                   pl.BlockSpec((DB, COL_TILE), lambda i, j: (i, j)),
                   pl.BlockSpec((8, COL_TILE), lambda i, j: (i, j))],
        out_shape=[jax.ShapeDtypeStruct((R, D), F32),
                   jax.ShapeDtypeStruct((nrb * DB, dff), F32),
                   jax.ShapeDtypeStruct((nrb * 8, dff), F32)],
        scratch_shapes=[pltpu.VMEM((tm2, D), F32), pltpu.VMEM((tm2, COL_TILE), F32),
                        pltpu.VMEM((tm2, COL_TILE), F32)],
        compiler_params=_cparams(("parallel", "arbitrary")),
        name="ffn",
    )(h2, h2, x1, wup_bf, wup_bf, wdn_bf, conv_w[0], conv_b, state_conv[0, :, 0], state_conv[0, :, 1],
      norm_final_g.reshape(1, D))

    y3 = y_all.reshape(B, TP, D)
    y_prompt = y3[:, FRONT:]
    y_sample = y_all[:DB].reshape(DB, 1, D)
    P3 = P.reshape(B, TP, NW)
    new_k_p = P3[:, ROW0:, KB:KB + kvw].reshape(1, B, T, ATT_KV_HEADS, hd)
    new_v_p = P3[:, ROW0:, VB:VB + kvw].reshape(1, B, T, ATT_KV_HEADS, hd)
    new_ki_p = P3[:, ROW0:, SM:SM + IDX_DIM].reshape(1, B, T, IDX_DIM)
    new_gla_p = s_fin[None]
    tails = a_tail.reshape(B, nrb // B, 8, dff)[:, -1, 8 - (CONV_W - 1):, :]
    new_conv_p = tails[None]
    new_k_s = kn_s.reshape(1, DB, 1, ATT_KV_HEADS, hd)
    new_v_s = vn_s.reshape(1, DB, 1, ATT_KV_HEADS, hd)
    new_ki_s = ki_s.reshape(1, DB, 1, IDX_DIM)
    new_gla_s = s_new_s[None]
    new_conv_s = jnp.stack([state_conv[0, :, 1], a_head[:DB]], axis=1)[None]
    return (y_prompt, y_sample, new_k_p, new_v_p, new_ki_p, new_gla_p, new_conv_p,
            new_k_s, new_v_s, new_ki_s, new_gla_s, new_conv_s)
```

```python
import functools
import math

import jax
import jax.numpy as jnp
from jax import lax
from jax.experimental import pallas as pl
from jax.experimental.pallas import tpu as pltpu

F32 = jnp.float32
BF16 = jnp.bfloat16
I32 = jnp.int32

N_META = 16
GLA_HEADS = 4
GLA_GATE_RANK = 16
GLA_GATE_NORM = 16.0
GLA_CHUNK = 64
GLA_SUB = 8
ATT_HEADS = 16
ATT_KV_HEADS = 4
IDX_HEADS = 16
IDX_DIM = 64
TOPK_MAX = 256
ROPE_THETA = 500000.0
CONV_W = 3
EPS = 1e-6
PAGE_SIZE = 128
WI_OFF = IDX_DIM + GLA_GATE_RANK

LANE = 128
FRONT = 128
ROW0 = FRONT - N_META
COL_TILE = 512
QB_PER_CLASS = 2
VMEM_LIMIT = 56 * 1024 * 1024
INT_MIN = -2 ** 31
LOG2_E = 1.4426950408889634


def _cparams(sem):
    return pltpu.CompilerParams(dimension_semantics=sem, vmem_limit_bytes=VMEM_LIMIT)


def _sigmoid(x):
    return 1.0 / (1.0 + jnp.exp(-x))


def _pattern_value(p):
    return pltpu.bitcast(jnp.where(p < 0, p ^ jnp.int32(0x7FFFFFFF), p), F32)


def _kth_largest(load_scores, rows, k_sel):
    def body(it, t):
        step = jnp.left_shift(jnp.int32(1), 30 - 2 * it)
        scores = load_scores()
        adv = jnp.zeros((rows, 1), I32)
        for m in (1, 2, 3):
            cnt = jnp.sum((scores >= _pattern_value(t + m * step)).astype(I32), axis=1, keepdims=True)
            adv = adv + (cnt >= k_sel).astype(I32)
        return t + adv * step

    t = lax.fori_loop(0, 16, body, jnp.full((rows, 1), INT_MIN, I32))
    return jnp.where(t == INT_MIN, -jnp.inf, _pattern_value(t))


def _rope_block(xb, c, sa, sb, half):
    return xb * c + pltpu.roll(xb, LANE - half, 1) * sa + pltpu.roll(xb, half, 1) * sb


def _inproj_kernel(x_ref, g_ref, wa_ref, wb_ref, wc_ref, ws_ref, tab_ref, o_ref, h_ref,
                   *, n_a, n_b, n_c, b128, b64):
    j = pl.program_id(1)

    @pl.when(j == 0)
    def _():
        x = x_ref[...]
        ms = jnp.mean(x * x, axis=-1, keepdims=True)
        h_ref[...] = (x * lax.rsqrt(ms + EPS) * g_ref[...]).astype(BF16)

    def proj(w_ref):
        return jnp.dot(h_ref[...], w_ref[...], preferred_element_type=F32)

    @pl.when(j < n_a)
    def _():
        o_ref[...] = proj(wa_ref)

    @pl.when(jnp.logical_and(j >= n_a, j < n_a + n_b))
    def _():
        acc = proj(wb_ref)
        jb = j - n_a
        is128 = jnp.logical_and(jb >= b128[0], jb < b128[1])
        is64 = jnp.logical_and(jb >= b64[0], jb < b64[1])

        @pl.when(is128)
        def _():
            c, sa, sb = tab_ref[:, 0:128], tab_ref[:, 128:256], tab_ref[:, 256:384]
            for blk in range(COL_TILE // LANE):
                sl = slice(blk * LANE, (blk + 1) * LANE)
                o_ref[:, sl] = _rope_block(acc[:, sl], c, sa, sb, 16)

        @pl.when(is64)
        def _():
            c, sa, sb = tab_ref[:, 384:512], tab_ref[:, 512:640], tab_ref[:, 640:768]
            for blk in range(COL_TILE // LANE):
                sl = slice(blk * LANE, (blk + 1) * LANE)
                o_ref[:, sl] = _rope_block(acc[:, sl], c, sa, sb, 8)

        @pl.when(jnp.logical_not(is128 | is64))
        def _():
            o_ref[...] = acc

    @pl.when(jnp.logical_and(j >= n_a + n_b, j < n_a + n_b + n_c))
    def _():
        o_ref[...] = proj(wc_ref)

    @pl.when(j == n_a + n_b + n_c)
    def _():
        acc = proj(ws_ref)
        lane = lax.broadcasted_iota(I32, (1, LANE), 1)
        first = lane < IDX_DIM
        c = jnp.where(first, tab_ref[:, 384:512], 1.0)
        sa = jnp.where(first, tab_ref[:, 512:640], 0.0)
        sb = jnp.where(first, tab_ref[:, 640:768], 0.0)
        o_ref[:, 0:LANE] = _rope_block(acc[:, 0:LANE], c, sa, sb, 8)
        o_ref[:, LANE:] = acc[:, LANE:]


def _log_alpha(sm, wau_ref, ba_ref):
    x = jnp.dot(sm.astype(BF16), wau_ref[...], preferred_element_type=F32) + ba_ref[...]
    return (jnp.minimum(x, 0.0) - jnp.log(1.0 + jnp.exp(-jnp.abs(x)))) * (1.0 / GLA_GATE_NORM)


def _row_to_col(row, n):
    eye = lax.broadcasted_iota(I32, (n, n), 0) == lax.broadcasted_iota(I32, (n, n), 1)
    return jnp.sum(jnp.where(eye, row, 0.0), axis=1, keepdims=True)


def _readout(o, r, gain):
    ms = jnp.mean(o * o, axis=-1, keepdims=True)
    return o * lax.rsqrt(ms + EPS) * gain * (r * _sigmoid(r))


def _gla_chunk(q, k, v, g, S):
    C, dk = q.shape
    vb = v.astype(BF16)
    row = lax.broadcasted_iota(I32, g.shape, 0)
    b = g
    sh = 1
    while sh < C:
        b = b + jnp.where(row >= sh, pltpu.roll(b, sh, 0), 0.0)
        sh *= 2
    bl = b[C - 1:C, :]
    o = jnp.dot((q * jnp.exp(b)).astype(BF16), S.astype(BF16), preferred_element_type=F32)

    tcol = lax.broadcasted_iota(I32, (GLA_SUB, C), 1)
    trow = lax.broadcasted_iota(I32, (GLA_SUB, C), 0)
    a_rows = []
    kf = None
    prev_ref = None
    for blk in range(C // GLA_SUB):
        r0 = blk * GLA_SUB
        ref = b[r0 - 1:r0, :] if blk > 0 else jnp.zeros_like(bl)
        b_i = b[r0:r0 + GLA_SUB, :]
        q_i = q[r0:r0 + GLA_SUB, :]
        if blk == 0:
            a_off = jnp.zeros((GLA_SUB, C), F32)
        else:
            fresh = k[r0 - GLA_SUB:r0, :] * jnp.exp(ref - b[r0 - GLA_SUB:r0, :])
            kf = fresh if blk == 1 else jnp.concatenate([kf * jnp.exp(ref - prev_ref), fresh], axis=0)
            kf_full = jnp.concatenate([kf, jnp.zeros((C - r0, dk), F32)], axis=0)
            qe = q_i * jnp.exp(b_i - ref)
            a_off = lax.dot_general(qe.astype(BF16), kf_full.astype(BF16), (((1,), (1,)), ((), ())),
                                    preferred_element_type=F32)
        prev_ref = ref
        diag = jnp.zeros((GLA_SUB, C), F32)
        for sl in range(GLA_SUB):
            s = r0 + sl
            w = q_i * k[s:s + 1, :] * jnp.exp(jnp.minimum(b_i - b[s:s + 1, :], 0.0))
            diag = jnp.where(tcol == s, jnp.sum(w, axis=1, keepdims=True), diag)
        a_rows.append(jnp.where(tcol < r0, a_off, jnp.where(tcol <= trow + r0, diag, 0.0)))
    a = jnp.concatenate(a_rows, axis=0)
    o = o + jnp.dot(a.astype(BF16), vb, preferred_element_type=F32)

    kd = k * jnp.exp(bl - b)
    upd = lax.dot_general(kd.astype(BF16), vb, (((0,), (0,)), ((), ())),
                          preferred_element_type=F32)
    return o, S * _row_to_col(jnp.exp(bl), dk) + upd


def _gla_prompt_kernel(q_ref, k_ref, v_ref, r_ref, sm_ref, wau_ref, ba_ref, gain_ref,
                       o_ref, sfin_ref, s_ref, *, dk, dv):
    c = pl.program_id(1)
    H = s_ref.shape[0]

    @pl.when(c == 0)
    def _():
        s_ref[...] = jnp.zeros_like(s_ref)
        o_ref[...] = jnp.zeros_like(o_ref)

    @pl.when(c > 0)
    def _():
        g_all = _log_alpha(sm_ref[...], wau_ref, ba_ref)
        for h in range(H):
            ks, vs = slice(h * dk, (h + 1) * dk), slice(h * dv, (h + 1) * dv)
            o, s_new = _gla_chunk(q_ref[:, ks] * (dk ** -0.5), k_ref[:, ks], v_ref[:, vs],
                                  g_all[:, ks], s_ref[h])
            s_ref[h] = s_new
            o_ref[:, vs] = _readout(o, r_ref[:, vs], gain_ref[:, vs]).astype(BF16)

    @pl.when(c == pl.num_programs(1) - 1)
    def _():
        sfin_ref[...] = s_ref[...]


def _gla_sample_kernel(alias_ref, q_ref, k_ref, v_ref, r_ref, sm_ref, wau_ref, ba_ref, gain_ref,
                       st_ref, o_ref, snew_ref, acc_ref, *, dk, dv):
    del alias_ref
    d = pl.program_id(0)
    H = st_ref.shape[0]
    g_all = _log_alpha(jnp.broadcast_to(sm_ref[...], (8, LANE)), wau_ref, ba_ref)[0:1, :]
    ons = []
    for h in range(H):
        ks, vs = slice(h * dk, (h + 1) * dk), slice(h * dv, (h + 1) * dv)
        q = q_ref[:, ks] * (dk ** -0.5)
        k = k_ref[:, ks]
        v = v_ref[:, vs]
        eg = jnp.exp(g_all[:, ks])
        S = st_ref[h]
        qe = jnp.broadcast_to(q * eg, (8, dk))
        o = jnp.dot(qe.astype(BF16), S.astype(BF16), preferred_element_type=F32)[0:1, :]
        o = o + jnp.sum(q * k, axis=1, keepdims=True) * v
        snew_ref[h] = S * _row_to_col(eg, dk) + _row_to_col(k, dk) * v
        ons.append(_readout(o, r_ref[:, vs], gain_ref[:, vs]))
    on = jnp.concatenate(ons, axis=1)
    row = lax.broadcasted_iota(I32, acc_ref.shape, 0)

    @pl.when(d == 0)
    def _():
        acc_ref[...] = jnp.zeros_like(acc_ref)

    acc_ref[...] = jnp.where(row == d, on, acc_ref[...])

    @pl.when(d == pl.num_programs(0) - 1)
    def _():
        o_ref[...] = acc_ref[...].astype(BF16)


def _dsa_prompt_kernel(alias_ref, qi_ref, smq_ref, smk_ref, qb_ref, kb_ref, vb_ref, o_ref,
                       ka_ref, kbb_ref, kbf_ref, vbf_ref, key_ref, *, k_sel, hd, width, q_lo):
    del alias_ref
    qb = q_lo + pl.program_id(1)
    TQ = qi_ref.shape[0]
    nt = (((1,), (1,)), ((), ()))
    G = ATT_HEADS // ATT_KV_HEADS

    @pl.when(pl.program_id(1) == 0)
    def _():
        smk = smk_ref[0:width, :]
        lane = lax.broadcasted_iota(I32, (1, LANE), 1)
        ka_ref[...] = jnp.where(lane < IDX_DIM, smk, 0.0).astype(BF16)
        kbb_ref[...] = jnp.where(lane >= IDX_DIM, pltpu.roll(smk, IDX_DIM, 1), 0.0).astype(BF16)
        kbf_ref[...] = kb_ref[0:width, :].astype(BF16)
        vbf_ref[...] = vb_ref[0:width, :].astype(BF16)

    def body(TK):
        score = jnp.zeros((TQ, TK), F32)
        for p in range(IDX_HEADS // 2):
            qp = qi_ref[:, p * LANE:(p + 1) * LANE].astype(BF16)
            for half, kref in ((0, ka_ref), (1, kbb_ref)):
                h = 2 * p + half
                s = lax.dot_general(qp, kref[0:TK, :], nt, preferred_element_type=F32)
                w = smq_ref[:, WI_OFF + h:WI_OFF + h + 1] * (IDX_HEADS ** -0.5 * IDX_DIM ** -0.5)
                score = score + w * jnp.maximum(s, 0.0)

        qrow = qb * TQ + lax.broadcasted_iota(I32, (TQ, 1), 0)
        kcol = lax.broadcasted_iota(I32, (1, TK), 1)
        adm = jnp.logical_and(kcol <= qrow, kcol >= ROW0)
        key_ref[:, 0:TK] = jnp.where(adm, score, -jnp.inf)
        thr = _kth_largest(lambda: key_ref[:, 0:TK], TQ, k_sel)
        sc = key_ref[:, 0:TK]
        sel = jnp.logical_and(sc >= thr, sc > -jnp.inf)

        for n in range(ATT_KV_HEADS):
            q4 = jnp.concatenate(
                [(qb_ref[:, (n * G + gq) * hd:(n * G + gq + 1) * hd] * (hd ** -0.5 * LOG2_E)).astype(BF16)
                 for gq in range(G)], axis=0)
            s = lax.dot_general(q4, kbf_ref[0:TK, n * hd:(n + 1) * hd], nt, preferred_element_type=F32)
            s = jnp.where(sel[None], s.reshape(G, TQ, TK), -jnp.inf).reshape(G * TQ, TK)
            m = jnp.max(s, axis=1, keepdims=True)
            m = jnp.where(m == -jnp.inf, 0.0, m)
            p = jnp.exp2(s - m)
            l = jnp.sum(p, axis=1, keepdims=True)
            o = jnp.dot(p.astype(BF16), vbf_ref[0:TK, n * hd:(n + 1) * hd], preferred_element_type=F32)
            o = jnp.where(l > 0.0, o / l, 0.0)
            for gq in range(G):
                o_ref[:, (n * G + gq) * hd:(n * G + gq + 1) * hd] = o[gq * TQ:(gq + 1) * TQ, :].astype(BF16)

    body(width)


def _idx_page_copy(cache_ref, buf_ref, sem, page, slot, p):
    return pltpu.make_async_copy(cache_ref.at[page], buf_ref.at[slot, p], sem.at[slot])


def _dsa_sample_score_kernel(pt_ref, cache_ref, qi_ref, wi_ref, kin_ref, sc_ref, scn_ref,
                             buf_ref, sem, *, n_pages):
    d = pl.program_id(0)
    slot = d % 2

    def fetch(dd, sl):
        def start(p, carry):
            _idx_page_copy(cache_ref, buf_ref, sem, pt_ref[dd, p], sl, p).start()
            return carry
        lax.fori_loop(0, n_pages, start, 0, unroll=8)

    @pl.when(d == 0)
    def _():
        fetch(0, 0)

    @pl.when(d + 1 < pl.num_programs(0))
    def _():
        fetch(d + 1, 1 - slot)

    def wait(p, carry):
        _idx_page_copy(cache_ref, buf_ref, sem, 0, slot, p).wait()
        return carry

    lax.fori_loop(0, n_pages, wait, 0, unroll=8)
    nt = (((1,), (1,)), ((), ()))
    qi = qi_ref[...].astype(BF16)
    w = wi_ref[...] * (IDX_HEADS ** -0.5 * IDX_DIM ** -0.5)
    kp = buf_ref[slot].astype(BF16)
    qib = jnp.broadcast_to(qi[None], (n_pages, IDX_HEADS, IDX_DIM))
    s = lax.dot_general(qib, kp, (((2,), (1,)), ((0,), (0,))), preferred_element_type=F32)
    sc_ref[...] = jnp.sum(w[None] * jnp.maximum(s, 0.0), axis=1)
    kn = jnp.broadcast_to(kin_ref[...], (8, IDX_DIM)).astype(BF16)
    sn = lax.dot_general(qi, kn, nt, preferred_element_type=F32)[:, 0:1]
    scn_ref[...] = jnp.broadcast_to(jnp.sum(w * jnp.maximum(sn, 0.0), axis=0, keepdims=True), (1, LANE))


def _dsa_sample_select_kernel(sc_ref, scn_ref, hl_ref, sel_ref, pos_ref, key_ref, *, k_sel, chunk):
    DB, NP = sc_ref.shape
    nblk = NP // LANE
    lane1 = lax.broadcasted_iota(I32, (1, LANE), 1)
    key_ref[:, 0:NP] = sc_ref[...]
    key_ref[:, NP:] = jnp.where(lane1 == 0, scn_ref[...], -jnp.inf)
    thr = _kth_largest(lambda: key_ref[...], DB, k_sel)
    keys = key_ref[...]
    gt = keys > thr
    eq = jnp.logical_and(keys == thr, keys > -jnp.inf)
    n_gt = jnp.sum(gt.astype(I32), axis=1, keepdims=True).astype(F32)

    iu = lax.broadcasted_iota(I32, (LANE, LANE), 0)
    ju = lax.broadcasted_iota(I32, (LANE, LANE), 1)
    upper = jnp.where(iu < ju, 1.0, 0.0).astype(BF16)
    ones = jnp.ones((LANE, LANE), BF16)

    def excl_prefix(mask):
        mb = jnp.where(mask, 1.0, 0.0).astype(BF16)
        stacked = jnp.concatenate([mb[:, c * LANE:(c + 1) * LANE] for c in range(nblk + 1)], axis=0)
        within = jnp.dot(stacked, upper, preferred_element_type=F32)
        tot = jnp.dot(stacked, ones, preferred_element_type=F32)[:, 0:1]
        outs = []
        run = jnp.zeros((DB, 1), F32)
        for c in range(nblk + 1):
            outs.append(within[c * DB:(c + 1) * DB, :] + run)
            run = run + tot[c * DB:(c + 1) * DB, :]
        return jnp.concatenate(outs, axis=1)

    pos_gt = excl_prefix(gt)
    pos_eq = excl_prefix(eq) + n_gt
    keep_eq = jnp.logical_and(eq, pos_eq < k_sel)
    pos = jnp.where(gt, pos_gt, jnp.where(keep_eq, pos_eq, -1.0))
    for d in range(DB):
        pos_ref[d] = pos[d:d + 1, :]

    jrow = lax.broadcasted_iota(I32, (k_sel, 1), 0).astype(F32)
    width = NP + LANE
    def compact(d, carry):
        acc = jnp.zeros((k_sel, LANE), F32)
        for c0 in range(0, width, chunk):
            c1 = min(c0 + chunk, width)
            e = jnp.where(pos_ref[d, :, c0:c1] == jrow, 1.0, 0.0).astype(BF16)
            acc = acc + jnp.dot(e, hl_ref[c0:c1, :], preferred_element_type=F32)
        sel_ref[d] = acc.astype(I32)
        return carry

    lax.fori_loop(0, DB, compact, 0)


def _kv_copies(ck_ref, cv_ref, kbuf_ref, vbuf_ref, sem, page, slot, buf, j):
    ck = pltpu.make_async_copy(ck_ref.at[page, slot], kbuf_ref.at[buf, j], sem.at[0, buf])
    cv = pltpu.make_async_copy(cv_ref.at[page, slot], vbuf_ref.at[buf, j], sem.at[1, buf])
    return ck, cv


def _dsa_sample_attend_kernel(sel_s_ref, pt_ref, alias_ref, ck_ref, cv_ref, selv_ref, q_ref,
                              kn_ref, vn_ref, o_ref, kbuf_ref, vbuf_ref, acc_ref, sem,
                              *, k_sel, n_pages, hd):
    del alias_ref
    d = pl.program_id(0)
    buf = d % 2

    def fetch(dd, bb):
        def start(j, carry):
            page = jnp.minimum(sel_s_ref[dd, 2 * j], n_pages - 1)
            ck, cv = _kv_copies(ck_ref, cv_ref, kbuf_ref, vbuf_ref, sem, pt_ref[dd, page],
                                sel_s_ref[dd, 2 * j + 1], bb, j)
            ck.start()
            cv.start(priority=1)
            return carry
        lax.fori_loop(0, k_sel, start, 0, unroll=8)

    @pl.when(d == 0)
    def _():
        fetch(0, 0)

    @pl.when(d + 1 < pl.num_programs(0))
    def _():
        fetch(d + 1, 1 - buf)

    def wait(j, carry):
        ck, cv = _kv_copies(ck_ref, cv_ref, kbuf_ref, vbuf_ref, sem, 0, 0, buf, j)
        ck.wait()
        cv.wait()
        return carry

    lax.fori_loop(0, k_sel, wait, 0, unroll=8)
    is_new = selv_ref[:, 0:1] >= n_pages
    G = ATT_HEADS // ATT_KV_HEADS
    nt = (((1,), (1,)), ((), ()))
    q = q_ref[...] * (hd ** -0.5)
    outs = []
    for n in range(ATT_KV_HEADS):
        hs = slice(n * hd, (n + 1) * hd)
        kk = jnp.where(is_new, kn_ref[:, hs], kbuf_ref[buf, :, n, :]).astype(BF16)
        vv = jnp.where(is_new, vn_ref[:, hs], vbuf_ref[buf, :, n, :]).astype(BF16)
        qn = jnp.concatenate([q[n * G:(n + 1) * G, :], jnp.zeros((8 - G, hd), F32)], axis=0).astype(BF16)
        s = lax.dot_general(qn, kk, nt, preferred_element_type=F32)
        m = jnp.max(s, axis=1, keepdims=True)
        p = jnp.exp(s - m)
        l = jnp.sum(p, axis=1, keepdims=True)
        o = jnp.dot(p.astype(BF16), vv, preferred_element_type=F32) / l
        outs.append(o[0:G, :])
    acc_ref[pl.ds(d, 1)] = jnp.concatenate(outs, axis=0)[None]

    @pl.when(d == pl.num_programs(0) - 1)
    def _():
        for h in range(ATT_HEADS):
            o_ref[:, h * hd:(h + 1) * hd] = acc_ref[:, h, :].astype(BF16)


def _mix_kernel(a_ref, b_ref, wa_ref, wb_ref, ga_ref, gb_ref, o_ref):
    ba = jnp.dot(a_ref[...], wa_ref[...], preferred_element_type=F32)
    bb = jnp.dot(b_ref[...], wb_ref[...], preferred_element_type=F32)
    o_ref[...] = (_sigmoid(ga_ref[...]) * ba + _sigmoid(gb_ref[...]) * bb).astype(BF16)


def _outproj_kernel(x_ref, m_ref, w_ref, g_ref, x1_ref, h_ref):
    x1 = x_ref[...] + jnp.dot(m_ref[...], w_ref[...], preferred_element_type=F32)
    x1_ref[...] = x1
    ms = jnp.mean(x1 * x1, axis=-1, keepdims=True)
    h_ref[...] = (x1 * lax.rsqrt(ms + EPS) * g_ref[...]).astype(BF16)


def _ffn_kernel(h_ref, halo_ref, x1_ref, wa_ref, wb_ref, wd_ref, cw_ref, cb_ref, st0_ref, st1_ref,
                gf_ref, y_ref, head_ref, tail_ref, acc_ref, s1_ref, s2_ref, *, n_dec):
    i = pl.program_id(0)
    j = pl.program_id(1)
    h = h_ref[...]
    a = jnp.dot(h, wa_ref[...], preferred_element_type=F32)
    b = jnp.dot(h, wb_ref[...], preferred_element_type=F32)
    ah = jnp.dot(halo_ref[...], wa_ref[...], preferred_element_type=F32)
    ah = jnp.where(i > 0, ah, 0.0)
    row = lax.broadcasted_iota(I32, a.shape, 0)
    hl = ah.shape[0]
    s1_ref[...] = jnp.where(row == 0, ah[hl - 1:hl, :], pltpu.roll(a, 1, 0))
    s2_ref[...] = jnp.where(row == 0, ah[hl - 2:hl - 1, :],
                            jnp.where(row == 1, ah[hl - 1:hl, :], pltpu.roll(a, 2, 0)))

    @pl.when(i == 0)
    def _():
        s1_ref[0:n_dec, :] = st1_ref[...]
        s2_ref[0:n_dec, :] = st0_ref[...]

    conv = cb_ref[...] + cw_ref[0:1, :] * s2_ref[...] + cw_ref[1:2, :] * s1_ref[...] + cw_ref[2:3, :] * a
    gate = (conv * _sigmoid(conv) * b).astype(BF16)
    part = jnp.dot(gate, wd_ref[...], preferred_element_type=F32)

    @pl.when(j == 0)
    def _():
        acc_ref[...] = part

    @pl.when(j > 0)
    def _():
        acc_ref[...] = acc_ref[...] + part

    head_ref[...] = a[0:head_ref.shape[0], :]
    tail_ref[...] = a[a.shape[0] - 8:, :]

    @pl.when(j == pl.num_programs(1) - 1)
    def _():
        x2 = x1_ref[...] + acc_ref[...]
        ms = jnp.mean(x2 * x2, axis=-1, keepdims=True)
        y_ref[...] = x2 * lax.rsqrt(ms + EPS) * gf_ref[...]


def _rope_tables(pos, hd, reps):
    rot = hd // 4
    half = rot // 2
    inv = jnp.exp(-math.log(ROPE_THETA) * jnp.arange(half, dtype=F32) * 2.0 / rot)
    ang = pos.astype(F32)[:, None] * inv[None, :]
    cos, sin = jnp.cos(ang), jnp.sin(ang)
    n = pos.shape[0]
    one = jnp.ones((n, hd - rot), F32)
    zero_r = jnp.zeros((n, hd - rot), F32)
    zero_h = jnp.zeros((n, half), F32)
    c = jnp.concatenate([cos, cos, one], axis=1)
    sa = jnp.concatenate([-sin, zero_h, zero_r], axis=1)
    sb = jnp.concatenate([zero_h, sin, zero_r], axis=1)
    return [jnp.tile(t, (1, reps)) for t in (c, sa, sb)]


def kernel(x_prompt, x_sample, cache_k, cache_v, cache_idx_k, state_gla, state_conv, page_table, meta_tokens, norm_mix_g, w_in, w_alpha_up, b_alpha, gla_norm_g, w_branch_a, w_branch_b, w_out, norm_ffn_g, w_up, conv_w, conv_b, w_down, norm_final_g):
    B, SEQ, D = x_prompt.shape
    DB = x_sample.shape[0]
    assert x_sample.shape[1] == 1 and w_in.shape[0] == 1
    n_pool = cache_k.shape[1]
    n_pages = page_table.shape[1]
    past = n_pages * PAGE_SIZE
    assert n_pages == LANE and PAGE_SIZE == LANE
    dff = w_down.shape[1]
    H = GLA_HEADS
    dk = D // 2 // H
    dv = D // H
    hd = D // ATT_HEADS
    kvw = ATT_KV_HEADS * hd
    T = SEQ + N_META
    TP = SEQ + FRONT
    R = B * TP
    assert DB <= GLA_CHUNK and DB % 16 == 0 and SEQ % LANE == 0 and dff % COL_TILE == 0
    k_sel_p = min(TOPK_MAX, T // 4)
    k_sel_s = min(TOPK_MAX, (past + 1) // 4)

    sizes = (H * dk, H * dk, H * dv, H * dv, GLA_GATE_RANK, ATT_HEADS * hd, kvw, kvw,
             IDX_HEADS * IDX_DIM, IDX_DIM, IDX_HEADS, D, D)
    offs = [0]
    for s_ in sizes:
        offs.append(offs[-1] + s_)
    w0 = w_in[0]
    wg_a = w0[:, offs[0]:offs[4]].astype(BF16)
    wg_b = w0[:, offs[5]:offs[9]].astype(BF16)
    wg_c = w0[:, offs[11]:offs[13]].astype(BF16)
    small_pad = COL_TILE - (IDX_DIM + GLA_GATE_RANK + IDX_HEADS)
    wg_s = jnp.concatenate([w0[:, offs[9]:offs[10]], w0[:, offs[4]:offs[5]], w0[:, offs[10]:offs[11]],
                            jnp.zeros((D, small_pad), F32)], axis=1).astype(BF16)
    n_a, n_b, n_c = (wg_a.shape[1] // COL_TILE, wg_b.shape[1] // COL_TILE, wg_c.shape[1] // COL_TILE)
    assert all(w_.shape[1] % COL_TILE == 0 for w_ in (wg_a, wg_b, wg_c)) and hd == LANE
    QA, KA, VA, RA = 0, sizes[0], sizes[0] + sizes[1], sizes[0] + sizes[1] + sizes[2]
    QB = n_a * COL_TILE
    KB, VB, QI = QB + sizes[5], QB + sizes[5] + sizes[6], QB + sizes[5] + sizes[6] + sizes[7]
    GA = (n_a + n_b) * COL_TILE
    GB = GA + D
    SM = (n_a + n_b + n_c) * COL_TILE
    NW = SM + COL_TILE
    assert all(v_ % COL_TILE == 0 for v_ in (QA, KA, VA, RA, QB, KB, VB, QI, GA, GB))
    assert VB - KB == COL_TILE
    n_ct = NW // COL_TILE
    wau_pad = jnp.zeros((LANE, H * dk), F32).at[IDX_DIM:IDX_DIM + GLA_GATE_RANK].set(w_alpha_up[0]).astype(BF16)
    wa_bf = w_branch_a[0].astype(BF16)
    wb_bf = w_branch_b[0].astype(BF16)
    wo_bf = w_out[0].astype(BF16)
    wup_bf = w_up[0].astype(BF16)
    wdn_bf = w_down[0].astype(BF16)

    front = jnp.zeros((B, ROW0, D), F32).at[0, :DB].set(x_sample[:, 0])
    meta = jnp.broadcast_to(meta_tokens[None].astype(F32), (B, N_META, D))
    x_all = jnp.concatenate([front, meta, x_prompt], axis=1).reshape(R, D)
    rpos = jnp.maximum(jnp.arange(TP, dtype=jnp.int32) - ROW0, 0)
    pos = jnp.concatenate([rpos.at[:DB].set(past), rpos])
    tabs = jnp.concatenate(_rope_tables(pos, hd, 1) + _rope_tables(pos, IDX_DIM, LANE // IDX_DIM), axis=1)

    tm1 = TP // 2
    bpb = TP // tm1
    P = pl.pallas_call(
        functools.partial(_inproj_kernel, n_a=n_a, n_b=n_b, n_c=n_c,
                          b128=(0, (VB - QB) // COL_TILE),
                          b64=((QI - QB) // COL_TILE, (QI - QB + IDX_HEADS * IDX_DIM) // COL_TILE)),
        grid=(R // tm1, n_ct),
        in_specs=[pl.BlockSpec((tm1, D), lambda i, j: (i, 0)),
                  pl.BlockSpec((1, D), lambda i, j: (0, 0)),
                  pl.BlockSpec((D, COL_TILE), lambda i, j: (0, jnp.minimum(j, n_a - 1))),
                  pl.BlockSpec((D, COL_TILE), lambda i, j: (0, jnp.clip(j - n_a, 0, n_b - 1))),
                  pl.BlockSpec((D, COL_TILE), lambda i, j: (0, jnp.clip(j - n_a - n_b, 0, n_c - 1))),
                  pl.BlockSpec((D, COL_TILE), lambda i, j: (0, 0)),
                  pl.BlockSpec((tm1, 6 * LANE), lambda i, j: (jnp.where(i < bpb, i, bpb + i % bpb), 0))],
        out_specs=pl.BlockSpec((tm1, COL_TILE), lambda i, j: (i, j)),
        out_shape=jax.ShapeDtypeStruct((R, NW), F32),
        scratch_shapes=[pltpu.VMEM((tm1, D), BF16)],
        compiler_params=_cparams(("parallel", "arbitrary")),
        name="inproj",
    )(x_all, norm_mix_g, wg_a, wg_b, wg_c, wg_s, tabs)

    C = GLA_CHUNK
    ncb = TP // C
    ba2 = b_alpha.reshape(1, H * dk)
    gain2 = gla_norm_g.reshape(1, H * dv)
    smc = SM // LANE
    a_out, s_fin = pl.pallas_call(
        functools.partial(_gla_prompt_kernel, dk=dk, dv=dv),
        grid=(B, ncb),
        in_specs=[pl.BlockSpec((C, H * dk), lambda b, c: (b * ncb + c, QA // (H * dk))),
                  pl.BlockSpec((C, H * dk), lambda b, c: (b * ncb + c, KA // (H * dk))),
                  pl.BlockSpec((C, H * dv), lambda b, c: (b * ncb + c, VA // (H * dv))),
                  pl.BlockSpec((C, H * dv), lambda b, c: (b * ncb + c, RA // (H * dv))),
                  pl.BlockSpec((C, LANE), lambda b, c: (b * ncb + c, smc)),
                  pl.BlockSpec((LANE, H * dk), lambda b, c: (0, 0)),
                  pl.BlockSpec((1, H * dk), lambda b, c: (0, 0)),
                  pl.BlockSpec((1, H * dv), lambda b, c: (0, 0))],
        out_specs=[pl.BlockSpec((C, H * dv), lambda b, c: (b * ncb + c, 0)),
                   pl.BlockSpec((None, H, dk, dv), lambda b, c: (b, 0, 0, 0))],
        out_shape=[jax.ShapeDtypeStruct((R, H * dv), BF16),
                   jax.ShapeDtypeStruct((B, H, dk, dv), F32)],
        scratch_shapes=[pltpu.VMEM((H, dk, dv), F32)],
        compiler_params=_cparams(("parallel", "arbitrary")),
        name="gla_prompt",
    )(P, P, P, P, P, wau_pad, ba2, gain2)

    p_s = P[:DB]
    P_rows = p_s.reshape(DB, 1, NW)
    a_out, s_new_s = pl.pallas_call(
        functools.partial(_gla_sample_kernel, dk=dk, dv=dv),
        grid=(DB,),
        in_specs=[pl.BlockSpec(memory_space=pl.ANY),
                  pl.BlockSpec((None, 1, H * dk), lambda d: (d, 0, QA // (H * dk))),
                  pl.BlockSpec((None, 1, H * dk), lambda d: (d, 0, KA // (H * dk))),
                  pl.BlockSpec((None, 1, H * dv), lambda d: (d, 0, VA // (H * dv))),
                  pl.BlockSpec((None, 1, H * dv), lambda d: (d, 0, RA // (H * dv))),
                  pl.BlockSpec((None, 1, LANE), lambda d: (d, 0, smc)),
                  pl.BlockSpec((LANE, H * dk), lambda d: (0, 0)),
                  pl.BlockSpec((1, H * dk), lambda d: (0, 0)),
                  pl.BlockSpec((1, H * dv), lambda d: (0, 0)),
                  pl.BlockSpec((None, H, dk, dv), lambda d: (d, 0, 0, 0))],
        out_specs=[pl.BlockSpec((DB, H * dv), lambda d: (0, 0)),
                   pl.BlockSpec((None, H, dk, dv), lambda d: (d, 0, 0, 0))],
        out_shape=[jax.ShapeDtypeStruct((R, H * dv), BF16),
                   jax.ShapeDtypeStruct((DB, H, dk, dv), F32)],
        scratch_shapes=[pltpu.VMEM((DB, H * dv), F32)],
        input_output_aliases={0: 0},
        compiler_params=_cparams(("arbitrary",)),
        name="gla_sample",
    )(a_out, P_rows, P_rows, P_rows, P_rows, P_rows, wau_pad, ba2, gain2, state_gla[0])

    TQ = LANE
    nqb = TP // TQ
    n_cls = -(-nqb // QB_PER_CLASS)
    widths = tuple(min((c + 1) * QB_PER_CLASS * TQ, TP) for c in range(n_cls))
    o_b = jnp.zeros((R, ATT_HEADS * hd), BF16)
    for c, width in enumerate(widths):
        q_lo = c * QB_PER_CLASS
        n_q = min(QB_PER_CLASS, nqb - q_lo)
        row_blk = lambda b, q, q_lo=q_lo: b * nqb + q_lo + q
        specs = [pl.BlockSpec((TQ, IDX_HEADS * IDX_DIM), lambda b, q, r_=row_blk: (r_(b, q), QI // (IDX_HEADS * IDX_DIM))),
                 pl.BlockSpec((TQ, LANE), lambda b, q, r_=row_blk: (r_(b, q), smc)),
                 pl.BlockSpec((TP, LANE), lambda b, q: (b, smc)),
                 pl.BlockSpec((TQ, ATT_HEADS * hd), lambda b, q, r_=row_blk: (r_(b, q), QB // (ATT_HEADS * hd))),
                 pl.BlockSpec((TP, kvw), lambda b, q: (b, KB // kvw)),
                 pl.BlockSpec((TP, kvw), lambda b, q: (b, VB // kvw))]
        o_b = pl.pallas_call(
            functools.partial(_dsa_prompt_kernel, k_sel=k_sel_p, hd=hd, width=width, q_lo=q_lo),
            grid=(B, n_q),
            in_specs=[pl.BlockSpec(memory_space=pl.ANY)] + specs,
            out_specs=pl.BlockSpec((TQ, ATT_HEADS * hd), lambda b, q, r_=row_blk: (r_(b, q), 0)),
            out_shape=jax.ShapeDtypeStruct((R, ATT_HEADS * hd), BF16),
            scratch_shapes=[pltpu.VMEM((width, LANE), BF16), pltpu.VMEM((width, LANE), BF16),
                            pltpu.VMEM((width, kvw), BF16), pltpu.VMEM((width, kvw), BF16),
                            pltpu.VMEM((TQ, width), F32)],
            input_output_aliases={0: 0},
            compiler_params=_cparams(("parallel", "arbitrary")),
            name=f"dsa_prompt_w{width}",
        )(o_b, P, P, P, P, P, P)

    qi_s = p_s[:, QI:QI + IDX_HEADS * IDX_DIM].reshape(DB, IDX_HEADS, IDX_DIM)
    wi_s = p_s[:, SM + WI_OFF:SM + WI_OFF + IDX_HEADS].reshape(DB, IDX_HEADS, 1)
    ki_s = p_s[:, SM:SM + IDX_DIM].reshape(DB, 1, IDX_DIM)
    sc, scn = pl.pallas_call(
        functools.partial(_dsa_sample_score_kernel, n_pages=n_pages),
        grid_spec=pltpu.PrefetchScalarGridSpec(
            num_scalar_prefetch=1,
            grid=(DB,),
            in_specs=[pl.BlockSpec(memory_space=pl.ANY),
                      pl.BlockSpec((None, IDX_HEADS, IDX_DIM), lambda d, pt: (d, 0, 0)),
                      pl.BlockSpec((None, IDX_HEADS, 1), lambda d, pt: (d, 0, 0)),
                      pl.BlockSpec((None, 1, IDX_DIM), lambda d, pt: (d, 0, 0))],
            out_specs=[pl.BlockSpec((None, n_pages, PAGE_SIZE), lambda d, pt: (d, 0, 0)),
                       pl.BlockSpec((None, 1, LANE), lambda d, pt: (d, 0, 0))],
            scratch_shapes=[pltpu.VMEM((2, n_pages, IDX_DIM, PAGE_SIZE), F32),
                            pltpu.SemaphoreType.DMA((2,))]),
        out_shape=[jax.ShapeDtypeStruct((DB, n_pages, PAGE_SIZE), F32),
                   jax.ShapeDtypeStruct((DB, 1, LANE), F32)],
        compiler_params=_cparams(("arbitrary",)),
        name="dsa_sample_score",
    )(page_table, jnp.swapaxes(cache_idx_k[0], 1, 2), qi_s, wi_s, ki_s)

    sidx = lax.broadcasted_iota(jnp.int32, (past + LANE, LANE), 0)
    hl_lane = lax.broadcasted_iota(jnp.int32, (past + LANE, LANE), 1)
    hl = jnp.where(hl_lane == 0, sidx // PAGE_SIZE, jnp.where(hl_lane == 1, sidx % PAGE_SIZE, 0)).astype(BF16)
    sel = pl.pallas_call(
        functools.partial(_dsa_sample_select_kernel, k_sel=k_sel_s, chunk=2048),
        out_shape=jax.ShapeDtypeStruct((DB, k_sel_s, LANE), I32),
        scratch_shapes=[pltpu.VMEM((DB, 1, past + LANE), F32), pltpu.VMEM((DB, past + LANE), F32)],
        compiler_params=pltpu.CompilerParams(vmem_limit_bytes=VMEM_LIMIT),
        name="dsa_sample_select",
    )(sc.reshape(DB, past), scn.reshape(DB, LANE), hl)

    sel_s = sel[:, :, 0:2].reshape(DB, 2 * k_sel_s)
    q_s = p_s[:, QB:QB + ATT_HEADS * hd].reshape(DB, ATT_HEADS, hd)
    kn_s = p_s[:, KB:KB + kvw]
    vn_s = p_s[:, VB:VB + kvw]
    o_b = pl.pallas_call(
        functools.partial(_dsa_sample_attend_kernel, k_sel=k_sel_s, n_pages=n_pages, hd=hd),
        grid_spec=pltpu.PrefetchScalarGridSpec(
            num_scalar_prefetch=2,
            grid=(DB,),
            in_specs=[pl.BlockSpec(memory_space=pl.ANY),
                      pl.BlockSpec(memory_space=pl.ANY),
                      pl.BlockSpec(memory_space=pl.ANY),
                      pl.BlockSpec((None, k_sel_s, LANE), lambda d, s_, pt: (d, 0, 0)),
                      pl.BlockSpec((None, ATT_HEADS, hd), lambda d, s_, pt: (d, 0, 0)),
                      pl.BlockSpec((None, 1, kvw), lambda d, s_, pt: (d, 0, KB // kvw)),
                      pl.BlockSpec((None, 1, kvw), lambda d, s_, pt: (d, 0, VB // kvw))],
            out_specs=pl.BlockSpec((DB, ATT_HEADS * hd), lambda d, s_, pt: (0, 0)),
            scratch_shapes=[pltpu.VMEM((2, k_sel_s, ATT_KV_HEADS, hd), F32),
                            pltpu.VMEM((2, k_sel_s, ATT_KV_HEADS, hd), F32),
                            pltpu.VMEM((DB, ATT_HEADS, hd), F32), pltpu.SemaphoreType.DMA((2, 2))]),
        out_shape=jax.ShapeDtypeStruct((R, ATT_HEADS * hd), BF16),
        input_output_aliases={2: 0},
        compiler_params=_cparams(("arbitrary",)),
        name="dsa_sample_attend",
    )(sel_s, page_table, o_b, cache_k[0], cache_v[0], sel, q_s, P_rows, P_rows)

    mix = pl.pallas_call(
        _mix_kernel,
        grid=(R // tm1, D // COL_TILE),
        in_specs=[pl.BlockSpec((tm1, H * dv), lambda i, j: (i, 0)),
                  pl.BlockSpec((tm1, ATT_HEADS * hd), lambda i, j: (i, 0)),
                  pl.BlockSpec((H * dv, COL_TILE), lambda i, j: (0, j)),
                  pl.BlockSpec((ATT_HEADS * hd, COL_TILE), lambda i, j: (0, j)),
                  pl.BlockSpec((tm1, COL_TILE), lambda i, j: (i, GA // COL_TILE + j)),
                  pl.BlockSpec((tm1, COL_TILE), lambda i, j: (i, GB // COL_TILE + j))],
        out_specs=pl.BlockSpec((tm1, COL_TILE), lambda i, j: (i, j)),
        out_shape=jax.ShapeDtypeStruct((R, D), BF16),
        compiler_params=_cparams(("parallel", "arbitrary")),
        name="mix",
    )(a_out, o_b, wa_bf, wb_bf, P, P)

    tm2 = TP // 4
    x1, h2 = pl.pallas_call(
        _outproj_kernel,
        grid=(R // tm2,),
        in_specs=[pl.BlockSpec((tm2, D), lambda i: (i, 0)),
                  pl.BlockSpec((tm2, D), lambda i: (i, 0)),
                  pl.BlockSpec((D, D), lambda i: (0, 0)),
                  pl.BlockSpec((1, D), lambda i: (0, 0))],
        out_specs=[pl.BlockSpec((tm2, D), lambda i: (i, 0)),
                   pl.BlockSpec((tm2, D), lambda i: (i, 0))],
        out_shape=[jax.ShapeDtypeStruct((R, D), F32), jax.ShapeDtypeStruct((R, D), BF16)],
        compiler_params=_cparams(("parallel",)),
        name="outproj",
    )(x_all, mix, wo_bf, norm_ffn_g)

    nrb = R // tm2
    nft = dff // COL_TILE
    y_all, a_head, a_tail = pl.pallas_call(
        functools.partial(_ffn_kernel, n_dec=DB),
        grid=(nrb, nft),
        in_specs=[pl.BlockSpec((tm2, D), lambda i, j: (i, 0)),
                  pl.BlockSpec((16, D), lambda i, j: (jnp.maximum(i * (tm2 // 16) - 1, 0), 0)),
                  pl.BlockSpec((tm2, D), lambda i, j: (i, 0)),
                  pl.BlockSpec((D, COL_TILE), lambda i, j: (0, j)),
                  pl.BlockSpec((D, COL_TILE), lambda i, j: (0, nft + j)),
                  pl.BlockSpec((COL_TILE, D), lambda i, j: (j, 0)),
                  pl.BlockSpec((CONV_W, COL_TILE), lambda i, j: (0, j)),
                  pl.BlockSpec((1, COL_TILE), lambda i, j: (0, j)),
                  pl.BlockSpec((DB, COL_TILE), lambda i, j: (0, j)),
                  pl.BlockSpec((DB, COL_TILE), lambda i, j: (0, j)),
                  pl.BlockSpec((1, D), lambda i, j: (0, 0))],
        out_specs=[pl.BlockSpec((tm2, D), lambda i, j: (i, 0)),
                   pl.BlockSpec((DB, COL_TILE), lambda i, j: (i, j)),
                   pl.BlockSpec((8, COL_TILE), lambda i, j: (i, j))],
        out_shape=[jax.ShapeDtypeStruct((R, D), F32),
                   jax.ShapeDtypeStruct((nrb * DB, dff), F32),
                   jax.ShapeDtypeStruct((nrb * 8, dff), F32)],
        scratch_shapes=[pltpu.VMEM((tm2, D), F32), pltpu.VMEM((tm2, COL_TILE), F32),
                        pltpu.VMEM((tm2, COL_TILE), F32)],
        compiler_params=_cparams(("parallel", "arbitrary")),
        name="ffn",
    )(h2, h2, x1, wup_bf, wup_bf, wdn_bf, conv_w[0], conv_b, state_conv[0, :, 0], state_conv[0, :, 1],
      norm_final_g.reshape(1, D))

    y3 = y_all.reshape(B, TP, D)
    y_prompt = y3[:, FRONT:]
    y_sample = y_all[:DB].reshape(DB, 1, D)
    P3 = P.reshape(B, TP, NW)
    new_k_p = P3[:, ROW0:, KB:KB + kvw].reshape(1, B, T, ATT_KV_HEADS, hd)
    new_v_p = P3[:, ROW0:, VB:VB + kvw].reshape(1, B, T, ATT_KV_HEADS, hd)
    new_ki_p = P3[:, ROW0:, SM:SM + IDX_DIM].reshape(1, B, T, IDX_DIM)
    new_gla_p = s_fin[None]
    tails = a_tail.reshape(B, nrb // B, 8, dff)[:, -1, 8 - (CONV_W - 1):, :]
    new_conv_p = tails[None]
    new_k_s = kn_s.reshape(1, DB, 1, ATT_KV_HEADS, hd)
    new_v_s = vn_s.reshape(1, DB, 1, ATT_KV_HEADS, hd)
    new_ki_s = ki_s.reshape(1, DB, 1, IDX_DIM)
    new_gla_s = s_new_s[None]
    new_conv_s = jnp.stack([state_conv[0, :, 1], a_head[:DB]], axis=1)[None]
    return (y_prompt, y_sample, new_k_p, new_v_p, new_ki_p, new_gla_p, new_conv_p,
            new_k_s, new_v_s, new_ki_s, new_gla_s, new_conv_s)
```

```python
import functools
import math

import jax
import jax.numpy as jnp
import numpy as np
from jax import lax
from jax.experimental import pallas as pl
from jax.experimental.pallas import tpu as pltpu

F32 = jnp.float32
BF16 = jnp.bfloat16
I32 = jnp.int32

N_META = 16
GLA_HEADS = 4
GLA_GATE_RANK = 16
GLA_GATE_NORM = 16.0
GLA_CHUNK = 64
GLA_SUB = 8
ATT_HEADS = 16
ATT_KV_HEADS = 4
IDX_HEADS = 16
IDX_DIM = 64
TOPK_MAX = 256
ROPE_THETA = 500000.0
CONV_W = 3
EPS = 1e-6
PAGE_SIZE = 128
WI_OFF = IDX_DIM + GLA_GATE_RANK

LANE = 128
FRONT = 128
ROW0 = FRONT - N_META
COL_TILE = 512
QB_PER_CLASS = 2
VMEM_LIMIT = 56 * 1024 * 1024
INT_MIN = -2 ** 31
LOG2_E = 1.4426950408889634


def _cparams(sem):
    return pltpu.CompilerParams(dimension_semantics=sem, vmem_limit_bytes=VMEM_LIMIT)


def _sigmoid(x):
    return 1.0 / (1.0 + jnp.exp(-x))


def _pattern_value(p):
    return pltpu.bitcast(jnp.where(p < 0, p ^ jnp.int32(0x7FFFFFFF), p), F32)


def _kth_largest(load_scores, rows, k_sel):
    def body(it, t):
        step = jnp.left_shift(jnp.int32(1), 30 - 2 * it)
        scores = load_scores()
        adv = jnp.zeros((rows, 1), I32)
        for m in (1, 2, 3):
            cnt = jnp.sum((scores >= _pattern_value(t + m * step)).astype(I32), axis=1, keepdims=True)
            adv = adv + (cnt >= k_sel).astype(I32)
        return t + adv * step

    t = lax.fori_loop(0, 16, body, jnp.full((rows, 1), INT_MIN, I32))
    return jnp.where(t == INT_MIN, -jnp.inf, _pattern_value(t))


def _rope_block(xb, c, sa, sb, half):
    return xb * c + pltpu.roll(xb, LANE - half, 1) * sa + pltpu.roll(xb, half, 1) * sb


def _inproj_kernel(x_ref, g_ref, wa_ref, wb_ref, wc_ref, ws_ref, tab_ref, o_ref, h_ref,
                   *, n_a, n_b, n_c, b128, b64):
    j = pl.program_id(1)

    @pl.when(j == 0)
    def _():
        x = x_ref[...]
        ms = jnp.mean(x * x, axis=-1, keepdims=True)
        h_ref[...] = (x * lax.rsqrt(ms + EPS) * g_ref[...]).astype(BF16)

    def proj(w_ref):
        return jnp.dot(h_ref[...], w_ref[...], preferred_element_type=F32)

    @pl.when(j < n_a)
    def _():
        o_ref[...] = proj(wa_ref)

    @pl.when(jnp.logical_and(j >= n_a, j < n_a + n_b))
    def _():
        acc = proj(wb_ref)
        jb = j - n_a
        is128 = jnp.logical_and(jb >= b128[0], jb < b128[1])
        is64 = jnp.logical_and(jb >= b64[0], jb < b64[1])

        @pl.when(is128)
        def _():
            c, sa, sb = tab_ref[:, 0:128], tab_ref[:, 128:256], tab_ref[:, 256:384]
            for blk in range(COL_TILE // LANE):
                sl = slice(blk * LANE, (blk + 1) * LANE)
                o_ref[:, sl] = _rope_block(acc[:, sl], c, sa, sb, 16)

        @pl.when(is64)
        def _():
            c, sa, sb = tab_ref[:, 384:512], tab_ref[:, 512:640], tab_ref[:, 640:768]
            for blk in range(COL_TILE // LANE):
                sl = slice(blk * LANE, (blk + 1) * LANE)
                o_ref[:, sl] = _rope_block(acc[:, sl], c, sa, sb, 8)

        @pl.when(jnp.logical_not(is128 | is64))
        def _():
            o_ref[...] = acc

    @pl.when(jnp.logical_and(j >= n_a + n_b, j < n_a + n_b + n_c))
    def _():
        o_ref[...] = proj(wc_ref)

    @pl.when(j == n_a + n_b + n_c)
    def _():
        acc = proj(ws_ref)
        lane = lax.broadcasted_iota(I32, (1, LANE), 1)
        first = lane < IDX_DIM
        c = jnp.where(first, tab_ref[:, 384:512], 1.0)
        sa = jnp.where(first, tab_ref[:, 512:640], 0.0)
        sb = jnp.where(first, tab_ref[:, 640:768], 0.0)
        o_ref[:, 0:LANE] = _rope_block(acc[:, 0:LANE], c, sa, sb, 8)
        o_ref[:, LANE:] = acc[:, LANE:]


def _log_alpha(sm, wau_ref, ba_ref):
    x = jnp.dot(sm.astype(BF16), wau_ref[...], preferred_element_type=F32) + ba_ref[...]
    return (jnp.minimum(x, 0.0) - jnp.log(1.0 + jnp.exp(-jnp.abs(x)))) * (1.0 / GLA_GATE_NORM)


def _row_to_col(row, n):
    eye = lax.broadcasted_iota(I32, (n, n), 0) == lax.broadcasted_iota(I32, (n, n), 1)
    return jnp.sum(jnp.where(eye, row, 0.0), axis=1, keepdims=True)


def _readout(o, r, gain):
    ms = jnp.mean(o * o, axis=-1, keepdims=True)
    return o * lax.rsqrt(ms + EPS) * gain * (r * _sigmoid(r))


def _gla_chunk(q, k, v, g, S):
    C, dk = q.shape
    vb = v.astype(BF16)
    row = lax.broadcasted_iota(I32, g.shape, 0)
    b = g
    sh = 1
    while sh < C:
        b = b + jnp.where(row >= sh, pltpu.roll(b, sh, 0), 0.0)
        sh *= 2
    bl = b[C - 1:C, :]
    o = jnp.dot((q * jnp.exp(b)).astype(BF16), S.astype(BF16), preferred_element_type=F32)

    tcol = lax.broadcasted_iota(I32, (GLA_SUB, C), 1)
    trow = lax.broadcasted_iota(I32, (GLA_SUB, C), 0)
    a_rows = []
    kf = None
    prev_ref = None
    for blk in range(C // GLA_SUB):
        r0 = blk * GLA_SUB
        ref = b[r0 - 1:r0, :] if blk > 0 else jnp.zeros_like(bl)
        b_i = b[r0:r0 + GLA_SUB, :]
        q_i = q[r0:r0 + GLA_SUB, :]
        if blk == 0:
            a_off = jnp.zeros((GLA_SUB, C), F32)
        else:
            fresh = k[r0 - GLA_SUB:r0, :] * jnp.exp(ref - b[r0 - GLA_SUB:r0, :])
            kf = fresh if blk == 1 else jnp.concatenate([kf * jnp.exp(ref - prev_ref), fresh], axis=0)
            kf_full = jnp.concatenate([kf, jnp.zeros((C - r0, dk), F32)], axis=0)
            qe = q_i * jnp.exp(b_i - ref)
            a_off = lax.dot_general(qe.astype(BF16), kf_full.astype(BF16), (((1,), (1,)), ((), ())),
                                    preferred_element_type=F32)
        prev_ref = ref
        diag = jnp.zeros((GLA_SUB, C), F32)
        for sl in range(GLA_SUB):
            s = r0 + sl
            w = q_i * k[s:s + 1, :] * jnp.exp(jnp.minimum(b_i - b[s:s + 1, :], 0.0))
            diag = jnp.where(tcol == s, jnp.sum(w, axis=1, keepdims=True), diag)
        a_rows.append(jnp.where(tcol < r0, a_off, jnp.where(tcol <= trow + r0, diag, 0.0)))
    a = jnp.concatenate(a_rows, axis=0)
    o = o + jnp.dot(a.astype(BF16), vb, preferred_element_type=F32)

    kd = k * jnp.exp(bl - b)
    upd = lax.dot_general(kd.astype(BF16), vb, (((0,), (0,)), ((), ())),
                          preferred_element_type=F32)
    return o, S * _row_to_col(jnp.exp(bl), dk) + upd


def _gla_prompt_kernel(q_ref, k_ref, v_ref, r_ref, sm_ref, wau_ref, ba_ref, gain_ref,
                       o_ref, sfin_ref, s_ref, *, dk, dv):
    c = pl.program_id(1)
    H = s_ref.shape[0]

    @pl.when(c == 0)
    def _():
        s_ref[...] = jnp.zeros_like(s_ref)
        o_ref[...] = jnp.zeros_like(o_ref)

    @pl.when(c > 0)
    def _():
        g_all = _log_alpha(sm_ref[...], wau_ref, ba_ref)
        for h in range(H):
            ks, vs = slice(h * dk, (h + 1) * dk), slice(h * dv, (h + 1) * dv)
            o, s_new = _gla_chunk(q_ref[:, ks] * (dk ** -0.5), k_ref[:, ks], v_ref[:, vs],
                                  g_all[:, ks], s_ref[h])
            s_ref[h] = s_new
            o_ref[:, vs] = _readout(o, r_ref[:, vs], gain_ref[:, vs]).astype(BF16)

    @pl.when(c == pl.num_programs(1) - 1)
    def _():
        sfin_ref[...] = s_ref[...]


def _gla_sample_kernel(alias_ref, q_ref, k_ref, v_ref, r_ref, sm_ref, wau_ref, ba_ref, gain_ref,
                       st_ref, o_ref, snew_ref, acc_ref, *, dk, dv):
    del alias_ref
    d = pl.program_id(0)
    H = st_ref.shape[0]
    g_all = _log_alpha(jnp.broadcast_to(sm_ref[...], (8, LANE)), wau_ref, ba_ref)[0:1, :]
    ons = []
    for h in range(H):
        ks, vs = slice(h * dk, (h + 1) * dk), slice(h * dv, (h + 1) * dv)
        q = q_ref[:, ks] * (dk ** -0.5)
        k = k_ref[:, ks]
        v = v_ref[:, vs]
        eg = jnp.exp(g_all[:, ks])
        S = st_ref[h]
        qe = jnp.broadcast_to(q * eg, (8, dk))
        o = jnp.dot(qe.astype(BF16), S.astype(BF16), preferred_element_type=F32)[0:1, :]
        o = o + jnp.sum(q * k, axis=1, keepdims=True) * v
        snew_ref[h] = S * _row_to_col(eg, dk) + _row_to_col(k, dk) * v
        ons.append(_readout(o, r_ref[:, vs], gain_ref[:, vs]))
    on = jnp.concatenate(ons, axis=1)
    row = lax.broadcasted_iota(I32, acc_ref.shape, 0)

    @pl.when(d == 0)
    def _():
        acc_ref[...] = jnp.zeros_like(acc_ref)

    acc_ref[...] = jnp.where(row == d, on, acc_ref[...])

    @pl.when(d == pl.num_programs(0) - 1)
    def _():
        o_ref[...] = acc_ref[...].astype(BF16)


def _keep_lowest_ties(key_ref, bias_ref, thr, TK, k_sel):
    sc = key_ref[:, 0:TK]
    gt = sc > thr
    eq = jnp.logical_and(sc == thr, sc > -jnp.inf)
    room = k_sel - jnp.sum(jnp.where(gt, 1.0, 0.0), axis=1, keepdims=True)
    upper = jnp.where(lax.broadcasted_iota(I32, (LANE, LANE), 0) < lax.broadcasted_iota(I32, (LANE, LANE), 1),
                      1.0, 0.0).astype(BF16)
    run = jnp.zeros_like(room)
    for c in range(TK // LANE):
        blk = slice(c * LANE, (c + 1) * LANE)
        e = jnp.where(eq[:, blk], 1.0, 0.0)
        before = jnp.dot(e.astype(BF16), upper, preferred_element_type=F32) + run
        keep = jnp.logical_or(gt[:, blk], jnp.logical_and(eq[:, blk], before < room))
        bias_ref[:, blk] = jnp.where(keep, 0.0, -jnp.inf)
        run = run + jnp.sum(e, axis=1, keepdims=True)


def _dsa_prompt_kernel(alias_ref, qi_ref, smq_ref, smk_ref, qb_ref, kb_ref, vb_ref, o_ref,
                       ka_ref, kbb_ref, kbf_ref, vbf_ref, key_ref, bias_ref, *, k_sel, hd, width, q_lo):
    del alias_ref
    qb = q_lo + pl.program_id(1)
    TQ = qi_ref.shape[0]
    nt = (((1,), (1,)), ((), ()))
    G = ATT_HEADS // ATT_KV_HEADS

    @pl.when(pl.program_id(1) == 0)
    def _():
        smk = smk_ref[0:width, :]
        lane = lax.broadcasted_iota(I32, (1, LANE), 1)
        ka_ref[...] = jnp.where(lane < IDX_DIM, smk, 0.0).astype(BF16)
        kbb_ref[...] = jnp.where(lane >= IDX_DIM, pltpu.roll(smk, IDX_DIM, 1), 0.0).astype(BF16)
        kbf_ref[...] = kb_ref[0:width, :].astype(BF16)
        vbf_ref[...] = vb_ref[0:width, :].astype(BF16)

    def body(TK):
        score = jnp.zeros((TQ, TK), F32)
        for p in range(IDX_HEADS // 2):
            qp = qi_ref[:, p * LANE:(p + 1) * LANE].astype(BF16)
            for half, kref in ((0, ka_ref), (1, kbb_ref)):
                h = 2 * p + half
                s = lax.dot_general(qp, kref[0:TK, :], nt, preferred_element_type=F32)
                w = smq_ref[:, WI_OFF + h:WI_OFF + h + 1] * (IDX_HEADS ** -0.5 * IDX_DIM ** -0.5)
                score = score + w * jnp.maximum(s, 0.0)

        qrow = qb * TQ + lax.broadcasted_iota(I32, (TQ, 1), 0)
        kcol = lax.broadcasted_iota(I32, (1, TK), 1)
        adm = jnp.logical_and(kcol <= qrow, kcol >= ROW0)
        key_ref[:, 0:TK] = jnp.where(adm, score, -jnp.inf)
        thr = _kth_largest(lambda: key_ref[:, 0:TK], TQ, k_sel)
        sc = key_ref[:, 0:TK]
        ge = jnp.logical_and(sc >= thr, sc > -jnp.inf)
        bias_ref[:, 0:TK] = jnp.where(ge, 0.0, -jnp.inf)
        n_ge = jnp.sum(jnp.where(ge, 1.0, 0.0), axis=1, keepdims=True)

        @pl.when(jnp.max(n_ge) > k_sel)
        def _():
            _keep_lowest_ties(key_ref, bias_ref, thr, TK, k_sel)

        for n in range(ATT_KV_HEADS):
            q4 = jnp.concatenate(
                [(qb_ref[:, (n * G + gq) * hd:(n * G + gq + 1) * hd] * (hd ** -0.5 * LOG2_E)).astype(BF16)
                 for gq in range(G)], axis=0)
            s = lax.dot_general(q4, kbf_ref[0:TK, n * hd:(n + 1) * hd], nt, preferred_element_type=F32)
            s = (s.reshape(G, TQ, TK) + bias_ref[:, 0:TK][None]).reshape(G * TQ, TK)
            m = jnp.max(s, axis=1, keepdims=True)
            m = jnp.where(m == -jnp.inf, 0.0, m)
            p = jnp.exp2(s - m)
            l = jnp.sum(p, axis=1, keepdims=True)
            o = jnp.dot(p.astype(BF16), vbf_ref[0:TK, n * hd:(n + 1) * hd], preferred_element_type=F32)
            o = jnp.where(l > 0.0, o / l, 0.0)
            for gq in range(G):
                o_ref[:, (n * G + gq) * hd:(n * G + gq + 1) * hd] = o[gq * TQ:(gq + 1) * TQ, :].astype(BF16)

    body(width)


def _idx_page_copy(cache_ref, buf_ref, sem, page, slot, p):
    return pltpu.make_async_copy(cache_ref.at[page], buf_ref.at[slot, p], sem.at[slot])


def _dsa_sample_score_kernel(pt_ref, cache_ref, qi_ref, wi_ref, kin_ref, sc_ref, scn_ref,
                             buf_ref, sem, *, n_pages):
    d = pl.program_id(0)
    slot = d % 2

    def fetch(dd, sl):
        def start(p, carry):
            _idx_page_copy(cache_ref, buf_ref, sem, pt_ref[dd, p], sl, p).start()
            return carry
        lax.fori_loop(0, n_pages, start, 0, unroll=8)

    @pl.when(d == 0)
    def _():
        fetch(0, 0)

    @pl.when(d + 1 < pl.num_programs(0))
    def _():
        fetch(d + 1, 1 - slot)

    def wait(p, carry):
        _idx_page_copy(cache_ref, buf_ref, sem, 0, slot, p).wait()
        return carry

    lax.fori_loop(0, n_pages, wait, 0, unroll=8)
    nt = (((1,), (1,)), ((), ()))
    qi = qi_ref[...].astype(BF16)
    w = wi_ref[...] * (IDX_HEADS ** -0.5 * IDX_DIM ** -0.5)
    kp = buf_ref[slot].astype(BF16)
    qib = jnp.broadcast_to(qi[None], (n_pages, IDX_HEADS, IDX_DIM))
    s = lax.dot_general(qib, kp, (((2,), (1,)), ((0,), (0,))), preferred_element_type=F32)
    sc_ref[...] = jnp.sum(w[None] * jnp.maximum(s, 0.0), axis=1)
    kn = jnp.broadcast_to(kin_ref[...], (8, IDX_DIM)).astype(BF16)
    sn = lax.dot_general(qi, kn, nt, preferred_element_type=F32)[:, 0:1]
    scn_ref[...] = jnp.broadcast_to(jnp.sum(w * jnp.maximum(sn, 0.0), axis=0, keepdims=True), (1, LANE))


def _dsa_sample_select_kernel(sc_ref, scn_ref, hl_ref, sel_ref, pos_ref, key_ref, *, k_sel, chunk):
    DB, NP = sc_ref.shape
    nblk = NP // LANE
    lane1 = lax.broadcasted_iota(I32, (1, LANE), 1)
    key_ref[:, 0:NP] = sc_ref[...]
    key_ref[:, NP:] = jnp.where(lane1 == 0, scn_ref[...], -jnp.inf)
    thr = _kth_largest(lambda: key_ref[...], DB, k_sel)
    keys = key_ref[...]
    gt = keys > thr
    eq = jnp.logical_and(keys == thr, keys > -jnp.inf)
    n_gt = jnp.sum(gt.astype(I32), axis=1, keepdims=True).astype(F32)

    iu = lax.broadcasted_iota(I32, (LANE, LANE), 0)
    ju = lax.broadcasted_iota(I32, (LANE, LANE), 1)
    upper = jnp.where(iu < ju, 1.0, 0.0).astype(BF16)
    ones = jnp.ones((LANE, LANE), BF16)

    def excl_prefix(mask):
        mb = jnp.where(mask, 1.0, 0.0).astype(BF16)
        stacked = jnp.concatenate([mb[:, c * LANE:(c + 1) * LANE] for c in range(nblk + 1)], axis=0)
        within = jnp.dot(stacked, upper, preferred_element_type=F32)
        tot = jnp.dot(stacked, ones, preferred_element_type=F32)[:, 0:1]
        outs = []
        run = jnp.zeros((DB, 1), F32)
        for c in range(nblk + 1):
            outs.append(within[c * DB:(c + 1) * DB, :] + run)
            run = run + tot[c * DB:(c + 1) * DB, :]
        return jnp.concatenate(outs, axis=1)

    pos_gt = excl_prefix(gt)
    pos_eq = excl_prefix(eq) + n_gt
    keep_eq = jnp.logical_and(eq, pos_eq < k_sel)
    pos = jnp.where(gt, pos_gt, jnp.where(keep_eq, pos_eq, -1.0))
    for d in range(DB):
        pos_ref[d] = pos[d:d + 1, :]

    jrow = lax.broadcasted_iota(I32, (k_sel, 1), 0).astype(F32)
    width = NP + LANE
    def compact(d, carry):
        acc = jnp.zeros((k_sel, LANE), F32)
        for c0 in range(0, width, chunk):
            c1 = min(c0 + chunk, width)
            e = jnp.where(pos_ref[d, :, c0:c1] == jrow, 1.0, 0.0).astype(BF16)
            acc = acc + jnp.dot(e, hl_ref[c0:c1, :], preferred_element_type=F32)
        sel_ref[d] = acc.astype(I32)
        return carry

    lax.fori_loop(0, DB, compact, 0)


def _kv_copies(ck_ref, cv_ref, kbuf_ref, vbuf_ref, sem, page, slot, buf, j):
    ck = pltpu.make_async_copy(ck_ref.at[page, slot], kbuf_ref.at[buf, j], sem.at[0, buf])
    cv = pltpu.make_async_copy(cv_ref.at[page, slot], vbuf_ref.at[buf, j], sem.at[1, buf])
    return ck, cv


def _dsa_sample_attend_kernel(sel_s_ref, pt_ref, alias_ref, ck_ref, cv_ref, selv_ref, q_ref,
                              kn_ref, vn_ref, o_ref, kbuf_ref, vbuf_ref, acc_ref, sem,
                              *, k_sel, n_pages, hd):
    del alias_ref
    d = pl.program_id(0)
    buf = d % 2

    def fetch(dd, bb):
        def start(j, carry):
            page = jnp.minimum(sel_s_ref[dd, 2 * j], n_pages - 1)
            ck, cv = _kv_copies(ck_ref, cv_ref, kbuf_ref, vbuf_ref, sem, pt_ref[dd, page],
                                sel_s_ref[dd, 2 * j + 1], bb, j)
            ck.start()
            cv.start(priority=1)
            return carry
        lax.fori_loop(0, k_sel, start, 0, unroll=8)

    @pl.when(d == 0)
    def _():
        fetch(0, 0)

    @pl.when(d + 1 < pl.num_programs(0))
    def _():
        fetch(d + 1, 1 - buf)

    def wait(j, carry):
        ck, cv = _kv_copies(ck_ref, cv_ref, kbuf_ref, vbuf_ref, sem, 0, 0, buf, j)
        ck.wait()
        cv.wait()
        return carry

    lax.fori_loop(0, k_sel, wait, 0, unroll=8)
    is_new = selv_ref[:, 0:1] >= n_pages
    G = ATT_HEADS // ATT_KV_HEADS
    nt = (((1,), (1,)), ((), ()))
    q = q_ref[...] * (hd ** -0.5)
    outs = []
    for n in range(ATT_KV_HEADS):
        hs = slice(n * hd, (n + 1) * hd)
        kk = jnp.where(is_new, kn_ref[:, hs], kbuf_ref[buf, :, n, :]).astype(BF16)
        vv = jnp.where(is_new, vn_ref[:, hs], vbuf_ref[buf, :, n, :]).astype(BF16)
        qn = jnp.concatenate([q[n * G:(n + 1) * G, :], jnp.zeros((8 - G, hd), F32)], axis=0).astype(BF16)
        s = lax.dot_general(qn, kk, nt, preferred_element_type=F32)
        m = jnp.max(s, axis=1, keepdims=True)
        p = jnp.exp(s - m)
        l = jnp.sum(p, axis=1, keepdims=True)
        o = jnp.dot(p.astype(BF16), vv, preferred_element_type=F32) / l
        outs.append(o[0:G, :])
    acc_ref[pl.ds(d, 1)] = jnp.concatenate(outs, axis=0)[None]

    @pl.when(d == pl.num_programs(0) - 1)
    def _():
        for h in range(ATT_HEADS):
            o_ref[:, h * hd:(h + 1) * hd] = acc_ref[:, h, :].astype(BF16)


def _mix_kernel(a_ref, b_ref, wa_ref, wb_ref, ga_ref, gb_ref, o_ref):
    ba = jnp.dot(a_ref[...], wa_ref[...], preferred_element_type=F32)
    bb = jnp.dot(b_ref[...], wb_ref[...], preferred_element_type=F32)
    o_ref[...] = (_sigmoid(ga_ref[...]) * ba + _sigmoid(gb_ref[...]) * bb).astype(BF16)


def _outproj_kernel(x_ref, m_ref, w_ref, g_ref, x1_ref, h_ref):
    x1 = x_ref[...] + jnp.dot(m_ref[...], w_ref[...], preferred_element_type=F32)
    x1_ref[...] = x1
    ms = jnp.mean(x1 * x1, axis=-1, keepdims=True)
    h_ref[...] = (x1 * lax.rsqrt(ms + EPS) * g_ref[...]).astype(BF16)


def _ffn_kernel(h_ref, halo_ref, x1_ref, wa_ref, wb_ref, wd_ref, cw_ref, cb_ref, st0_ref, st1_ref,
                gf_ref, y_ref, head_ref, tail_ref, acc_ref, s1_ref, s2_ref, *, n_dec):
    i = pl.program_id(0)
    j = pl.program_id(1)
    h = h_ref[...]
    a = jnp.dot(h, wa_ref[...], preferred_element_type=F32)
    b = jnp.dot(h, wb_ref[...], preferred_element_type=F32)
    ah = jnp.dot(halo_ref[...], wa_ref[...], preferred_element_type=F32)
    ah = jnp.where(i > 0, ah, 0.0)
    row = lax.broadcasted_iota(I32, a.shape, 0)
    hl = ah.shape[0]
    s1_ref[...] = jnp.where(row == 0, ah[hl - 1:hl, :], pltpu.roll(a, 1, 0))
    s2_ref[...] = jnp.where(row == 0, ah[hl - 2:hl - 1, :],
                            jnp.where(row == 1, ah[hl - 1:hl, :], pltpu.roll(a, 2, 0)))

    @pl.when(i == 0)
    def _():
        s1_ref[0:n_dec, :] = st1_ref[...]
        s2_ref[0:n_dec, :] = st0_ref[...]

    conv = cb_ref[...] + cw_ref[0:1, :] * s2_ref[...] + cw_ref[1:2, :] * s1_ref[...] + cw_ref[2:3, :] * a
    gate = (conv * _sigmoid(conv) * b).astype(BF16)
    part = jnp.dot(gate, wd_ref[...], preferred_element_type=F32)

    @pl.when(j == 0)
    def _():
        acc_ref[...] = part

    @pl.when(j > 0)
    def _():
        acc_ref[...] = acc_ref[...] + part

    head_ref[...] = a[0:head_ref.shape[0], :]
    tail_ref[...] = a[a.shape[0] - 8:, :]

    @pl.when(j == pl.num_programs(1) - 1)
    def _():
        x2 = x1_ref[...] + acc_ref[...]
        ms = jnp.mean(x2 * x2, axis=-1, keepdims=True)
        y_ref[...] = x2 * lax.rsqrt(ms + EPS) * gf_ref[...]


def _rope_tables(pos, hd, reps):
    rot = hd // 4
    half = rot // 2
    inv = jnp.exp(-math.log(ROPE_THETA) * jnp.arange(half, dtype=F32) * 2.0 / rot)
    ang = pos.astype(F32)[:, None] * inv[None, :]
    cos, sin = jnp.cos(ang), jnp.sin(ang)
    n = pos.shape[0]
    one = jnp.ones((n, hd - rot), F32)
    zero_r = jnp.zeros((n, hd - rot), F32)
    zero_h = jnp.zeros((n, half), F32)
    c = jnp.concatenate([cos, cos, one], axis=1)
    sa = jnp.concatenate([-sin, zero_h, zero_r], axis=1)
    sb = jnp.concatenate([zero_h, sin, zero_r], axis=1)
    return [jnp.tile(t, (1, reps)) for t in (c, sa, sb)]


def kernel(x_prompt, x_sample, cache_k, cache_v, cache_idx_k, state_gla, state_conv, page_table, meta_tokens, norm_mix_g, w_in, w_alpha_up, b_alpha, gla_norm_g, w_branch_a, w_branch_b, w_out, norm_ffn_g, w_up, conv_w, conv_b, w_down, norm_final_g):
    B, SEQ, D = x_prompt.shape
    DB = x_sample.shape[0]
    assert x_sample.shape[1] == 1 and w_in.shape[0] == 1
    n_pool = cache_k.shape[1]
    n_pages = page_table.shape[1]
    past = n_pages * PAGE_SIZE
    assert n_pages == LANE and PAGE_SIZE == LANE
    dff = w_down.shape[1]
    H = GLA_HEADS
    dk = D // 2 // H
    dv = D // H
    hd = D // ATT_HEADS
    kvw = ATT_KV_HEADS * hd
    T = SEQ + N_META
    TP = SEQ + FRONT
    R = B * TP
    assert DB <= GLA_CHUNK and DB % 16 == 0 and SEQ % LANE == 0 and dff % COL_TILE == 0
    k_sel_p = min(TOPK_MAX, T // 4)
    k_sel_s = min(TOPK_MAX, (past + 1) // 4)

    sizes = (H * dk, H * dk, H * dv, H * dv, GLA_GATE_RANK, ATT_HEADS * hd, kvw, kvw,
             IDX_HEADS * IDX_DIM, IDX_DIM, IDX_HEADS, D, D)
    offs = [0]
    for s_ in sizes:
        offs.append(offs[-1] + s_)
    w0 = w_in[0]
    wg_a = w0[:, offs[0]:offs[4]].astype(BF16)
    wg_b = w0[:, offs[5]:offs[9]].astype(BF16)
    wg_c = w0[:, offs[11]:offs[13]].astype(BF16)
    small_pad = COL_TILE - (IDX_DIM + GLA_GATE_RANK + IDX_HEADS)
    wg_s = jnp.concatenate([w0[:, offs[9]:offs[10]], w0[:, offs[4]:offs[5]], w0[:, offs[10]:offs[11]],
                            jnp.zeros((D, small_pad), F32)], axis=1).astype(BF16)
    n_a, n_b, n_c = (wg_a.shape[1] // COL_TILE, wg_b.shape[1] // COL_TILE, wg_c.shape[1] // COL_TILE)
    assert all(w_.shape[1] % COL_TILE == 0 for w_ in (wg_a, wg_b, wg_c)) and hd == LANE
    QA, KA, VA, RA = 0, sizes[0], sizes[0] + sizes[1], sizes[0] + sizes[1] + sizes[2]
    QB = n_a * COL_TILE
    KB, VB, QI = QB + sizes[5], QB + sizes[5] + sizes[6], QB + sizes[5] + sizes[6] + sizes[7]
    GA = (n_a + n_b) * COL_TILE
    GB = GA + D
    SM = (n_a + n_b + n_c) * COL_TILE
    NW = SM + COL_TILE
    assert all(v_ % COL_TILE == 0 for v_ in (QA, KA, VA, RA, QB, KB, VB, QI, GA, GB))
    assert VB - KB == COL_TILE
    n_ct = NW // COL_TILE
    wau_pad = jnp.zeros((LANE, H * dk), F32).at[IDX_DIM:IDX_DIM + GLA_GATE_RANK].set(w_alpha_up[0]).astype(BF16)
    wa_bf = w_branch_a[0].astype(BF16)
    wb_bf = w_branch_b[0].astype(BF16)
    wo_bf = w_out[0].astype(BF16)
    wup_bf = w_up[0].astype(BF16)
    wdn_bf = w_down[0].astype(BF16)

    front = jnp.zeros((B, ROW0, D), F32).at[0, :DB].set(x_sample[:, 0])
    meta = jnp.broadcast_to(meta_tokens[None].astype(F32), (B, N_META, D))
    x_all = jnp.concatenate([front, meta, x_prompt], axis=1).reshape(R, D)
    rpos = jnp.maximum(jnp.arange(TP, dtype=jnp.int32) - ROW0, 0)
    pos = jnp.concatenate([rpos.at[:DB].set(past), rpos])
    tabs = jnp.concatenate(_rope_tables(pos, hd, 1) + _rope_tables(pos, IDX_DIM, LANE // IDX_DIM), axis=1)

    tm1 = TP // 2
    bpb = TP // tm1
    P = pl.pallas_call(
        functools.partial(_inproj_kernel, n_a=n_a, n_b=n_b, n_c=n_c,
                          b128=(0, (VB - QB) // COL_TILE),
                          b64=((QI - QB) // COL_TILE, (QI - QB + IDX_HEADS * IDX_DIM) // COL_TILE)),
        grid=(R // tm1, n_ct),
        in_specs=[pl.BlockSpec((tm1, D), lambda i, j: (i, 0)),
                  pl.BlockSpec((1, D), lambda i, j: (0, 0)),
                  pl.BlockSpec((D, COL_TILE), lambda i, j: (0, jnp.minimum(j, n_a - 1))),
                  pl.BlockSpec((D, COL_TILE), lambda i, j: (0, jnp.clip(j - n_a, 0, n_b - 1))),
                  pl.BlockSpec((D, COL_TILE), lambda i, j: (0, jnp.clip(j - n_a - n_b, 0, n_c - 1))),
                  pl.BlockSpec((D, COL_TILE), lambda i, j: (0, 0)),
                  pl.BlockSpec((tm1, 6 * LANE), lambda i, j: (jnp.where(i < bpb, i, bpb + i % bpb), 0))],
        out_specs=pl.BlockSpec((tm1, COL_TILE), lambda i, j: (i, j)),
        out_shape=jax.ShapeDtypeStruct((R, NW), F32),
        scratch_shapes=[pltpu.VMEM((tm1, D), BF16)],
        compiler_params=_cparams(("parallel", "arbitrary")),
        name="inproj",
    )(x_all, norm_mix_g, wg_a, wg_b, wg_c, wg_s, tabs)

    C = GLA_CHUNK
    ncb = TP // C
    ba2 = b_alpha.reshape(1, H * dk)
    gain2 = gla_norm_g.reshape(1, H * dv)
    smc = SM // LANE
    a_out, s_fin = pl.pallas_call(
        functools.partial(_gla_prompt_kernel, dk=dk, dv=dv),
        grid=(B, ncb),
        in_specs=[pl.BlockSpec((C, H * dk), lambda b, c: (b * ncb + c, QA // (H * dk))),
                  pl.BlockSpec((C, H * dk), lambda b, c: (b * ncb + c, KA // (H * dk))),
                  pl.BlockSpec((C, H * dv), lambda b, c: (b * ncb + c, VA // (H * dv))),
                  pl.BlockSpec((C, H * dv), lambda b, c: (b * ncb + c, RA // (H * dv))),
                  pl.BlockSpec((C, LANE), lambda b, c: (b * ncb + c, smc)),
                  pl.BlockSpec((LANE, H * dk), lambda b, c: (0, 0)),
                  pl.BlockSpec((1, H * dk), lambda b, c: (0, 0)),
                  pl.BlockSpec((1, H * dv), lambda b, c: (0, 0))],
        out_specs=[pl.BlockSpec((C, H * dv), lambda b, c: (b * ncb + c, 0)),
                   pl.BlockSpec((None, H, dk, dv), lambda b, c: (b, 0, 0, 0))],
        out_shape=[jax.ShapeDtypeStruct((R, H * dv), BF16),
                   jax.ShapeDtypeStruct((B, H, dk, dv), F32)],
        scratch_shapes=[pltpu.VMEM((H, dk, dv), F32)],
        compiler_params=_cparams(("parallel", "arbitrary")),
        name="gla_prompt",
    )(P, P, P, P, P, wau_pad, ba2, gain2)

    p_s = P[:DB]
    P_rows = p_s.reshape(DB, 1, NW)
    a_out, s_new_s = pl.pallas_call(
        functools.partial(_gla_sample_kernel, dk=dk, dv=dv),
        grid=(DB,),
        in_specs=[pl.BlockSpec(memory_space=pl.ANY),
                  pl.BlockSpec((None, 1, H * dk), lambda d: (d, 0, QA // (H * dk))),
                  pl.BlockSpec((None, 1, H * dk), lambda d: (d, 0, KA // (H * dk))),
                  pl.BlockSpec((None, 1, H * dv), lambda d: (d, 0, VA // (H * dv))),
                  pl.BlockSpec((None, 1, H * dv), lambda d: (d, 0, RA // (H * dv))),
                  pl.BlockSpec((None, 1, LANE), lambda d: (d, 0, smc)),
                  pl.BlockSpec((LANE, H * dk), lambda d: (0, 0)),
                  pl.BlockSpec((1, H * dk), lambda d: (0, 0)),
                  pl.BlockSpec((1, H * dv), lambda d: (0, 0)),
                  pl.BlockSpec((None, H, dk, dv), lambda d: (d, 0, 0, 0))],
        out_specs=[pl.BlockSpec((DB, H * dv), lambda d: (0, 0)),
                   pl.BlockSpec((None, H, dk, dv), lambda d: (d, 0, 0, 0))],
        out_shape=[jax.ShapeDtypeStruct((R, H * dv), BF16),
                   jax.ShapeDtypeStruct((DB, H, dk, dv), F32)],
        scratch_shapes=[pltpu.VMEM((DB, H * dv), F32)],
        input_output_aliases={0: 0},
        compiler_params=_cparams(("arbitrary",)),
        name="gla_sample",
    )(a_out, P_rows, P_rows, P_rows, P_rows, P_rows, wau_pad, ba2, gain2, state_gla[0])

    TQ = LANE
    nqb = TP // TQ
    n_cls = -(-nqb // QB_PER_CLASS)
    widths = tuple(min((c + 1) * QB_PER_CLASS * TQ, TP) for c in range(n_cls))
    o_b = jnp.zeros((R, ATT_HEADS * hd), BF16)
    for c, width in enumerate(widths):
        q_lo = c * QB_PER_CLASS
        n_q = min(QB_PER_CLASS, nqb - q_lo)
        row_blk = lambda b, q, q_lo=q_lo: b * nqb + q_lo + q
        specs = [pl.BlockSpec((TQ, IDX_HEADS * IDX_DIM), lambda b, q, r_=row_blk: (r_(b, q), QI // (IDX_HEADS * IDX_DIM))),
                 pl.BlockSpec((TQ, LANE), lambda b, q, r_=row_blk: (r_(b, q), smc)),
                 pl.BlockSpec((TP, LANE), lambda b, q: (b, smc)),
                 pl.BlockSpec((TQ, ATT_HEADS * hd), lambda b, q, r_=row_blk: (r_(b, q), QB // (ATT_HEADS * hd))),
                 pl.BlockSpec((TP, kvw), lambda b, q: (b, KB // kvw)),
                 pl.BlockSpec((TP, kvw), lambda b, q: (b, VB // kvw))]
        o_b = pl.pallas_call(
            functools.partial(_dsa_prompt_kernel, k_sel=k_sel_p, hd=hd, width=width, q_lo=q_lo),
            grid=(B, n_q),
            in_specs=[pl.BlockSpec(memory_space=pl.ANY)] + specs,
            out_specs=pl.BlockSpec((TQ, ATT_HEADS * hd), lambda b, q, r_=row_blk: (r_(b, q), 0)),
            out_shape=jax.ShapeDtypeStruct((R, ATT_HEADS * hd), BF16),
            scratch_shapes=[pltpu.VMEM((width, LANE), BF16), pltpu.VMEM((width, LANE), BF16),
                            pltpu.VMEM((width, kvw), BF16), pltpu.VMEM((width, kvw), BF16),
                            pltpu.VMEM((TQ, width), F32), pltpu.VMEM((TQ, width), F32)],
            input_output_aliases={0: 0},
            compiler_params=_cparams(("parallel", "arbitrary")),
            name=f"dsa_prompt_w{width}",
        )(o_b, P, P, P, P, P, P)

    qi_s = p_s[:, QI:QI + IDX_HEADS * IDX_DIM].reshape(DB, IDX_HEADS, IDX_DIM)
    wi_s = p_s[:, SM + WI_OFF:SM + WI_OFF + IDX_HEADS].reshape(DB, IDX_HEADS, 1)
    ki_s = p_s[:, SM:SM + IDX_DIM].reshape(DB, 1, IDX_DIM)
    sc, scn = pl.pallas_call(
        functools.partial(_dsa_sample_score_kernel, n_pages=n_pages),
        grid_spec=pltpu.PrefetchScalarGridSpec(
            num_scalar_prefetch=1,
            grid=(DB,),
            in_specs=[pl.BlockSpec(memory_space=pl.ANY),
                      pl.BlockSpec((None, IDX_HEADS, IDX_DIM), lambda d, pt: (d, 0, 0)),
                      pl.BlockSpec((None, IDX_HEADS, 1), lambda d, pt: (d, 0, 0)),
                      pl.BlockSpec((None, 1, IDX_DIM), lambda d, pt: (d, 0, 0))],
            out_specs=[pl.BlockSpec((None, n_pages, PAGE_SIZE), lambda d, pt: (d, 0, 0)),
                       pl.BlockSpec((None, 1, LANE), lambda d, pt: (d, 0, 0))],
            scratch_shapes=[pltpu.VMEM((2, n_pages, IDX_DIM, PAGE_SIZE), F32),
                            pltpu.SemaphoreType.DMA((2,))]),
        out_shape=[jax.ShapeDtypeStruct((DB, n_pages, PAGE_SIZE), F32),
                   jax.ShapeDtypeStruct((DB, 1, LANE), F32)],
        compiler_params=_cparams(("arbitrary",)),
        name="dsa_sample_score",
    )(page_table, jnp.swapaxes(cache_idx_k[0], 1, 2), qi_s, wi_s, ki_s)

    hl_np = np.zeros((past + LANE, LANE), np.float32)
    hl_np[:, 0] = np.arange(past + LANE) // PAGE_SIZE
    hl_np[:, 1] = np.arange(past + LANE) % PAGE_SIZE
    hl = jnp.asarray(hl_np, dtype=BF16)
    sel = pl.pallas_call(
        functools.partial(_dsa_sample_select_kernel, k_sel=k_sel_s, chunk=2048),
        out_shape=jax.ShapeDtypeStruct((DB, k_sel_s, LANE), I32),
        scratch_shapes=[pltpu.VMEM((DB, 1, past + LANE), F32), pltpu.VMEM((DB, past + LANE), F32)],
        compiler_params=pltpu.CompilerParams(vmem_limit_bytes=VMEM_LIMIT),
        name="dsa_sample_select",
    )(sc.reshape(DB, past), scn.reshape(DB, LANE), hl)

    sel_s = sel[:, :, 0:2].reshape(DB, 2 * k_sel_s)
    q_s = p_s[:, QB:QB + ATT_HEADS * hd].reshape(DB, ATT_HEADS, hd)
    kn_s = p_s[:, KB:KB + kvw]
    vn_s = p_s[:, VB:VB + kvw]
    o_b = pl.pallas_call(
        functools.partial(_dsa_sample_attend_kernel, k_sel=k_sel_s, n_pages=n_pages, hd=hd),
        grid_spec=pltpu.PrefetchScalarGridSpec(
            num_scalar_prefetch=2,
            grid=(DB,),
            in_specs=[pl.BlockSpec(memory_space=pl.ANY),
                      pl.BlockSpec(memory_space=pl.ANY),
                      pl.BlockSpec(memory_space=pl.ANY),
                      pl.BlockSpec((None, k_sel_s, LANE), lambda d, s_, pt: (d, 0, 0)),
                      pl.BlockSpec((None, ATT_HEADS, hd), lambda d, s_, pt: (d, 0, 0)),
                      pl.BlockSpec((None, 1, kvw), lambda d, s_, pt: (d, 0, KB // kvw)),
                      pl.BlockSpec((None, 1, kvw), lambda d, s_, pt: (d, 0, VB // kvw))],
            out_specs=pl.BlockSpec((DB, ATT_HEADS * hd), lambda d, s_, pt: (0, 0)),
            scratch_shapes=[pltpu.VMEM((2, k_sel_s, ATT_KV_HEADS, hd), F32),
                            pltpu.VMEM((2, k_sel_s, ATT_KV_HEADS, hd), F32),
                            pltpu.VMEM((DB, ATT_HEADS, hd), F32), pltpu.SemaphoreType.DMA((2, 2))]),
        out_shape=jax.ShapeDtypeStruct((R, ATT_HEADS * hd), BF16),
        input_output_aliases={2: 0},
        compiler_params=_cparams(("arbitrary",)),
        name="dsa_sample_attend",
    )(sel_s, page_table, o_b, cache_k[0], cache_v[0], sel, q_s, P_rows, P_rows)

    mix = pl.pallas_call(
        _mix_kernel,
        grid=(R // tm1, D // COL_TILE),
        in_specs=[pl.BlockSpec((tm1, H * dv), lambda i, j: (i, 0)),
                  pl.BlockSpec((tm1, ATT_HEADS * hd), lambda i, j: (i, 0)),
                  pl.BlockSpec((H * dv, COL_TILE), lambda i, j: (0, j)),
                  pl.BlockSpec((ATT_HEADS * hd, COL_TILE), lambda i, j: (0, j)),
                  pl.BlockSpec((tm1, COL_TILE), lambda i, j: (i, GA // COL_TILE + j)),
                  pl.BlockSpec((tm1, COL_TILE), lambda i, j: (i, GB // COL_TILE + j))],
        out_specs=pl.BlockSpec((tm1, COL_TILE), lambda i, j: (i, j)),
        out_shape=jax.ShapeDtypeStruct((R, D), BF16),
        compiler_params=_cparams(("parallel", "arbitrary")),
        name="mix",
    )(a_out, o_b, wa_bf, wb_bf, P, P)

    tm2 = TP // 4
    x1, h2 = pl.pallas_call(
        _outproj_kernel,
        grid=(R // tm2,),
        in_specs=[pl.BlockSpec((tm2, D), lambda i: (i, 0)),
                  pl.BlockSpec((tm2, D), lambda i: (i, 0)),
                  pl.BlockSpec((D, D), lambda i: (0, 0)),
                  pl.BlockSpec((1, D), lambda i: (0, 0))],
        out_specs=[pl.BlockSpec((tm2, D), lambda i: (i, 0)),
                   pl.BlockSpec((tm2, D), lambda i: (i, 0))],
        out_shape=[jax.ShapeDtypeStruct((R, D), F32), jax.ShapeDtypeStruct((R, D), BF16)],
        compiler_params=_cparams(("parallel",)),
        name="outproj",
    )(x_all, mix, wo_bf, norm_ffn_g)

    nrb = R // tm2
    nft = dff // COL_TILE
    y_all, a_head, a_tail = pl.pallas_call(
        functools.partial(_ffn_kernel, n_dec=DB),
        grid=(nrb, nft),
        in_specs=[pl.BlockSpec((tm2, D), lambda i, j: (i, 0)),
                  pl.BlockSpec((16, D), lambda i, j: (jnp.maximum(i * (tm2 // 16) - 1, 0), 0)),
                  pl.BlockSpec((tm2, D), lambda i, j: (i, 0)),
                  pl.BlockSpec((D, COL_TILE), lambda i, j: (0, j)),
                  pl.BlockSpec((D, COL_TILE), lambda i, j: (0, nft + j)),
                  pl.BlockSpec((COL_TILE, D), lambda i, j: (j, 0)),
                  pl.BlockSpec((CONV_W, COL_TILE), lambda i, j: (0, j)),
                  pl.BlockSpec((1, COL_TILE), lambda i, j: (0, j)),
                  pl.BlockSpec((DB, COL_TILE), lambda i, j: (0, j)),
                  pl.BlockSpec((DB, COL_TILE), lambda i, j: (0, j)),
                  pl.BlockSpec((1, D), lambda i, j: (0, 0))],
        out_specs=[pl.BlockSpec((tm2, D), lambda i, j: (i, 0)),
                   pl.BlockSpec((DB, COL_TILE), lambda i, j: (i, j)),
                   pl.BlockSpec((8, COL_TILE), lambda i, j: (i, j))],
        out_shape=[jax.ShapeDtypeStruct((R, D), F32),
                   jax.ShapeDtypeStruct((nrb * DB, dff), F32),
                   jax.ShapeDtypeStruct((nrb * 8, dff), F32)],
        scratch_shapes=[pltpu.VMEM((tm2, D), F32), pltpu.VMEM((tm2, COL_TILE), F32),
                        pltpu.VMEM((tm2, COL_TILE), F32)],
        compiler_params=_cparams(("parallel", "arbitrary")),
        name="ffn",
    )(h2, h2, x1, wup_bf, wup_bf, wdn_bf, conv_w[0], conv_b, state_conv[0, :, 0], state_conv[0, :, 1],
      norm_final_g.reshape(1, D))

    y3 = y_all.reshape(B, TP, D)
    y_prompt = y3[:, FRONT:]
    y_sample = y_all[:DB].reshape(DB, 1, D)
    P3 = P.reshape(B, TP, NW)
    new_k_p = P3[:, ROW0:, KB:KB + kvw].reshape(1, B, T, ATT_KV_HEADS, hd)
    new_v_p = P3[:, ROW0:, VB:VB + kvw].reshape(1, B, T, ATT_KV_HEADS, hd)
    new_ki_p = P3[:, ROW0:, SM:SM + IDX_DIM].reshape(1, B, T, IDX_DIM)
    new_gla_p = s_fin[None]
    tails = a_tail.reshape(B, nrb // B, 8, dff)[:, -1, 8 - (CONV_W - 1):, :]
    new_conv_p = tails[None]
    new_k_s = kn_s.reshape(1, DB, 1, ATT_KV_HEADS, hd)
    new_v_s = vn_s.reshape(1, DB, 1, ATT_KV_HEADS, hd)
    new_ki_s = ki_s.reshape(1, DB, 1, IDX_DIM)
    new_gla_s = s_new_s[None]
    new_conv_s = jnp.stack([state_conv[0, :, 1], a_head[:DB]], axis=1)[None]
    return (y_prompt, y_sample, new_k_p, new_v_p, new_ki_p, new_gla_p, new_conv_p,
            new_k_s, new_v_s, new_ki_s, new_gla_s, new_conv_s)
```

```python
import functools
import math

import jax
import jax.numpy as jnp
import numpy as np
from jax import lax
from jax.experimental import pallas as pl
from jax.experimental.pallas import tpu as pltpu

F32 = jnp.float32
BF16 = jnp.bfloat16
I32 = jnp.int32

N_META = 16
GLA_HEADS = 4
GLA_GATE_RANK = 16
GLA_GATE_NORM = 16.0
GLA_CHUNK = 64
GLA_SUB = 8
ATT_HEADS = 16
ATT_KV_HEADS = 4
IDX_HEADS = 16
IDX_DIM = 64
TOPK_MAX = 256
ROPE_THETA = 500000.0
CONV_W = 3
EPS = 1e-6
PAGE_SIZE = 128
WI_OFF = IDX_DIM + GLA_GATE_RANK

LANE = 128
FRONT = 128
ROW0 = FRONT - N_META
COL_TILE = 512
QB_PER_CLASS = 2
VMEM_LIMIT = 56 * 1024 * 1024
INT_MIN = -2 ** 31
LOG2_E = 1.4426950408889634


def _cparams(sem):
    return pltpu.CompilerParams(dimension_semantics=sem, vmem_limit_bytes=VMEM_LIMIT)


def _sigmoid(x):
    return 1.0 / (1.0 + jnp.exp(-x))


def _pattern_value(p):
    return pltpu.bitcast(jnp.where(p < 0, p ^ jnp.int32(0x7FFFFFFF), p), F32)


def _kth_largest(load_scores, rows, k_sel):
    def body(it, t):
        step = jnp.left_shift(jnp.int32(1), 30 - 2 * it)
        scores = load_scores()
        adv = jnp.zeros((rows, 1), I32)
        for m in (1, 2, 3):
            cnt = jnp.sum((scores >= _pattern_value(t + m * step)).astype(I32), axis=1, keepdims=True)
            adv = adv + (cnt >= k_sel).astype(I32)
        return t + adv * step

    t = lax.fori_loop(0, 16, body, jnp.full((rows, 1), INT_MIN, I32))
    return jnp.where(t == INT_MIN, -jnp.inf, _pattern_value(t))


def _rope_block(xb, c, sa, sb, half):
    return xb * c + pltpu.roll(xb, LANE - half, 1) * sa + pltpu.roll(xb, half, 1) * sb


def _inproj_kernel(x_ref, g_ref, wa_ref, wb_ref, wc_ref, ws_ref, tab_ref, o_ref, h_ref,
                   *, n_a, n_b, n_c, b128, b64):
    j = pl.program_id(1)

    @pl.when(j == 0)
    def _():
        x = x_ref[...]
        ms = jnp.mean(x * x, axis=-1, keepdims=True)
        h_ref[...] = (x * lax.rsqrt(ms + EPS) * g_ref[...]).astype(BF16)

    def proj(w_ref):
        return jnp.dot(h_ref[...], w_ref[...], preferred_element_type=F32)

    @pl.when(j < n_a)
    def _():
        o_ref[...] = proj(wa_ref)

    @pl.when(jnp.logical_and(j >= n_a, j < n_a + n_b))
    def _():
        acc = proj(wb_ref)
        jb = j - n_a
        is128 = jnp.logical_and(jb >= b128[0], jb < b128[1])
        is64 = jnp.logical_and(jb >= b64[0], jb < b64[1])

        @pl.when(is128)
        def _():
            c, sa, sb = tab_ref[:, 0:128], tab_ref[:, 128:256], tab_ref[:, 256:384]
            for blk in range(COL_TILE // LANE):
                sl = slice(blk * LANE, (blk + 1) * LANE)
                o_ref[:, sl] = _rope_block(acc[:, sl], c, sa, sb, 16)

        @pl.when(is64)
        def _():
            c, sa, sb = tab_ref[:, 384:512], tab_ref[:, 512:640], tab_ref[:, 640:768]
            for blk in range(COL_TILE // LANE):
                sl = slice(blk * LANE, (blk + 1) * LANE)
                o_ref[:, sl] = _rope_block(acc[:, sl], c, sa, sb, 8)

        @pl.when(jnp.logical_not(is128 | is64))
        def _():
            o_ref[...] = acc

    @pl.when(jnp.logical_and(j >= n_a + n_b, j < n_a + n_b + n_c))
    def _():
        o_ref[...] = proj(wc_ref)

    @pl.when(j == n_a + n_b + n_c)
    def _():
        acc = proj(ws_ref)
        lane = lax.broadcasted_iota(I32, (1, LANE), 1)
        first = lane < IDX_DIM
        c = jnp.where(first, tab_ref[:, 384:512], 1.0)
        sa = jnp.where(first, tab_ref[:, 512:640], 0.0)
        sb = jnp.where(first, tab_ref[:, 640:768], 0.0)
        o_ref[:, 0:LANE] = _rope_block(acc[:, 0:LANE], c, sa, sb, 8)
        o_ref[:, LANE:] = acc[:, LANE:]


def _log_alpha(sm, wau_ref, ba_ref):
    x = jnp.dot(sm.astype(BF16), wau_ref[...], preferred_element_type=F32) + ba_ref[...]
    return (jnp.minimum(x, 0.0) - jnp.log(1.0 + jnp.exp(-jnp.abs(x)))) * (1.0 / GLA_GATE_NORM)


def _row_to_col(row, n):
    eye = lax.broadcasted_iota(I32, (n, n), 0) == lax.broadcasted_iota(I32, (n, n), 1)
    return jnp.sum(jnp.where(eye, row, 0.0), axis=1, keepdims=True)


def _readout(o, r, gain):
    ms = jnp.mean(o * o, axis=-1, keepdims=True)
    return o * lax.rsqrt(ms + EPS) * gain * (r * _sigmoid(r))


def _gla_chunk(q, k, v, g, S):
    C, dk = q.shape
    vb = v.astype(BF16)
    row = lax.broadcasted_iota(I32, g.shape, 0)
    b = g
    sh = 1
    while sh < C:
        b = b + jnp.where(row >= sh, pltpu.roll(b, sh, 0), 0.0)
        sh *= 2
    bl = b[C - 1:C, :]
    o = jnp.dot((q * jnp.exp(b)).astype(BF16), S.astype(BF16), preferred_element_type=F32)

    tcol = lax.broadcasted_iota(I32, (GLA_SUB, C), 1)
    trow = lax.broadcasted_iota(I32, (GLA_SUB, C), 0)
    a_rows = []
    kf = None
    prev_ref = None
    for blk in range(C // GLA_SUB):
        r0 = blk * GLA_SUB
        ref = b[r0 - 1:r0, :] if blk > 0 else jnp.zeros_like(bl)
        b_i = b[r0:r0 + GLA_SUB, :]
        q_i = q[r0:r0 + GLA_SUB, :]
        if blk == 0:
            a_off = jnp.zeros((GLA_SUB, C), F32)
        else:
            fresh = k[r0 - GLA_SUB:r0, :] * jnp.exp(ref - b[r0 - GLA_SUB:r0, :])
            kf = fresh if blk == 1 else jnp.concatenate([kf * jnp.exp(ref - prev_ref), fresh], axis=0)
            kf_full = jnp.concatenate([kf, jnp.zeros((C - r0, dk), F32)], axis=0)
            qe = q_i * jnp.exp(b_i - ref)
            a_off = lax.dot_general(qe.astype(BF16), kf_full.astype(BF16), (((1,), (1,)), ((), ())),
                                    preferred_element_type=F32)
        prev_ref = ref
        diag = jnp.zeros((GLA_SUB, C), F32)
        for sl in range(GLA_SUB):
            s = r0 + sl
            w = q_i * k[s:s + 1, :] * jnp.exp(jnp.minimum(b_i - b[s:s + 1, :], 0.0))
            diag = jnp.where(tcol == s, jnp.sum(w, axis=1, keepdims=True), diag)
        a_rows.append(jnp.where(tcol < r0, a_off, jnp.where(tcol <= trow + r0, diag, 0.0)))
    a = jnp.concatenate(a_rows, axis=0)
    o = o + jnp.dot(a.astype(BF16), vb, preferred_element_type=F32)

    kd = k * jnp.exp(bl - b)
    upd = lax.dot_general(kd.astype(BF16), vb, (((0,), (0,)), ((), ())),
                          preferred_element_type=F32)
    return o, S * _row_to_col(jnp.exp(bl), dk) + upd


def _gla_prompt_kernel(q_ref, k_ref, v_ref, r_ref, sm_ref, wau_ref, ba_ref, gain_ref,
                       o_ref, sfin_ref, s_ref, *, dk, dv):
    c = pl.program_id(1)
    H = s_ref.shape[0]

    @pl.when(c == 0)
    def _():
        s_ref[...] = jnp.zeros_like(s_ref)
        o_ref[...] = jnp.zeros_like(o_ref)

    @pl.when(c > 0)
    def _():
        g_all = _log_alpha(sm_ref[...], wau_ref, ba_ref)
        for h in range(H):
            ks, vs = slice(h * dk, (h + 1) * dk), slice(h * dv, (h + 1) * dv)
            o, s_new = _gla_chunk(q_ref[:, ks] * (dk ** -0.5), k_ref[:, ks], v_ref[:, vs],
                                  g_all[:, ks], s_ref[h])
            s_ref[h] = s_new
            o_ref[:, vs] = _readout(o, r_ref[:, vs], gain_ref[:, vs]).astype(BF16)

    @pl.when(c == pl.num_programs(1) - 1)
    def _():
        sfin_ref[...] = s_ref[...]


def _gla_sample_kernel(alias_ref, q_ref, k_ref, v_ref, r_ref, sm_ref, wau_ref, ba_ref, gain_ref,
                       st_ref, o_ref, snew_ref, acc_ref, *, dk, dv):
    del alias_ref
    d = pl.program_id(0)
    H = st_ref.shape[0]
    g_all = _log_alpha(jnp.broadcast_to(sm_ref[...], (8, LANE)), wau_ref, ba_ref)[0:1, :]
    ons = []
    for h in range(H):
        ks, vs = slice(h * dk, (h + 1) * dk), slice(h * dv, (h + 1) * dv)
        q = q_ref[:, ks] * (dk ** -0.5)
        k = k_ref[:, ks]
        v = v_ref[:, vs]
        eg = jnp.exp(g_all[:, ks])
        S = st_ref[h]
        qe = jnp.broadcast_to(q * eg, (8, dk))
        o = jnp.dot(qe.astype(BF16), S.astype(BF16), preferred_element_type=F32)[0:1, :]
        o = o + jnp.sum(q * k, axis=1, keepdims=True) * v
        snew_ref[h] = S * _row_to_col(eg, dk) + _row_to_col(k, dk) * v
        ons.append(_readout(o, r_ref[:, vs], gain_ref[:, vs]))
    on = jnp.concatenate(ons, axis=1)
    row = lax.broadcasted_iota(I32, acc_ref.shape, 0)

    @pl.when(d == 0)
    def _():
        acc_ref[...] = jnp.zeros_like(acc_ref)

    acc_ref[...] = jnp.where(row == d, on, acc_ref[...])

    @pl.when(d == pl.num_programs(0) - 1)
    def _():
        o_ref[...] = acc_ref[...].astype(BF16)


def _keep_lowest_ties(key_ref, bias_ref, thr, TK, k_sel):
    sc = key_ref[:, 0:TK]
    gt = sc > thr
    eq = jnp.logical_and(sc == thr, sc > -jnp.inf)
    room = k_sel - jnp.sum(jnp.where(gt, 1.0, 0.0), axis=1, keepdims=True)
    upper = jnp.where(lax.broadcasted_iota(I32, (LANE, LANE), 0) < lax.broadcasted_iota(I32, (LANE, LANE), 1),
                      1.0, 0.0).astype(BF16)
    run = jnp.zeros_like(room)
    for c in range(TK // LANE):
        blk = slice(c * LANE, (c + 1) * LANE)
        e = jnp.where(eq[:, blk], 1.0, 0.0)
        before = jnp.dot(e.astype(BF16), upper, preferred_element_type=F32) + run
        keep = jnp.logical_or(gt[:, blk], jnp.logical_and(eq[:, blk], before < room))
        bias_ref[:, blk] = jnp.where(keep, 0.0, -jnp.inf)
        run = run + jnp.sum(e, axis=1, keepdims=True)


def _dsa_prompt_kernel(alias_ref, qi_ref, smq_ref, smk_ref, qb_ref, kb_ref, vb_ref, o_ref,
                       ka_ref, kbb_ref, kbf_ref, vbf_ref, key_ref, bias_ref, *, k_sel, hd, width, q_lo):
    del alias_ref
    qb = q_lo + pl.program_id(1)
    TQ = qi_ref.shape[0]
    nt = (((1,), (1,)), ((), ()))
    G = ATT_HEADS // ATT_KV_HEADS

    @pl.when(pl.program_id(1) == 0)
    def _():
        smk = smk_ref[0:width, :]
        lane = lax.broadcasted_iota(I32, (1, LANE), 1)
        ka_ref[...] = jnp.where(lane < IDX_DIM, smk, 0.0).astype(BF16)
        kbb_ref[...] = jnp.where(lane >= IDX_DIM, pltpu.roll(smk, IDX_DIM, 1), 0.0).astype(BF16)
        kbf_ref[...] = kb_ref[0:width, :].astype(BF16)
        vbf_ref[...] = vb_ref[0:width, :].astype(BF16)

    def body(TK):
        score = jnp.zeros((TQ, TK), F32)
        for p in range(IDX_HEADS // 2):
            qp = qi_ref[:, p * LANE:(p + 1) * LANE].astype(BF16)
            for half, kref in ((0, ka_ref), (1, kbb_ref)):
                h = 2 * p + half
                s = lax.dot_general(qp, kref[0:TK, :], nt, preferred_element_type=F32)
                w = smq_ref[:, WI_OFF + h:WI_OFF + h + 1] * (IDX_HEADS ** -0.5 * IDX_DIM ** -0.5)
                score = score + w * jnp.maximum(s, 0.0)

        qrow = qb * TQ + lax.broadcasted_iota(I32, (TQ, 1), 0)
        kcol = lax.broadcasted_iota(I32, (1, TK), 1)
        adm = jnp.logical_and(kcol <= qrow, kcol >= ROW0)
        key_ref[:, 0:TK] = jnp.where(adm, score, -jnp.inf)
        thr = _kth_largest(lambda: key_ref[:, 0:TK], TQ, k_sel)
        sc = key_ref[:, 0:TK]
        ge = jnp.logical_and(sc >= thr, sc > -jnp.inf)
        bias_ref[:, 0:TK] = jnp.where(ge, 0.0, -jnp.inf)
        n_ge = jnp.sum(jnp.where(ge, 1.0, 0.0), axis=1, keepdims=True)

        @pl.when(jnp.max(n_ge) > k_sel)
        def _():
            _keep_lowest_ties(key_ref, bias_ref, thr, TK, k_sel)

        def qk(n):
            q4 = jnp.concatenate(
                [(qb_ref[:, (n * G + gq) * hd:(n * G + gq + 1) * hd] * (hd ** -0.5 * LOG2_E)).astype(BF16)
                 for gq in range(G)], axis=0)
            return lax.dot_general(q4, kbf_ref[0:TK, n * hd:(n + 1) * hd], nt, preferred_element_type=F32)

        s_next = qk(0)
        for n in range(ATT_KV_HEADS):
            s = s_next
            if n + 1 < ATT_KV_HEADS:
                s_next = qk(n + 1)
            s = (s.reshape(G, TQ, TK) + bias_ref[:, 0:TK][None]).reshape(G * TQ, TK)
            m = jnp.max(s, axis=1, keepdims=True)
            m = jnp.where(m == -jnp.inf, 0.0, m)
            p = jnp.exp2(s - m)
            l = jnp.sum(p, axis=1, keepdims=True)
            o = jnp.dot(p.astype(BF16), vbf_ref[0:TK, n * hd:(n + 1) * hd], preferred_element_type=F32)
            o = jnp.where(l > 0.0, o / l, 0.0)
            for gq in range(G):
                o_ref[:, (n * G + gq) * hd:(n * G + gq + 1) * hd] = o[gq * TQ:(gq + 1) * TQ, :].astype(BF16)

    body(width)


def _idx_page_copy(cache_ref, buf_ref, sem, page, slot, p):
    return pltpu.make_async_copy(cache_ref.at[page], buf_ref.at[slot, p], sem.at[slot])


def _dsa_sample_score_kernel(pt_ref, cache_ref, qi_ref, wi_ref, kin_ref, sc_ref, scn_ref,
                             buf_ref, sem, *, n_pages):
    d = pl.program_id(0)
    slot = d % 2

    def fetch(dd, sl):
        def start(p, carry):
            _idx_page_copy(cache_ref, buf_ref, sem, pt_ref[dd, p], sl, p).start()
            return carry
        lax.fori_loop(0, n_pages, start, 0, unroll=8)

    @pl.when(d == 0)
    def _():
        fetch(0, 0)

    @pl.when(d + 1 < pl.num_programs(0))
    def _():
        fetch(d + 1, 1 - slot)

    def wait(p, carry):
        _idx_page_copy(cache_ref, buf_ref, sem, 0, slot, p).wait()
        return carry

    lax.fori_loop(0, n_pages, wait, 0, unroll=8)
    nt = (((1,), (1,)), ((), ()))
    qi = qi_ref[...].astype(BF16)
    w = wi_ref[...] * (IDX_HEADS ** -0.5 * IDX_DIM ** -0.5)
    kp = buf_ref[slot].astype(BF16)
    qib = jnp.broadcast_to(qi[None], (n_pages, IDX_HEADS, IDX_DIM))
    s = lax.dot_general(qib, kp, (((2,), (1,)), ((0,), (0,))), preferred_element_type=F32)
    sc_ref[...] = jnp.sum(w[None] * jnp.maximum(s, 0.0), axis=1)
    kn = jnp.broadcast_to(kin_ref[...], (8, IDX_DIM)).astype(BF16)
    sn = lax.dot_general(qi, kn, nt, preferred_element_type=F32)[:, 0:1]
    scn_ref[...] = jnp.broadcast_to(jnp.sum(w * jnp.maximum(sn, 0.0), axis=0, keepdims=True), (1, LANE))


def _dsa_sample_select_kernel(sc_ref, scn_ref, hl_ref, sel_ref, pos_ref, key_ref, *, k_sel, chunk):
    DB, NP = sc_ref.shape
    nblk = NP // LANE
    lane1 = lax.broadcasted_iota(I32, (1, LANE), 1)
    key_ref[:, 0:NP] = sc_ref[...]
    key_ref[:, NP:] = jnp.where(lane1 == 0, scn_ref[...], -jnp.inf)
    thr = _kth_largest(lambda: key_ref[...], DB, k_sel)
    keys = key_ref[...]
    gt = keys > thr
    eq = jnp.logical_and(keys == thr, keys > -jnp.inf)
    n_gt = jnp.sum(gt.astype(I32), axis=1, keepdims=True).astype(F32)

    iu = lax.broadcasted_iota(I32, (LANE, LANE), 0)
    ju = lax.broadcasted_iota(I32, (LANE, LANE), 1)
    upper = jnp.where(iu < ju, 1.0, 0.0).astype(BF16)
    ones = jnp.ones((LANE, LANE), BF16)

    def excl_prefix(mask):
        mb = jnp.where(mask, 1.0, 0.0).astype(BF16)
        stacked = jnp.concatenate([mb[:, c * LANE:(c + 1) * LANE] for c in range(nblk + 1)], axis=0)
        within = jnp.dot(stacked, upper, preferred_element_type=F32)
        tot = jnp.dot(stacked, ones, preferred_element_type=F32)[:, 0:1]
        outs = []
        run = jnp.zeros((DB, 1), F32)
        for c in range(nblk + 1):
            outs.append(within[c * DB:(c + 1) * DB, :] + run)
            run = run + tot[c * DB:(c + 1) * DB, :]
        return jnp.concatenate(outs, axis=1)

    pos_gt = excl_prefix(gt)
    pos_eq = excl_prefix(eq) + n_gt
    keep_eq = jnp.logical_and(eq, pos_eq < k_sel)
    pos = jnp.where(gt, pos_gt, jnp.where(keep_eq, pos_eq, -1.0))
    for d in range(DB):
        pos_ref[d] = pos[d:d + 1, :]

    jrow = lax.broadcasted_iota(I32, (k_sel, 1), 0).astype(F32)
    width = NP + LANE
    def compact(d, carry):
        acc = jnp.zeros((k_sel, LANE), F32)
        for c0 in range(0, width, chunk):
            c1 = min(c0 + chunk, width)
            e = jnp.where(pos_ref[d, :, c0:c1] == jrow, 1.0, 0.0).astype(BF16)
            acc = acc + jnp.dot(e, hl_ref[c0:c1, :], preferred_element_type=F32)
        sel_ref[d] = acc.astype(I32)
        return carry

    lax.fori_loop(0, DB, compact, 0)


def _kv_copies(ck_ref, cv_ref, kbuf_ref, vbuf_ref, sem, page, slot, buf, j):
    ck = pltpu.make_async_copy(ck_ref.at[page, slot], kbuf_ref.at[buf, j], sem.at[0, buf])
    cv = pltpu.make_async_copy(cv_ref.at[page, slot], vbuf_ref.at[buf, j], sem.at[1, buf])
    return ck, cv


def _dsa_sample_attend_kernel(sel_s_ref, pt_ref, alias_ref, ck_ref, cv_ref, selv_ref, q_ref,
                              kn_ref, vn_ref, o_ref, kbuf_ref, vbuf_ref, acc_ref, sem,
                              *, k_sel, n_pages, hd):
    del alias_ref
    d = pl.program_id(0)
    buf = d % 2

    def fetch(dd, bb):
        def start(j, carry):
            page = jnp.minimum(sel_s_ref[dd, 2 * j], n_pages - 1)
            ck, cv = _kv_copies(ck_ref, cv_ref, kbuf_ref, vbuf_ref, sem, pt_ref[dd, page],
                                sel_s_ref[dd, 2 * j + 1], bb, j)
            ck.start()
            cv.start(priority=1)
            return carry
        lax.fori_loop(0, k_sel, start, 0, unroll=8)

    @pl.when(d == 0)
    def _():
        fetch(0, 0)

    @pl.when(d + 1 < pl.num_programs(0))
    def _():
        fetch(d + 1, 1 - buf)

    def wait(j, carry):
        ck, cv = _kv_copies(ck_ref, cv_ref, kbuf_ref, vbuf_ref, sem, 0, 0, buf, j)
        ck.wait()
        cv.wait()
        return carry

    lax.fori_loop(0, k_sel, wait, 0, unroll=8)
    is_new = selv_ref[:, 0:1] >= n_pages
    G = ATT_HEADS // ATT_KV_HEADS
    nt = (((1,), (1,)), ((), ()))
    q = q_ref[...] * (hd ** -0.5)
    outs = []
    for n in range(ATT_KV_HEADS):
        hs = slice(n * hd, (n + 1) * hd)
        kk = jnp.where(is_new, kn_ref[:, hs], kbuf_ref[buf, :, n, :]).astype(BF16)
        vv = jnp.where(is_new, vn_ref[:, hs], vbuf_ref[buf, :, n, :]).astype(BF16)
        qn = jnp.concatenate([q[n * G:(n + 1) * G, :], jnp.zeros((8 - G, hd), F32)], axis=0).astype(BF16)
        s = lax.dot_general(qn, kk, nt, preferred_element_type=F32)
        m = jnp.max(s, axis=1, keepdims=True)
        p = jnp.exp(s - m)
        l = jnp.sum(p, axis=1, keepdims=True)
        o = jnp.dot(p.astype(BF16), vv, preferred_element_type=F32) / l
        outs.append(o[0:G, :])
    acc_ref[pl.ds(d, 1)] = jnp.concatenate(outs, axis=0)[None]

    @pl.when(d == pl.num_programs(0) - 1)
    def _():
        for h in range(ATT_HEADS):
            o_ref[:, h * hd:(h + 1) * hd] = acc_ref[:, h, :].astype(BF16)


def _mix_kernel(a_ref, b_ref, wa_ref, wb_ref, ga_ref, gb_ref, o_ref):
    ba = jnp.dot(a_ref[...], wa_ref[...], preferred_element_type=F32)
    bb = jnp.dot(b_ref[...], wb_ref[...], preferred_element_type=F32)
    o_ref[...] = (_sigmoid(ga_ref[...]) * ba + _sigmoid(gb_ref[...]) * bb).astype(BF16)


def _outproj_kernel(x_ref, m_ref, w_ref, g_ref, x1_ref, h_ref):
    x1 = x_ref[...] + jnp.dot(m_ref[...], w_ref[...], preferred_element_type=F32)
    x1_ref[...] = x1
    ms = jnp.mean(x1 * x1, axis=-1, keepdims=True)
    h_ref[...] = (x1 * lax.rsqrt(ms + EPS) * g_ref[...]).astype(BF16)


def _ffn_kernel(h_ref, halo_ref, x1_ref, wa_ref, wb_ref, wd_ref, cw_ref, cb_ref, st0_ref, st1_ref,
                gf_ref, y_ref, head_ref, tail_ref, acc_ref, *, n_dec):
    i = pl.program_id(0)
    j = pl.program_id(1)

    @pl.when(j == 0)
    def _():
        acc_ref[...] = jnp.zeros_like(acc_ref)

    h = h_ref[...]
    ah = jnp.dot(halo_ref[...], wa_ref[...], preferred_element_type=F32)
    ah = jnp.where(i > 0, ah, 0.0)
    a = jnp.dot(h, wa_ref[...], preferred_element_type=F32)
    row = lax.broadcasted_iota(I32, a.shape, 0)
    hl = ah.shape[0]
    s1 = jnp.where(row == 0, ah[hl - 1:hl, :], pltpu.roll(a, 1, 0))
    s2 = jnp.where(row == 0, ah[hl - 2:hl - 1, :], jnp.where(row == 1, ah[hl - 1:hl, :], pltpu.roll(a, 2, 0)))
    dec = jnp.logical_and(i == 0, row < n_dec)
    pad = jnp.zeros((a.shape[0] - n_dec, a.shape[1]), F32)
    s1 = jnp.where(dec, jnp.concatenate([st1_ref[...], pad], axis=0), s1)
    s2 = jnp.where(dec, jnp.concatenate([st0_ref[...], pad], axis=0), s2)
    conv = cb_ref[...] + cw_ref[0:1, :] * s2 + cw_ref[1:2, :] * s1 + cw_ref[2:3, :] * a
    act = conv * _sigmoid(conv)
    b = jnp.dot(h, wb_ref[...], preferred_element_type=F32)
    gate = (act * b).astype(BF16)
    acc_ref[...] = acc_ref[...] + jnp.dot(gate, wd_ref[...], preferred_element_type=F32)
    head_ref[...] = a[0:head_ref.shape[0], :]
    tail_ref[...] = a[a.shape[0] - 8:, :]

    @pl.when(j == pl.num_programs(1) - 1)
    def _():
        x2 = x1_ref[...] + acc_ref[...]
        ms = jnp.mean(x2 * x2, axis=-1, keepdims=True)
        y_ref[...] = x2 * lax.rsqrt(ms + EPS) * gf_ref[...]


def _rope_tables(pos, hd, reps):
    rot = hd // 4
    half = rot // 2
    inv = jnp.exp(-math.log(ROPE_THETA) * jnp.arange(half, dtype=F32) * 2.0 / rot)
    ang = pos.astype(F32)[:, None] * inv[None, :]
    cos, sin = jnp.cos(ang), jnp.sin(ang)
    n = pos.shape[0]
    one = jnp.ones((n, hd - rot), F32)
    zero_r = jnp.zeros((n, hd - rot), F32)
    zero_h = jnp.zeros((n, half), F32)
    c = jnp.concatenate([cos, cos, one], axis=1)
    sa = jnp.concatenate([-sin, zero_h, zero_r], axis=1)
    sb = jnp.concatenate([zero_h, sin, zero_r], axis=1)
    return [jnp.tile(t, (1, reps)) for t in (c, sa, sb)]


def kernel(x_prompt, x_sample, cache_k, cache_v, cache_idx_k, state_gla, state_conv, page_table, meta_tokens, norm_mix_g, w_in, w_alpha_up, b_alpha, gla_norm_g, w_branch_a, w_branch_b, w_out, norm_ffn_g, w_up, conv_w, conv_b, w_down, norm_final_g):
    B, SEQ, D = x_prompt.shape
    DB = x_sample.shape[0]
    assert x_sample.shape[1] == 1 and w_in.shape[0] == 1
    n_pool = cache_k.shape[1]
    n_pages = page_table.shape[1]
    past = n_pages * PAGE_SIZE
    assert n_pages == LANE and PAGE_SIZE == LANE
    dff = w_down.shape[1]
    H = GLA_HEADS
    dk = D // 2 // H
    dv = D // H
    hd = D // ATT_HEADS
    kvw = ATT_KV_HEADS * hd
    T = SEQ + N_META
    TP = SEQ + FRONT
    R = B * TP
    assert DB <= GLA_CHUNK and DB % 16 == 0 and SEQ % LANE == 0 and dff % COL_TILE == 0
    k_sel_p = min(TOPK_MAX, T // 4)
    k_sel_s = min(TOPK_MAX, (past + 1) // 4)

    sizes = (H * dk, H * dk, H * dv, H * dv, GLA_GATE_RANK, ATT_HEADS * hd, kvw, kvw,
             IDX_HEADS * IDX_DIM, IDX_DIM, IDX_HEADS, D, D)
    offs = [0]
    for s_ in sizes:
        offs.append(offs[-1] + s_)
    w0 = w_in[0]
    wg_a = w0[:, offs[0]:offs[4]].astype(BF16)
    wg_b = w0[:, offs[5]:offs[9]].astype(BF16)
    wg_c = w0[:, offs[11]:offs[13]].astype(BF16)
    small_pad = COL_TILE - (IDX_DIM + GLA_GATE_RANK + IDX_HEADS)
    wg_s = jnp.concatenate([w0[:, offs[9]:offs[10]], w0[:, offs[4]:offs[5]], w0[:, offs[10]:offs[11]],
                            jnp.zeros((D, small_pad), F32)], axis=1).astype(BF16)
    n_a, n_b, n_c = (wg_a.shape[1] // COL_TILE, wg_b.shape[1] // COL_TILE, wg_c.shape[1] // COL_TILE)
    assert all(w_.shape[1] % COL_TILE == 0 for w_ in (wg_a, wg_b, wg_c)) and hd == LANE
    QA, KA, VA, RA = 0, sizes[0], sizes[0] + sizes[1], sizes[0] + sizes[1] + sizes[2]
    QB = n_a * COL_TILE
    KB, VB, QI = QB + sizes[5], QB + sizes[5] + sizes[6], QB + sizes[5] + sizes[6] + sizes[7]
    GA = (n_a + n_b) * COL_TILE
    GB = GA + D
    SM = (n_a + n_b + n_c) * COL_TILE
    NW = SM + COL_TILE
    assert all(v_ % COL_TILE == 0 for v_ in (QA, KA, VA, RA, QB, KB, VB, QI, GA, GB))
    assert VB - KB == COL_TILE
    n_ct = NW // COL_TILE
    wau_pad = jnp.zeros((LANE, H * dk), F32).at[IDX_DIM:IDX_DIM + GLA_GATE_RANK].set(w_alpha_up[0]).astype(BF16)
    wa_bf = w_branch_a[0].astype(BF16)
    wb_bf = w_branch_b[0].astype(BF16)
    wo_bf = w_out[0].astype(BF16)
    wup_bf = w_up[0].astype(BF16)
    wdn_bf = w_down[0].astype(BF16)

    front = jnp.zeros((B, ROW0, D), F32).at[0, :DB].set(x_sample[:, 0])
    meta = jnp.broadcast_to(meta_tokens[None].astype(F32), (B, N_META, D))
    x_all = jnp.concatenate([front, meta, x_prompt], axis=1).reshape(R, D)
    rpos = jnp.maximum(jnp.arange(TP, dtype=jnp.int32) - ROW0, 0)
    pos = jnp.concatenate([rpos.at[:DB].set(past), rpos])
    tabs = jnp.concatenate(_rope_tables(pos, hd, 1) + _rope_tables(pos, IDX_DIM, LANE // IDX_DIM), axis=1)

    tm1 = TP // 2
    bpb = TP // tm1
    P = pl.pallas_call(
        functools.partial(_inproj_kernel, n_a=n_a, n_b=n_b, n_c=n_c,
                          b128=(0, (VB - QB) // COL_TILE),
                          b64=((QI - QB) // COL_TILE, (QI - QB + IDX_HEADS * IDX_DIM) // COL_TILE)),
        grid=(R // tm1, n_ct),
        in_specs=[pl.BlockSpec((tm1, D), lambda i, j: (i, 0)),
                  pl.BlockSpec((1, D), lambda i, j: (0, 0)),
                  pl.BlockSpec((D, COL_TILE), lambda i, j: (0, jnp.minimum(j, n_a - 1))),
                  pl.BlockSpec((D, COL_TILE), lambda i, j: (0, jnp.clip(j - n_a, 0, n_b - 1))),
                  pl.BlockSpec((D, COL_TILE), lambda i, j: (0, jnp.clip(j - n_a - n_b, 0, n_c - 1))),
                  pl.BlockSpec((D, COL_TILE), lambda i, j: (0, 0)),
                  pl.BlockSpec((tm1, 6 * LANE), lambda i, j: (jnp.where(i < bpb, i, bpb + i % bpb), 0))],
        out_specs=pl.BlockSpec((tm1, COL_TILE), lambda i, j: (i, j)),
        out_shape=jax.ShapeDtypeStruct((R, NW), F32),
        scratch_shapes=[pltpu.VMEM((tm1, D), BF16)],
        compiler_params=_cparams(("parallel", "arbitrary")),
        name="inproj",
    )(x_all, norm_mix_g, wg_a, wg_b, wg_c, wg_s, tabs)

    C = GLA_CHUNK
    ncb = TP // C
    ba2 = b_alpha.reshape(1, H * dk)
    gain2 = gla_norm_g.reshape(1, H * dv)
    smc = SM // LANE
    a_out, s_fin = pl.pallas_call(
        functools.partial(_gla_prompt_kernel, dk=dk, dv=dv),
        grid=(B, ncb),
        in_specs=[pl.BlockSpec((C, H * dk), lambda b, c: (b * ncb + c, QA // (H * dk))),
                  pl.BlockSpec((C, H * dk), lambda b, c: (b * ncb + c, KA // (H * dk))),
                  pl.BlockSpec((C, H * dv), lambda b, c: (b * ncb + c, VA // (H * dv))),
                  pl.BlockSpec((C, H * dv), lambda b, c: (b * ncb + c, RA // (H * dv))),
                  pl.BlockSpec((C, LANE), lambda b, c: (b * ncb + c, smc)),
                  pl.BlockSpec((LANE, H * dk), lambda b, c: (0, 0)),
                  pl.BlockSpec((1, H * dk), lambda b, c: (0, 0)),
                  pl.BlockSpec((1, H * dv), lambda b, c: (0, 0))],
        out_specs=[pl.BlockSpec((C, H * dv), lambda b, c: (b * ncb + c, 0)),
                   pl.BlockSpec((None, H, dk, dv), lambda b, c: (b, 0, 0, 0))],
        out_shape=[jax.ShapeDtypeStruct((R, H * dv), BF16),
                   jax.ShapeDtypeStruct((B, H, dk, dv), F32)],
        scratch_shapes=[pltpu.VMEM((H, dk, dv), F32)],
        compiler_params=_cparams(("parallel", "arbitrary")),
        name="gla_prompt",
    )(P, P, P, P, P, wau_pad, ba2, gain2)

    p_s = P[:DB]
    P_rows = p_s.reshape(DB, 1, NW)
    a_out, s_new_s = pl.pallas_call(
        functools.partial(_gla_sample_kernel, dk=dk, dv=dv),
        grid=(DB,),
        in_specs=[pl.BlockSpec(memory_space=pl.ANY),
                  pl.BlockSpec((None, 1, H * dk), lambda d: (d, 0, QA // (H * dk))),
                  pl.BlockSpec((None, 1, H * dk), lambda d: (d, 0, KA // (H * dk))),
                  pl.BlockSpec((None, 1, H * dv), lambda d: (d, 0, VA // (H * dv))),
                  pl.BlockSpec((None, 1, H * dv), lambda d: (d, 0, RA // (H * dv))),
                  pl.BlockSpec((None, 1, LANE), lambda d: (d, 0, smc)),
                  pl.BlockSpec((LANE, H * dk), lambda d: (0, 0)),
                  pl.BlockSpec((1, H * dk), lambda d: (0, 0)),
                  pl.BlockSpec((1, H * dv), lambda d: (0, 0)),
                  pl.BlockSpec((None, H, dk, dv), lambda d: (d, 0, 0, 0))],
        out_specs=[pl.BlockSpec((DB, H * dv), lambda d: (0, 0)),
                   pl.BlockSpec((None, H, dk, dv), lambda d: (d, 0, 0, 0))],
        out_shape=[jax.ShapeDtypeStruct((R, H * dv), BF16),
                   jax.ShapeDtypeStruct((DB, H, dk, dv), F32)],
        scratch_shapes=[pltpu.VMEM((DB, H * dv), F32)],
        input_output_aliases={0: 0},
        compiler_params=_cparams(("arbitrary",)),
        name="gla_sample",
    )(a_out, P_rows, P_rows, P_rows, P_rows, P_rows, wau_pad, ba2, gain2, state_gla[0])

    TQ = LANE
    nqb = TP // TQ
    n_cls = -(-nqb // QB_PER_CLASS)
    widths = tuple(min((c + 1) * QB_PER_CLASS * TQ, TP) for c in range(n_cls))
    o_b = jnp.zeros((R, ATT_HEADS * hd), BF16)
    for c, width in enumerate(widths):
        q_lo = c * QB_PER_CLASS
        n_q = min(QB_PER_CLASS, nqb - q_lo)
        row_blk = lambda b, q, q_lo=q_lo: b * nqb + q_lo + q
        specs = [pl.BlockSpec((TQ, IDX_HEADS * IDX_DIM), lambda b, q, r_=row_blk: (r_(b, q), QI // (IDX_HEADS * IDX_DIM))),
                 pl.BlockSpec((TQ, LANE), lambda b, q, r_=row_blk: (r_(b, q), smc)),
                 pl.BlockSpec((TP, LANE), lambda b, q: (b, smc)),
                 pl.BlockSpec((TQ, ATT_HEADS * hd), lambda b, q, r_=row_blk: (r_(b, q), QB // (ATT_HEADS * hd))),
                 pl.BlockSpec((TP, kvw), lambda b, q: (b, KB // kvw)),
                 pl.BlockSpec((TP, kvw), lambda b, q: (b, VB // kvw))]
        o_b = pl.pallas_call(
            functools.partial(_dsa_prompt_kernel, k_sel=k_sel_p, hd=hd, width=width, q_lo=q_lo),
            grid=(B, n_q),
            in_specs=[pl.BlockSpec(memory_space=pl.ANY)] + specs,
            out_specs=pl.BlockSpec((TQ, ATT_HEADS * hd), lambda b, q, r_=row_blk: (r_(b, q), 0)),
            out_shape=jax.ShapeDtypeStruct((R, ATT_HEADS * hd), BF16),
            scratch_shapes=[pltpu.VMEM((width, LANE), BF16), pltpu.VMEM((width, LANE), BF16),
                            pltpu.VMEM((width, kvw), BF16), pltpu.VMEM((width, kvw), BF16),
                            pltpu.VMEM((TQ, width), F32), pltpu.VMEM((TQ, width), F32)],
            input_output_aliases={0: 0},
            compiler_params=_cparams(("parallel", "arbitrary")),
            name=f"dsa_prompt_w{width}",
        )(o_b, P, P, P, P, P, P)

    qi_s = p_s[:, QI:QI + IDX_HEADS * IDX_DIM].reshape(DB, IDX_HEADS, IDX_DIM)
    wi_s = p_s[:, SM + WI_OFF:SM + WI_OFF + IDX_HEADS].reshape(DB, IDX_HEADS, 1)
    ki_s = p_s[:, SM:SM + IDX_DIM].reshape(DB, 1, IDX_DIM)
    sc, scn = pl.pallas_call(
        functools.partial(_dsa_sample_score_kernel, n_pages=n_pages),
        grid_spec=pltpu.PrefetchScalarGridSpec(
            num_scalar_prefetch=1,
            grid=(DB,),
            in_specs=[pl.BlockSpec(memory_space=pl.ANY),
                      pl.BlockSpec((None, IDX_HEADS, IDX_DIM), lambda d, pt: (d, 0, 0)),
                      pl.BlockSpec((None, IDX_HEADS, 1), lambda d, pt: (d, 0, 0)),
                      pl.BlockSpec((None, 1, IDX_DIM), lambda d, pt: (d, 0, 0))],
            out_specs=[pl.BlockSpec((None, n_pages, PAGE_SIZE), lambda d, pt: (d, 0, 0)),
                       pl.BlockSpec((None, 1, LANE), lambda d, pt: (d, 0, 0))],
            scratch_shapes=[pltpu.VMEM((2, n_pages, IDX_DIM, PAGE_SIZE), F32),
                            pltpu.SemaphoreType.DMA((2,))]),
        out_shape=[jax.ShapeDtypeStruct((DB, n_pages, PAGE_SIZE), F32),
                   jax.ShapeDtypeStruct((DB, 1, LANE), F32)],
        compiler_params=_cparams(("arbitrary",)),
        name="dsa_sample_score",
    )(page_table, jnp.swapaxes(cache_idx_k[0], 1, 2), qi_s, wi_s, ki_s)

    hl_np = np.zeros((past + LANE, LANE), np.float32)
    hl_np[:, 0] = np.arange(past + LANE) // PAGE_SIZE
    hl_np[:, 1] = np.arange(past + LANE) % PAGE_SIZE
    hl = jnp.asarray(hl_np, dtype=BF16)
    sel = pl.pallas_call(
        functools.partial(_dsa_sample_select_kernel, k_sel=k_sel_s, chunk=2048),
        out_shape=jax.ShapeDtypeStruct((DB, k_sel_s, LANE), I32),
        scratch_shapes=[pltpu.VMEM((DB, 1, past + LANE), F32), pltpu.VMEM((DB, past + LANE), F32)],
        compiler_params=pltpu.CompilerParams(vmem_limit_bytes=VMEM_LIMIT),
        name="dsa_sample_select",
    )(sc.reshape(DB, past), scn.reshape(DB, LANE), hl)

    sel_s = sel[:, :, 0:2].reshape(DB, 2 * k_sel_s)
    q_s = p_s[:, QB:QB + ATT_HEADS * hd].reshape(DB, ATT_HEADS, hd)
    kn_s = p_s[:, KB:KB + kvw]
    vn_s = p_s[:, VB:VB + kvw]
    o_b = pl.pallas_call(
        functools.partial(_dsa_sample_attend_kernel, k_sel=k_sel_s, n_pages=n_pages, hd=hd),
        grid_spec=pltpu.PrefetchScalarGridSpec(
            num_scalar_prefetch=2,
            grid=(DB,),
            in_specs=[pl.BlockSpec(memory_space=pl.ANY),
                      pl.BlockSpec(memory_space=pl.ANY),
                      pl.BlockSpec(memory_space=pl.ANY),
                      pl.BlockSpec((None, k_sel_s, LANE), lambda d, s_, pt: (d, 0, 0)),
                      pl.BlockSpec((None, ATT_HEADS, hd), lambda d, s_, pt: (d, 0, 0)),
                      pl.BlockSpec((None, 1, kvw), lambda d, s_, pt: (d, 0, KB // kvw)),
                      pl.BlockSpec((None, 1, kvw), lambda d, s_, pt: (d, 0, VB // kvw))],
            out_specs=pl.BlockSpec((DB, ATT_HEADS * hd), lambda d, s_, pt: (0, 0)),
            scratch_shapes=[pltpu.VMEM((2, k_sel_s, ATT_KV_HEADS, hd), F32),
                            pltpu.VMEM((2, k_sel_s, ATT_KV_HEADS, hd), F32),
                            pltpu.VMEM((DB, ATT_HEADS, hd), F32), pltpu.SemaphoreType.DMA((2, 2))]),
        out_shape=jax.ShapeDtypeStruct((R, ATT_HEADS * hd), BF16),
        input_output_aliases={2: 0},
        compiler_params=_cparams(("arbitrary",)),
        name="dsa_sample_attend",
    )(sel_s, page_table, o_b, cache_k[0], cache_v[0], sel, q_s, P_rows, P_rows)

    mix = pl.pallas_call(
        _mix_kernel,
        grid=(R // tm1, D // COL_TILE),
        in_specs=[pl.BlockSpec((tm1, H * dv), lambda i, j: (i, 0)),
                  pl.BlockSpec((tm1, ATT_HEADS * hd), lambda i, j: (i, 0)),
                  pl.BlockSpec((H * dv, COL_TILE), lambda i, j: (0, j)),
                  pl.BlockSpec((ATT_HEADS * hd, COL_TILE), lambda i, j: (0, j)),
                  pl.BlockSpec((tm1, COL_TILE), lambda i, j: (i, GA // COL_TILE + j)),
                  pl.BlockSpec((tm1, COL_TILE), lambda i, j: (i, GB // COL_TILE + j))],
        out_specs=pl.BlockSpec((tm1, COL_TILE), lambda i, j: (i, j)),
        out_shape=jax.ShapeDtypeStruct((R, D), BF16),
        compiler_params=_cparams(("parallel", "arbitrary")),
        name="mix",
    )(a_out, o_b, wa_bf, wb_bf, P, P)

    tm2 = TP // 4
    x1, h2 = pl.pallas_call(
        _outproj_kernel,
        grid=(R // tm2,),
        in_specs=[pl.BlockSpec((tm2, D), lambda i: (i, 0)),
                  pl.BlockSpec((tm2, D), lambda i: (i, 0)),
                  pl.BlockSpec((D, D), lambda i: (0, 0)),
                  pl.BlockSpec((1, D), lambda i: (0, 0))],
        out_specs=[pl.BlockSpec((tm2, D), lambda i: (i, 0)),
                   pl.BlockSpec((tm2, D), lambda i: (i, 0))],
        out_shape=[jax.ShapeDtypeStruct((R, D), F32), jax.ShapeDtypeStruct((R, D), BF16)],
        compiler_params=_cparams(("parallel",)),
        name="outproj",
    )(x_all, mix, wo_bf, norm_ffn_g)

    nrb = R // tm2
    nft = dff // COL_TILE
    y_all, a_head, a_tail = pl.pallas_call(
        functools.partial(_ffn_kernel, n_dec=DB),
        grid=(nrb, nft),
        in_specs=[pl.BlockSpec((tm2, D), lambda i, j: (i, 0)),
                  pl.BlockSpec((16, D), lambda i, j: (jnp.maximum(i * (tm2 // 16) - 1, 0), 0)),
                  pl.BlockSpec((tm2, D), lambda i, j: (i, 0)),
                  pl.BlockSpec((D, COL_TILE), lambda i, j: (0, j)),
                  pl.BlockSpec((D, COL_TILE), lambda i, j: (0, nft + j)),
                  pl.BlockSpec((COL_TILE, D), lambda i, j: (j, 0)),
                  pl.BlockSpec((CONV_W, COL_TILE), lambda i, j: (0, j)),
                  pl.BlockSpec((1, COL_TILE), lambda i, j: (0, j)),
                  pl.BlockSpec((DB, COL_TILE), lambda i, j: (0, j)),
                  pl.BlockSpec((DB, COL_TILE), lambda i, j: (0, j)),
                  pl.BlockSpec((1, D), lambda i, j: (0, 0))],
        out_specs=[pl.BlockSpec((tm2, D), lambda i, j: (i, 0)),
                   pl.BlockSpec((DB, COL_TILE), lambda i, j: (i, j)),
                   pl.BlockSpec((8, COL_TILE), lambda i, j: (i, j))],
        out_shape=[jax.ShapeDtypeStruct((R, D), F32),
                   jax.ShapeDtypeStruct((nrb * DB, dff), F32),
                   jax.ShapeDtypeStruct((nrb * 8, dff), F32)],
        scratch_shapes=[pltpu.VMEM((tm2, D), F32)],
        compiler_params=_cparams(("parallel", "arbitrary")),
        name="ffn",
    )(h2, h2, x1, wup_bf, wup_bf, wdn_bf, conv_w[0], conv_b, state_conv[0, :, 0], state_conv[0, :, 1],
      norm_final_g.reshape(1, D))

    y3 = y_all.reshape(B, TP, D)
    y_prompt = y3[:, FRONT:]
    y_sample = y_all[:DB].reshape(DB, 1, D)
    P3 = P.reshape(B, TP, NW)
    new_k_p = P3[:, ROW0:, KB:KB + kvw].reshape(1, B, T, ATT_KV_HEADS, hd)
    new_v_p = P3[:, ROW0:, VB:VB + kvw].reshape(1, B, T, ATT_KV_HEADS, hd)
    new_ki_p = P3[:, ROW0:, SM:SM + IDX_DIM].reshape(1, B, T, IDX_DIM)
    new_gla_p = s_fin[None]
    tails = a_tail.reshape(B, nrb // B, 8, dff)[:, -1, 8 - (CONV_W - 1):, :]
    new_conv_p = tails[None]
    new_k_s = kn_s.reshape(1, DB, 1, ATT_KV_HEADS, hd)
    new_v_s = vn_s.reshape(1, DB, 1, ATT_KV_HEADS, hd)
    new_ki_s = ki_s.reshape(1, DB, 1, IDX_DIM)
    new_gla_s = s_new_s[None]
    new_conv_s = jnp.stack([state_conv[0, :, 1], a_head[:DB]], axis=1)[None]
    return (y_prompt, y_sample, new_k_p, new_v_p, new_ki_p, new_gla_p, new_conv_p,
            new_k_s, new_v_s, new_ki_s, new_gla_s, new_conv_s)
```

```python
import functools
import math

import jax
import jax.numpy as jnp
import numpy as np
from jax import lax
from jax.experimental import pallas as pl
from jax.experimental.pallas import tpu as pltpu

F32 = jnp.float32
BF16 = jnp.bfloat16
I32 = jnp.int32

N_META = 16
GLA_HEADS = 4
GLA_GATE_RANK = 16
GLA_GATE_NORM = 16.0
GLA_CHUNK = 64
GLA_SUB = 8
ATT_HEADS = 16
ATT_KV_HEADS = 4
IDX_HEADS = 16
IDX_DIM = 64
TOPK_MAX = 256
ROPE_THETA = 500000.0
CONV_W = 3
EPS = 1e-6
PAGE_SIZE = 128
WI_OFF = IDX_DIM + GLA_GATE_RANK

LANE = 128
FRONT = 128
ROW0 = FRONT - N_META
COL_TILE = 512
QB_PER_CLASS = 2
VMEM_LIMIT = 56 * 1024 * 1024
INT_MIN = -2 ** 31
LOG2_E = 1.4426950408889634


def _cparams(sem):
    return pltpu.CompilerParams(dimension_semantics=sem, vmem_limit_bytes=VMEM_LIMIT)


def _sigmoid(x):
    return 1.0 / (1.0 + jnp.exp(-x))


def _pattern_value(p):
    return pltpu.bitcast(jnp.where(p < 0, p ^ jnp.int32(0x7FFFFFFF), p), F32)


def _kth_largest(load_scores, rows, k_sel):
    def body(it, t):
        step = jnp.left_shift(jnp.int32(1), 30 - 2 * it)
        scores = load_scores()
        adv = jnp.zeros((rows, 1), I32)
        for m in (1, 2, 3):
            cnt = jnp.sum((scores >= _pattern_value(t + m * step)).astype(I32), axis=1, keepdims=True)
            adv = adv + (cnt >= k_sel).astype(I32)
        return t + adv * step

    t = lax.fori_loop(0, 16, body, jnp.full((rows, 1), INT_MIN, I32))
    return jnp.where(t == INT_MIN, -jnp.inf, _pattern_value(t))


def _rope_block(xb, c, sa, sb, half):
    return xb * c + pltpu.roll(xb, LANE - half, 1) * sa + pltpu.roll(xb, half, 1) * sb


def _inproj_kernel(x_ref, g_ref, wa_ref, wb_ref, wc_ref, ws_ref, tab_ref, o_ref, h_ref,
                   *, n_a, n_b, n_c, b128, b64):
    j = pl.program_id(1)

    @pl.when(j == 0)
    def _():
        x = x_ref[...]
        ms = jnp.mean(x * x, axis=-1, keepdims=True)
        h_ref[...] = (x * lax.rsqrt(ms + EPS) * g_ref[...]).astype(BF16)

    def proj(w_ref):
        return jnp.dot(h_ref[...], w_ref[...], preferred_element_type=F32)

    @pl.when(j < n_a)
    def _():
        o_ref[...] = proj(wa_ref)

    @pl.when(jnp.logical_and(j >= n_a, j < n_a + n_b))
    def _():
        acc = proj(wb_ref)
        jb = j - n_a
        is128 = jnp.logical_and(jb >= b128[0], jb < b128[1])
        is64 = jnp.logical_and(jb >= b64[0], jb < b64[1])

        @pl.when(is128)
        def _():
            c, sa, sb = tab_ref[:, 0:128], tab_ref[:, 128:256], tab_ref[:, 256:384]
            for blk in range(COL_TILE // LANE):
                sl = slice(blk * LANE, (blk + 1) * LANE)
                o_ref[:, sl] = _rope_block(acc[:, sl], c, sa, sb, 16)

        @pl.when(is64)
        def _():
            c, sa, sb = tab_ref[:, 384:512], tab_ref[:, 512:640], tab_ref[:, 640:768]
            for blk in range(COL_TILE // LANE):
                sl = slice(blk * LANE, (blk + 1) * LANE)
                o_ref[:, sl] = _rope_block(acc[:, sl], c, sa, sb, 8)

        @pl.when(jnp.logical_not(is128 | is64))
        def _():
            o_ref[...] = acc

    @pl.when(jnp.logical_and(j >= n_a + n_b, j < n_a + n_b + n_c))
    def _():
        o_ref[...] = proj(wc_ref)

    @pl.when(j == n_a + n_b + n_c)
    def _():
        acc = proj(ws_ref)
        lane = lax.broadcasted_iota(I32, (1, LANE), 1)
        first = lane < IDX_DIM
        c = jnp.where(first, tab_ref[:, 384:512], 1.0)
        sa = jnp.where(first, tab_ref[:, 512:640], 0.0)
        sb = jnp.where(first, tab_ref[:, 640:768], 0.0)
        o_ref[:, 0:LANE] = _rope_block(acc[:, 0:LANE], c, sa, sb, 8)
        o_ref[:, LANE:] = acc[:, LANE:]


def _log_alpha(sm, wau_ref, ba_ref):
    x = jnp.dot(sm.astype(BF16), wau_ref[...], preferred_element_type=F32) + ba_ref[...]
    return (jnp.minimum(x, 0.0) - jnp.log(1.0 + jnp.exp(-jnp.abs(x)))) * (1.0 / GLA_GATE_NORM)


def _row_to_col(row, n):
    eye = lax.broadcasted_iota(I32, (n, n), 0) == lax.broadcasted_iota(I32, (n, n), 1)
    return jnp.sum(jnp.where(eye, row, 0.0), axis=1, keepdims=True)


def _readout(o, r, gain):
    ms = jnp.mean(o * o, axis=-1, keepdims=True)
    return o * lax.rsqrt(ms + EPS) * gain * (r * _sigmoid(r))


def _gla_chunk(q, k, v, g, S):
    C, dk = q.shape
    vb = v.astype(BF16)
    row = lax.broadcasted_iota(I32, g.shape, 0)
    b = g
    sh = 1
    while sh < C:
        b = b + jnp.where(row >= sh, pltpu.roll(b, sh, 0), 0.0)
        sh *= 2
    bl = b[C - 1:C, :]
    o = jnp.dot((q * jnp.exp(b)).astype(BF16), S.astype(BF16), preferred_element_type=F32)

    tcol = lax.broadcasted_iota(I32, (GLA_SUB, C), 1)
    trow = lax.broadcasted_iota(I32, (GLA_SUB, C), 0)
    a_rows = []
    kf = None
    prev_ref = None
    for blk in range(C // GLA_SUB):
        r0 = blk * GLA_SUB
        ref = b[r0 - 1:r0, :] if blk > 0 else jnp.zeros_like(bl)
        b_i = b[r0:r0 + GLA_SUB, :]
        q_i = q[r0:r0 + GLA_SUB, :]
        if blk == 0:
            a_off = jnp.zeros((GLA_SUB, C), F32)
        else:
            fresh = k[r0 - GLA_SUB:r0, :] * jnp.exp(ref - b[r0 - GLA_SUB:r0, :])
            kf = fresh if blk == 1 else jnp.concatenate([kf * jnp.exp(ref - prev_ref), fresh], axis=0)
            kf_full = jnp.concatenate([kf, jnp.zeros((C - r0, dk), F32)], axis=0)
            qe = q_i * jnp.exp(b_i - ref)
            a_off = lax.dot_general(qe.astype(BF16), kf_full.astype(BF16), (((1,), (1,)), ((), ())),
                                    preferred_element_type=F32)
        prev_ref = ref
        diag = jnp.zeros((GLA_SUB, C), F32)
        for sl in range(GLA_SUB):
            s = r0 + sl
            w = q_i * k[s:s + 1, :] * jnp.exp(jnp.minimum(b_i - b[s:s + 1, :], 0.0))
            diag = jnp.where(tcol == s, jnp.sum(w, axis=1, keepdims=True), diag)
        a_rows.append(jnp.where(tcol < r0, a_off, jnp.where(tcol <= trow + r0, diag, 0.0)))
    a = jnp.concatenate(a_rows, axis=0)
    o = o + jnp.dot(a.astype(BF16), vb, preferred_element_type=F32)

    kd = k * jnp.exp(bl - b)
    upd = lax.dot_general(kd.astype(BF16), vb, (((0,), (0,)), ((), ())),
                          preferred_element_type=F32)
    return o, S * _row_to_col(jnp.exp(bl), dk) + upd


def _gla_prompt_kernel(q_ref, k_ref, v_ref, r_ref, sm_ref, wau_ref, ba_ref, gain_ref,
                       o_ref, sfin_ref, s_ref, *, dk, dv):
    c = pl.program_id(1)
    H = s_ref.shape[0]

    @pl.when(c == 0)
    def _():
        s_ref[...] = jnp.zeros_like(s_ref)
        o_ref[...] = jnp.zeros_like(o_ref)

    @pl.when(c > 0)
    def _():
        g_all = _log_alpha(sm_ref[...], wau_ref, ba_ref)
        for h in range(H):
            ks, vs = slice(h * dk, (h + 1) * dk), slice(h * dv, (h + 1) * dv)
            o, s_new = _gla_chunk(q_ref[:, ks] * (dk ** -0.5), k_ref[:, ks], v_ref[:, vs],
                                  g_all[:, ks], s_ref[h])
            s_ref[h] = s_new
            o_ref[:, vs] = _readout(o, r_ref[:, vs], gain_ref[:, vs]).astype(BF16)

    @pl.when(c == pl.num_programs(1) - 1)
    def _():
        sfin_ref[...] = s_ref[...]


def _gla_sample_kernel(alias_ref, q_ref, k_ref, v_ref, r_ref, sm_ref, wau_ref, ba_ref, gain_ref,
                       st_ref, o_ref, snew_ref, acc_ref, *, dk, dv):
    del alias_ref
    d = pl.program_id(0)
    H = st_ref.shape[0]
    g_all = _log_alpha(jnp.broadcast_to(sm_ref[...], (8, LANE)), wau_ref, ba_ref)[0:1, :]
    ons = []
    for h in range(H):
        ks, vs = slice(h * dk, (h + 1) * dk), slice(h * dv, (h + 1) * dv)
        q = q_ref[:, ks] * (dk ** -0.5)
        k = k_ref[:, ks]
        v = v_ref[:, vs]
        eg = jnp.exp(g_all[:, ks])
        S = st_ref[h]
        qe = jnp.broadcast_to(q * eg, (8, dk))
        o = jnp.dot(qe.astype(BF16), S.astype(BF16), preferred_element_type=F32)[0:1, :]
        o = o + jnp.sum(q * k, axis=1, keepdims=True) * v
        snew_ref[h] = S * _row_to_col(eg, dk) + _row_to_col(k, dk) * v
        ons.append(_readout(o, r_ref[:, vs], gain_ref[:, vs]))
    on = jnp.concatenate(ons, axis=1)
    row = lax.broadcasted_iota(I32, acc_ref.shape, 0)

    @pl.when(d == 0)
    def _():
        acc_ref[...] = jnp.zeros_like(acc_ref)

    acc_ref[...] = jnp.where(row == d, on, acc_ref[...])

    @pl.when(d == pl.num_programs(0) - 1)
    def _():
        o_ref[...] = acc_ref[...].astype(BF16)


def _keep_lowest_ties(key_ref, bias_ref, thr, TK, k_sel):
    sc = key_ref[:, 0:TK]
    gt = sc > thr
    eq = jnp.logical_and(sc == thr, sc > -jnp.inf)
    room = k_sel - jnp.sum(jnp.where(gt, 1.0, 0.0), axis=1, keepdims=True)
    upper = jnp.where(lax.broadcasted_iota(I32, (LANE, LANE), 0) < lax.broadcasted_iota(I32, (LANE, LANE), 1),
                      1.0, 0.0).astype(BF16)
    run = jnp.zeros_like(room)
    for c in range(TK // LANE):
        blk = slice(c * LANE, (c + 1) * LANE)
        e = jnp.where(eq[:, blk], 1.0, 0.0)
        before = jnp.dot(e.astype(BF16), upper, preferred_element_type=F32) + run
        keep = jnp.logical_or(gt[:, blk], jnp.logical_and(eq[:, blk], before < room))
        bias_ref[:, blk] = jnp.where(keep, 0.0, -jnp.inf)
        run = run + jnp.sum(e, axis=1, keepdims=True)


def _dsa_prompt_kernel(alias_ref, qi_ref, smq_ref, smk_ref, qb_ref, kb_ref, vb_ref, o_ref,
                       ka_ref, kbb_ref, kbf_ref, vbf_ref, key_ref, bias_ref, *, k_sel, hd, width, q_lo):
    del alias_ref
    qb = q_lo + pl.program_id(1)
    TQ = qi_ref.shape[0]
    nt = (((1,), (1,)), ((), ()))
    G = ATT_HEADS // ATT_KV_HEADS

    @pl.when(pl.program_id(1) == 0)
    def _():
        smk = smk_ref[0:width, :]
        lane = lax.broadcasted_iota(I32, (1, LANE), 1)
        ka_ref[...] = jnp.where(lane < IDX_DIM, smk, 0.0).astype(BF16)
        kbb_ref[...] = jnp.where(lane >= IDX_DIM, pltpu.roll(smk, IDX_DIM, 1), 0.0).astype(BF16)
        kbf_ref[...] = kb_ref[0:width, :].astype(BF16)
        vbf_ref[...] = vb_ref[0:width, :].astype(BF16)

    def body(TK):
        score = jnp.zeros((TQ, TK), F32)
        for p in range(IDX_HEADS // 2):
            qp = qi_ref[:, p * LANE:(p + 1) * LANE].astype(BF16)
            for half, kref in ((0, ka_ref), (1, kbb_ref)):
                h = 2 * p + half
                s = lax.dot_general(qp, kref[0:TK, :], nt, preferred_element_type=F32)
                w = smq_ref[:, WI_OFF + h:WI_OFF + h + 1] * (IDX_HEADS ** -0.5 * IDX_DIM ** -0.5)
                score = score + w * jnp.maximum(s, 0.0)

        qrow = qb * TQ + lax.broadcasted_iota(I32, (TQ, 1), 0)
        kcol = lax.broadcasted_iota(I32, (1, TK), 1)
        adm = jnp.logical_and(kcol <= qrow, kcol >= ROW0)
        key_ref[:, 0:TK] = jnp.where(adm, score, -jnp.inf)
        thr = _kth_largest(lambda: key_ref[:, 0:TK], TQ, k_sel)
        sc = key_ref[:, 0:TK]
        ge = jnp.logical_and(sc >= thr, sc > -jnp.inf)
        bias_ref[:, 0:TK] = jnp.where(ge, 0.0, -jnp.inf)
        n_ge = jnp.sum(jnp.where(ge, 1.0, 0.0), axis=1, keepdims=True)

        @pl.when(jnp.max(n_ge) > k_sel)
        def _():
            _keep_lowest_ties(key_ref, bias_ref, thr, TK, k_sel)

        def qk(n):
            q4 = jnp.concatenate(
                [(qb_ref[:, (n * G + gq) * hd:(n * G + gq + 1) * hd] * (hd ** -0.5 * LOG2_E)).astype(BF16)
                 for gq in range(G)], axis=0)
            return lax.dot_general(q4, kbf_ref[0:TK, n * hd:(n + 1) * hd], nt, preferred_element_type=F32)

        s_next = qk(0)
        for n in range(ATT_KV_HEADS):
            s = s_next
            if n + 1 < ATT_KV_HEADS:
                s_next = qk(n + 1)
            s = (s.reshape(G, TQ, TK) + bias_ref[:, 0:TK][None]).reshape(G * TQ, TK)
            m = jnp.max(s, axis=1, keepdims=True)
            m = jnp.where(m == -jnp.inf, 0.0, m)
            p = jnp.exp2(s - m)
            l = jnp.sum(p, axis=1, keepdims=True)
            o = jnp.dot(p.astype(BF16), vbf_ref[0:TK, n * hd:(n + 1) * hd], preferred_element_type=F32)
            o = jnp.where(l > 0.0, o / l, 0.0)
            for gq in range(G):
                o_ref[:, (n * G + gq) * hd:(n * G + gq + 1) * hd] = o[gq * TQ:(gq + 1) * TQ, :].astype(BF16)

    body(width)


def _idx_page_copy(cache_ref, buf_ref, sem, page, slot, p):
    return pltpu.make_async_copy(cache_ref.at[page], buf_ref.at[slot, p], sem.at[slot])


def _dsa_sample_score_kernel(pt_ref, cache_ref, qi_ref, wi_ref, kin_ref, sc_ref, scn_ref,
                             buf_ref, sem, *, n_pages):
    d = pl.program_id(0)
    slot = d % 2

    def fetch(dd, sl):
        def start(p, carry):
            _idx_page_copy(cache_ref, buf_ref, sem, pt_ref[dd, p], sl, p).start()
            return carry
        lax.fori_loop(0, n_pages, start, 0, unroll=8)

    @pl.when(d == 0)
    def _():
        fetch(0, 0)

    @pl.when(d + 1 < pl.num_programs(0))
    def _():
        fetch(d + 1, 1 - slot)

    def wait(p, carry):
        _idx_page_copy(cache_ref, buf_ref, sem, 0, slot, p).wait()
        return carry

    lax.fori_loop(0, n_pages, wait, 0, unroll=8)
    nt = (((1,), (1,)), ((), ()))
    qi = qi_ref[...].astype(BF16)
    w = wi_ref[...] * (IDX_HEADS ** -0.5 * IDX_DIM ** -0.5)
    kp = buf_ref[slot].astype(BF16)
    qib = jnp.broadcast_to(qi[None], (n_pages, IDX_HEADS, IDX_DIM))
    s = lax.dot_general(qib, kp, (((2,), (1,)), ((0,), (0,))), preferred_element_type=F32)
    sc_ref[...] = jnp.sum(w[None] * jnp.maximum(s, 0.0), axis=1)
    kn = jnp.broadcast_to(kin_ref[...], (8, IDX_DIM)).astype(BF16)
    sn = lax.dot_general(qi, kn, nt, preferred_element_type=F32)[:, 0:1]
    scn_ref[...] = jnp.broadcast_to(jnp.sum(w * jnp.maximum(sn, 0.0), axis=0, keepdims=True), (1, LANE))


def _dsa_sample_select_kernel(sc_ref, scn_ref, hl_ref, ptf_ref, sel_ref, pos_ref, key_ref, *, k_sel, chunk):
    DB, NP = sc_ref.shape
    nblk = NP // LANE
    lane1 = lax.broadcasted_iota(I32, (1, LANE), 1)
    key_ref[:, 0:NP] = sc_ref[...]
    key_ref[:, NP:] = jnp.where(lane1 == 0, scn_ref[...], -jnp.inf)
    thr = _kth_largest(lambda: key_ref[...], DB, k_sel)
    keys = key_ref[...]
    gt = keys > thr
    eq = jnp.logical_and(keys == thr, keys > -jnp.inf)
    n_gt = jnp.sum(gt.astype(I32), axis=1, keepdims=True).astype(F32)

    iu = lax.broadcasted_iota(I32, (LANE, LANE), 0)
    ju = lax.broadcasted_iota(I32, (LANE, LANE), 1)
    upper = jnp.where(iu < ju, 1.0, 0.0).astype(BF16)
    ones = jnp.ones((LANE, LANE), BF16)

    def excl_prefix(mask):
        mb = jnp.where(mask, 1.0, 0.0).astype(BF16)
        stacked = jnp.concatenate([mb[:, c * LANE:(c + 1) * LANE] for c in range(nblk + 1)], axis=0)
        within = jnp.dot(stacked, upper, preferred_element_type=F32)
        tot = jnp.dot(stacked, ones, preferred_element_type=F32)[:, 0:1]
        outs = []
        run = jnp.zeros((DB, 1), F32)
        for c in range(nblk + 1):
            outs.append(within[c * DB:(c + 1) * DB, :] + run)
            run = run + tot[c * DB:(c + 1) * DB, :]
        return jnp.concatenate(outs, axis=1)

    pos_gt = excl_prefix(gt)
    pos_eq = excl_prefix(eq) + n_gt
    keep_eq = jnp.logical_and(eq, pos_eq < k_sel)
    pos = jnp.where(gt, pos_gt, jnp.where(keep_eq, pos_eq, -1.0))
    for d in range(DB):
        pos_ref[d] = pos[d:d + 1, :]

    jrow = lax.broadcasted_iota(I32, (k_sel, 1), 0).astype(F32)
    lane_k = lax.broadcasted_iota(I32, (k_sel, LANE), 1)
    width = NP + LANE

    def compact(d, carry):
        acc = jnp.zeros((k_sel, LANE), F32)
        for c0 in range(0, width, chunk):
            c1 = min(c0 + chunk, width)
            e = jnp.where(pos_ref[d, :, c0:c1] == jrow, 1.0, 0.0).astype(BF16)
            acc = acc + jnp.dot(e, hl_ref[c0:c1, :], preferred_element_type=F32)
        onehot = lane_k.astype(F32) == acc[:, 0:1]
        phys = jnp.sum(jnp.where(onehot, ptf_ref[d], 0.0), axis=1, keepdims=True)
        acc = jnp.where(lane_k == 2, phys * PAGE_SIZE + acc[:, 1:2], acc)
        sel_ref[d] = acc.astype(I32)
        return carry

    lax.fori_loop(0, DB, compact, 0)


def _kv_copies(ck_ref, cv_ref, kbuf_ref, vbuf_ref, sem, row, buf, j):
    ck = pltpu.make_async_copy(ck_ref.at[row], kbuf_ref.at[buf, j], sem.at[0, buf])
    cv = pltpu.make_async_copy(cv_ref.at[row], vbuf_ref.at[buf, j], sem.at[1, buf])
    return ck, cv


def _dsa_sample_attend_kernel(rows_ref, alias_ref, ck_ref, cv_ref, selv_ref, q_ref,
                              kn_ref, vn_ref, o_ref, kbuf_ref, vbuf_ref, acc_ref, sem,
                              *, k_sel, n_pages, hd):
    del alias_ref
    d = pl.program_id(0)
    buf = d % 2

    def fetch(dd, bb):
        def start(j, carry):
            ck, cv = _kv_copies(ck_ref, cv_ref, kbuf_ref, vbuf_ref, sem, rows_ref[dd, j], bb, j)
            ck.start()
            cv.start(priority=1)
            return carry
        lax.fori_loop(0, k_sel, start, 0, unroll=8)

    @pl.when(d == 0)
    def _():
        fetch(0, 0)

    @pl.when(d + 1 < pl.num_programs(0))
    def _():
        fetch(d + 1, 1 - buf)

    def wait(j, carry):
        ck, cv = _kv_copies(ck_ref, cv_ref, kbuf_ref, vbuf_ref, sem, 0, buf, j)
        ck.wait()
        cv.wait()
        return carry

    lax.fori_loop(0, k_sel, wait, 0, unroll=8)
    is_new = selv_ref[:, 0:1] >= n_pages
    G = ATT_HEADS // ATT_KV_HEADS
    nt = (((1,), (1,)), ((), ()))
    q = q_ref[...] * (hd ** -0.5)
    outs = []
    for n in range(ATT_KV_HEADS):
        hs = slice(n * hd, (n + 1) * hd)
        kk = jnp.where(is_new, kn_ref[:, hs], kbuf_ref[buf, :, n, :]).astype(BF16)
        vv = jnp.where(is_new, vn_ref[:, hs], vbuf_ref[buf, :, n, :]).astype(BF16)
        qn = jnp.concatenate([q[n * G:(n + 1) * G, :], jnp.zeros((8 - G, hd), F32)], axis=0).astype(BF16)
        s = lax.dot_general(qn, kk, nt, preferred_element_type=F32)
        m = jnp.max(s, axis=1, keepdims=True)
        p = jnp.exp(s - m)
        l = jnp.sum(p, axis=1, keepdims=True)
        o = jnp.dot(p.astype(BF16), vv, preferred_element_type=F32) / l
        outs.append(o[0:G, :])
    acc_ref[pl.ds(d, 1)] = jnp.concatenate(outs, axis=0)[None]

    @pl.when(d == pl.num_programs(0) - 1)
    def _():
        for h in range(ATT_HEADS):
            o_ref[:, h * hd:(h + 1) * hd] = acc_ref[:, h, :].astype(BF16)


def _mix_kernel(a_ref, b_ref, wa_ref, wb_ref, ga_ref, gb_ref, o_ref):
    ba = jnp.dot(a_ref[...], wa_ref[...], preferred_element_type=F32)
    bb = jnp.dot(b_ref[...], wb_ref[...], preferred_element_type=F32)
    o_ref[...] = (_sigmoid(ga_ref[...]) * ba + _sigmoid(gb_ref[...]) * bb).astype(BF16)


def _outproj_kernel(x_ref, m_ref, w_ref, g_ref, x1_ref, h_ref):
    x1 = x_ref[...] + jnp.dot(m_ref[...], w_ref[...], preferred_element_type=F32)
    x1_ref[...] = x1
    ms = jnp.mean(x1 * x1, axis=-1, keepdims=True)
    h_ref[...] = (x1 * lax.rsqrt(ms + EPS) * g_ref[...]).astype(BF16)


def _ffn_kernel(h_ref, halo_ref, x1_ref, wa_ref, wb_ref, wd_ref, cw_ref, cb_ref, st0_ref, st1_ref,
                gf_ref, y_ref, head_ref, tail_ref, acc_ref, *, n_dec):
    i = pl.program_id(0)
    j = pl.program_id(1)

    @pl.when(j == 0)
    def _():
        acc_ref[...] = jnp.zeros_like(acc_ref)

    h = h_ref[...]
    ah = jnp.dot(halo_ref[...], wa_ref[...], preferred_element_type=F32)
    ah = jnp.where(i > 0, ah, 0.0)
    a = jnp.dot(h, wa_ref[...], preferred_element_type=F32)
    row = lax.broadcasted_iota(I32, a.shape, 0)
    hl = ah.shape[0]
    s1 = jnp.where(row == 0, ah[hl - 1:hl, :], pltpu.roll(a, 1, 0))
    s2 = jnp.where(row == 0, ah[hl - 2:hl - 1, :], jnp.where(row == 1, ah[hl - 1:hl, :], pltpu.roll(a, 2, 0)))
    dec = jnp.logical_and(i == 0, row < n_dec)
    pad = jnp.zeros((a.shape[0] - n_dec, a.shape[1]), F32)
    s1 = jnp.where(dec, jnp.concatenate([st1_ref[...], pad], axis=0), s1)
    s2 = jnp.where(dec, jnp.concatenate([st0_ref[...], pad], axis=0), s2)
    conv = cb_ref[...] + cw_ref[0:1, :] * s2 + cw_ref[1:2, :] * s1 + cw_ref[2:3, :] * a
    act = conv * _sigmoid(conv)
    b = jnp.dot(h, wb_ref[...], preferred_element_type=F32)
    gate = (act * b).astype(BF16)
    acc_ref[...] = acc_ref[...] + jnp.dot(gate, wd_ref[...], preferred_element_type=F32)
    head_ref[...] = a[0:head_ref.shape[0], :]
    tail_ref[...] = a[a.shape[0] - 8:, :]

    @pl.when(j == pl.num_programs(1) - 1)
    def _():
        x2 = x1_ref[...] + acc_ref[...]
        ms = jnp.mean(x2 * x2, axis=-1, keepdims=True)
        y_ref[...] = x2 * lax.rsqrt(ms + EPS) * gf_ref[...]


def _rope_tables(pos, hd, reps):
    rot = hd // 4
    half = rot // 2
    inv = jnp.exp(-math.log(ROPE_THETA) * jnp.arange(half, dtype=F32) * 2.0 / rot)
    ang = pos.astype(F32)[:, None] * inv[None, :]
    cos, sin = jnp.cos(ang), jnp.sin(ang)
    n = pos.shape[0]
    one = jnp.ones((n, hd - rot), F32)
    zero_r = jnp.zeros((n, hd - rot), F32)
    zero_h = jnp.zeros((n, half), F32)
    c = jnp.concatenate([cos, cos, one], axis=1)
    sa = jnp.concatenate([-sin, zero_h, zero_r], axis=1)
    sb = jnp.concatenate([zero_h, sin, zero_r], axis=1)
    return [jnp.tile(t, (1, reps)) for t in (c, sa, sb)]


def kernel(x_prompt, x_sample, cache_k, cache_v, cache_idx_k, state_gla, state_conv, page_table, meta_tokens, norm_mix_g, w_in, w_alpha_up, b_alpha, gla_norm_g, w_branch_a, w_branch_b, w_out, norm_ffn_g, w_up, conv_w, conv_b, w_down, norm_final_g):
    B, SEQ, D = x_prompt.shape
    DB = x_sample.shape[0]
    assert x_sample.shape[1] == 1 and w_in.shape[0] == 1
    n_pool = cache_k.shape[1]
    n_pages = page_table.shape[1]
    past = n_pages * PAGE_SIZE
    assert n_pages == LANE and PAGE_SIZE == LANE
    dff = w_down.shape[1]
    H = GLA_HEADS
    dk = D // 2 // H
    dv = D // H
    hd = D // ATT_HEADS
    kvw = ATT_KV_HEADS * hd
    T = SEQ + N_META
    TP = SEQ + FRONT
    R = B * TP
    assert DB <= GLA_CHUNK and DB % 16 == 0 and SEQ % LANE == 0 and dff % COL_TILE == 0
    k_sel_p = min(TOPK_MAX, T // 4)
    k_sel_s = min(TOPK_MAX, (past + 1) // 4)

    sizes = (H * dk, H * dk, H * dv, H * dv, GLA_GATE_RANK, ATT_HEADS * hd, kvw, kvw,
             IDX_HEADS * IDX_DIM, IDX_DIM, IDX_HEADS, D, D)
    offs = [0]
    for s_ in sizes:
        offs.append(offs[-1] + s_)
    w0 = w_in[0]
    wg_a = w0[:, offs[0]:offs[4]].astype(BF16)
    wg_b = w0[:, offs[5]:offs[9]].astype(BF16)
    wg_c = w0[:, offs[11]:offs[13]].astype(BF16)
    small_pad = COL_TILE - (IDX_DIM + GLA_GATE_RANK + IDX_HEADS)
    wg_s = jnp.concatenate([w0[:, offs[9]:offs[10]], w0[:, offs[4]:offs[5]], w0[:, offs[10]:offs[11]],
                            jnp.zeros((D, small_pad), F32)], axis=1).astype(BF16)
    n_a, n_b, n_c = (wg_a.shape[1] // COL_TILE, wg_b.shape[1] // COL_TILE, wg_c.shape[1] // COL_TILE)
    assert all(w_.shape[1] % COL_TILE == 0 for w_ in (wg_a, wg_b, wg_c)) and hd == LANE
    QA, KA, VA, RA = 0, sizes[0], sizes[0] + sizes[1], sizes[0] + sizes[1] + sizes[2]
    QB = n_a * COL_TILE
    KB, VB, QI = QB + sizes[5], QB + sizes[5] + sizes[6], QB + sizes[5] + sizes[6] + sizes[7]
    GA = (n_a + n_b) * COL_TILE
    GB = GA + D
    SM = (n_a + n_b + n_c) * COL_TILE
    NW = SM + COL_TILE
    assert all(v_ % COL_TILE == 0 for v_ in (QA, KA, VA, RA, QB, KB, VB, QI, GA, GB))
    assert VB - KB == COL_TILE
    n_ct = NW // COL_TILE
    wau_pad = jnp.zeros((LANE, H * dk), F32).at[IDX_DIM:IDX_DIM + GLA_GATE_RANK].set(w_alpha_up[0]).astype(BF16)
    wa_bf = w_branch_a[0].astype(BF16)
    wb_bf = w_branch_b[0].astype(BF16)
    wo_bf = w_out[0].astype(BF16)
    wup_bf = w_up[0].astype(BF16)
    wdn_bf = w_down[0].astype(BF16)

    front = jnp.zeros((B, ROW0, D), F32).at[0, :DB].set(x_sample[:, 0])
    meta = jnp.broadcast_to(meta_tokens[None].astype(F32), (B, N_META, D))
    x_all = jnp.concatenate([front, meta, x_prompt], axis=1).reshape(R, D)
    rpos = jnp.maximum(jnp.arange(TP, dtype=jnp.int32) - ROW0, 0)
    pos = jnp.concatenate([rpos.at[:DB].set(past), rpos])
    tabs = jnp.concatenate(_rope_tables(pos, hd, 1) + _rope_tables(pos, IDX_DIM, LANE // IDX_DIM), axis=1)

    tm1 = TP // 2
    bpb = TP // tm1
    P = pl.pallas_call(
        functools.partial(_inproj_kernel, n_a=n_a, n_b=n_b, n_c=n_c,
                          b128=(0, (VB - QB) // COL_TILE),
                          b64=((QI - QB) // COL_TILE, (QI - QB + IDX_HEADS * IDX_DIM) // COL_TILE)),
        grid=(R // tm1, n_ct),
        in_specs=[pl.BlockSpec((tm1, D), lambda i, j: (i, 0)),
                  pl.BlockSpec((1, D), lambda i, j: (0, 0)),
                  pl.BlockSpec((D, COL_TILE), lambda i, j: (0, jnp.minimum(j, n_a - 1))),
                  pl.BlockSpec((D, COL_TILE), lambda i, j: (0, jnp.clip(j - n_a, 0, n_b - 1))),
                  pl.BlockSpec((D, COL_TILE), lambda i, j: (0, jnp.clip(j - n_a - n_b, 0, n_c - 1))),
                  pl.BlockSpec((D, COL_TILE), lambda i, j: (0, 0)),
                  pl.BlockSpec((tm1, 6 * LANE), lambda i, j: (jnp.where(i < bpb, i, bpb + i % bpb), 0))],
        out_specs=pl.BlockSpec((tm1, COL_TILE), lambda i, j: (i, j)),
        out_shape=jax.ShapeDtypeStruct((R, NW), F32),
        scratch_shapes=[pltpu.VMEM((tm1, D), BF16)],
        compiler_params=_cparams(("parallel", "arbitrary")),
        name="inproj",
    )(x_all, norm_mix_g, wg_a, wg_b, wg_c, wg_s, tabs)

    C = GLA_CHUNK
    ncb = TP // C
    ba2 = b_alpha.reshape(1, H * dk)
    gain2 = gla_norm_g.reshape(1, H * dv)
    smc = SM // LANE
    a_out, s_fin = pl.pallas_call(
        functools.partial(_gla_prompt_kernel, dk=dk, dv=dv),
        grid=(B, ncb),
        in_specs=[pl.BlockSpec((C, H * dk), lambda b, c: (b * ncb + c, QA // (H * dk))),
                  pl.BlockSpec((C, H * dk), lambda b, c: (b * ncb + c, KA // (H * dk))),
                  pl.BlockSpec((C, H * dv), lambda b, c: (b * ncb + c, VA // (H * dv))),
                  pl.BlockSpec((C, H * dv), lambda b, c: (b * ncb + c, RA // (H * dv))),
                  pl.BlockSpec((C, LANE), lambda b, c: (b * ncb + c, smc)),
                  pl.BlockSpec((LANE, H * dk), lambda b, c: (0, 0)),
                  pl.BlockSpec((1, H * dk), lambda b, c: (0, 0)),
                  pl.BlockSpec((1, H * dv), lambda b, c: (0, 0))],
        out_specs=[pl.BlockSpec((C, H * dv), lambda b, c: (b * ncb + c, 0)),
                   pl.BlockSpec((None, H, dk, dv), lambda b, c: (b, 0, 0, 0))],
        out_shape=[jax.ShapeDtypeStruct((R, H * dv), BF16),
                   jax.ShapeDtypeStruct((B, H, dk, dv), F32)],
        scratch_shapes=[pltpu.VMEM((H, dk, dv), F32)],
        compiler_params=_cparams(("parallel", "arbitrary")),
        name="gla_prompt",
    )(P, P, P, P, P, wau_pad, ba2, gain2)

    p_s = P[:DB]
    P_rows = p_s.reshape(DB, 1, NW)
    a_out, s_new_s = pl.pallas_call(
        functools.partial(_gla_sample_kernel, dk=dk, dv=dv),
        grid=(DB,),
        in_specs=[pl.BlockSpec(memory_space=pl.ANY),
                  pl.BlockSpec((None, 1, H * dk), lambda d: (d, 0, QA // (H * dk))),
                  pl.BlockSpec((None, 1, H * dk), lambda d: (d, 0, KA // (H * dk))),
                  pl.BlockSpec((None, 1, H * dv), lambda d: (d, 0, VA // (H * dv))),
                  pl.BlockSpec((None, 1, H * dv), lambda d: (d, 0, RA // (H * dv))),
                  pl.BlockSpec((None, 1, LANE), lambda d: (d, 0, smc)),
                  pl.BlockSpec((LANE, H * dk), lambda d: (0, 0)),
                  pl.BlockSpec((1, H * dk), lambda d: (0, 0)),
                  pl.BlockSpec((1, H * dv), lambda d: (0, 0)),
                  pl.BlockSpec((None, H, dk, dv), lambda d: (d, 0, 0, 0))],
        out_specs=[pl.BlockSpec((DB, H * dv), lambda d: (0, 0)),
                   pl.BlockSpec((None, H, dk, dv), lambda d: (d, 0, 0, 0))],
        out_shape=[jax.ShapeDtypeStruct((R, H * dv), BF16),
                   jax.ShapeDtypeStruct((DB, H, dk, dv), F32)],
        scratch_shapes=[pltpu.VMEM((DB, H * dv), F32)],
        input_output_aliases={0: 0},
        compiler_params=_cparams(("arbitrary",)),
        name="gla_sample",
    )(a_out, P_rows, P_rows, P_rows, P_rows, P_rows, wau_pad, ba2, gain2, state_gla[0])

    TQ = LANE
    nqb = TP // TQ
    n_cls = -(-nqb // QB_PER_CLASS)
    widths = tuple(min((c + 1) * QB_PER_CLASS * TQ, TP) for c in range(n_cls))
    o_b = jnp.zeros((R, ATT_HEADS * hd), BF16)
    for c, width in enumerate(widths):
        q_lo = c * QB_PER_CLASS
        n_q = min(QB_PER_CLASS, nqb - q_lo)
        row_blk = lambda b, q, q_lo=q_lo: b * nqb + q_lo + q
        specs = [pl.BlockSpec((TQ, IDX_HEADS * IDX_DIM), lambda b, q, r_=row_blk: (r_(b, q), QI // (IDX_HEADS * IDX_DIM))),
                 pl.BlockSpec((TQ, LANE), lambda b, q, r_=row_blk: (r_(b, q), smc)),
                 pl.BlockSpec((TP, LANE), lambda b, q: (b, smc)),
                 pl.BlockSpec((TQ, ATT_HEADS * hd), lambda b, q, r_=row_blk: (r_(b, q), QB // (ATT_HEADS * hd))),
                 pl.BlockSpec((TP, kvw), lambda b, q: (b, KB // kvw)),
                 pl.BlockSpec((TP, kvw), lambda b, q: (b, VB // kvw))]
        o_b = pl.pallas_call(
            functools.partial(_dsa_prompt_kernel, k_sel=k_sel_p, hd=hd, width=width, q_lo=q_lo),
            grid=(B, n_q),
            in_specs=[pl.BlockSpec(memory_space=pl.ANY)] + specs,
            out_specs=pl.BlockSpec((TQ, ATT_HEADS * hd), lambda b, q, r_=row_blk: (r_(b, q), 0)),
            out_shape=jax.ShapeDtypeStruct((R, ATT_HEADS * hd), BF16),
            scratch_shapes=[pltpu.VMEM((width, LANE), BF16), pltpu.VMEM((width, LANE), BF16),
                            pltpu.VMEM((width, kvw), BF16), pltpu.VMEM((width, kvw), BF16),
                            pltpu.VMEM((TQ, width), F32), pltpu.VMEM((TQ, width), F32)],
            input_output_aliases={0: 0},
            compiler_params=_cparams(("parallel", "arbitrary")),
            name=f"dsa_prompt_w{width}",
        )(o_b, P, P, P, P, P, P)

    qi_s = p_s[:, QI:QI + IDX_HEADS * IDX_DIM].reshape(DB, IDX_HEADS, IDX_DIM)
    wi_s = p_s[:, SM + WI_OFF:SM + WI_OFF + IDX_HEADS].reshape(DB, IDX_HEADS, 1)
    ki_s = p_s[:, SM:SM + IDX_DIM].reshape(DB, 1, IDX_DIM)
    sc, scn = pl.pallas_call(
        functools.partial(_dsa_sample_score_kernel, n_pages=n_pages),
        grid_spec=pltpu.PrefetchScalarGridSpec(
            num_scalar_prefetch=1,
            grid=(DB,),
            in_specs=[pl.BlockSpec(memory_space=pl.ANY),
                      pl.BlockSpec((None, IDX_HEADS, IDX_DIM), lambda d, pt: (d, 0, 0)),
                      pl.BlockSpec((None, IDX_HEADS, 1), lambda d, pt: (d, 0, 0)),
                      pl.BlockSpec((None, 1, IDX_DIM), lambda d, pt: (d, 0, 0))],
            out_specs=[pl.BlockSpec((None, n_pages, PAGE_SIZE), lambda d, pt: (d, 0, 0)),
                       pl.BlockSpec((None, 1, LANE), lambda d, pt: (d, 0, 0))],
            scratch_shapes=[pltpu.VMEM((2, n_pages, IDX_DIM, PAGE_SIZE), F32),
                            pltpu.SemaphoreType.DMA((2,))]),
        out_shape=[jax.ShapeDtypeStruct((DB, n_pages, PAGE_SIZE), F32),
                   jax.ShapeDtypeStruct((DB, 1, LANE), F32)],
        compiler_params=_cparams(("arbitrary",)),
        name="dsa_sample_score",
    )(page_table, jnp.swapaxes(cache_idx_k[0], 1, 2), qi_s, wi_s, ki_s)

    hl_np = np.zeros((past + LANE, LANE), np.float32)
    hl_np[:, 0] = np.arange(past + LANE) // PAGE_SIZE
    hl_np[:, 1] = np.arange(past + LANE) % PAGE_SIZE
    hl = jnp.asarray(hl_np, dtype=BF16)
    sel = pl.pallas_call(
        functools.partial(_dsa_sample_select_kernel, k_sel=k_sel_s, chunk=2048),
        out_shape=jax.ShapeDtypeStruct((DB, k_sel_s, LANE), I32),
        scratch_shapes=[pltpu.VMEM((DB, 1, past + LANE), F32), pltpu.VMEM((DB, past + LANE), F32)],
        compiler_params=pltpu.CompilerParams(vmem_limit_bytes=VMEM_LIMIT),
        name="dsa_sample_select",
    )(sc.reshape(DB, past), scn.reshape(DB, LANE), hl, page_table.astype(F32).reshape(DB, 1, n_pages))

    rows_s = sel[:, :, 2]
    q_s = p_s[:, QB:QB + ATT_HEADS * hd].reshape(DB, ATT_HEADS, hd)
    kn_s = p_s[:, KB:KB + kvw]
    vn_s = p_s[:, VB:VB + kvw]
    o_b = pl.pallas_call(
        functools.partial(_dsa_sample_attend_kernel, k_sel=k_sel_s, n_pages=n_pages, hd=hd),
        grid_spec=pltpu.PrefetchScalarGridSpec(
            num_scalar_prefetch=1,
            grid=(DB,),
            in_specs=[pl.BlockSpec(memory_space=pl.ANY),
                      pl.BlockSpec(memory_space=pl.ANY),
                      pl.BlockSpec(memory_space=pl.ANY),
                      pl.BlockSpec((None, k_sel_s, LANE), lambda d, s_: (d, 0, 0)),
                      pl.BlockSpec((None, ATT_HEADS, hd), lambda d, s_: (d, 0, 0)),
                      pl.BlockSpec((None, 1, kvw), lambda d, s_: (d, 0, KB // kvw)),
                      pl.BlockSpec((None, 1, kvw), lambda d, s_: (d, 0, VB // kvw))],
            out_specs=pl.BlockSpec((DB, ATT_HEADS * hd), lambda d, s_: (0, 0)),
            scratch_shapes=[pltpu.VMEM((2, k_sel_s, ATT_KV_HEADS, hd), F32),
                            pltpu.VMEM((2, k_sel_s, ATT_KV_HEADS, hd), F32),
                            pltpu.VMEM((DB, ATT_HEADS, hd), F32), pltpu.SemaphoreType.DMA((2, 2))]),
        out_shape=jax.ShapeDtypeStruct((R, ATT_HEADS * hd), BF16),
        input_output_aliases={1: 0},
        compiler_params=_cparams(("arbitrary",)),
        name="dsa_sample_attend",
    )(rows_s, o_b, cache_k[0].reshape(n_pool * PAGE_SIZE, ATT_KV_HEADS, hd),
      cache_v[0].reshape(n_pool * PAGE_SIZE, ATT_KV_HEADS, hd), sel, q_s, P_rows, P_rows)

    mix = pl.pallas_call(
        _mix_kernel,
        grid=(R // tm1, D // COL_TILE),
        in_specs=[pl.BlockSpec((tm1, H * dv), lambda i, j: (i, 0)),
                  pl.BlockSpec((tm1, ATT_HEADS * hd), lambda i, j: (i, 0)),
                  pl.BlockSpec((H * dv, COL_TILE), lambda i, j: (0, j)),
                  pl.BlockSpec((ATT_HEADS * hd, COL_TILE), lambda i, j: (0, j)),
                  pl.BlockSpec((tm1, COL_TILE), lambda i, j: (i, GA // COL_TILE + j)),
                  pl.BlockSpec((tm1, COL_TILE), lambda i, j: (i, GB // COL_TILE + j))],
        out_specs=pl.BlockSpec((tm1, COL_TILE), lambda i, j: (i, j)),
        out_shape=jax.ShapeDtypeStruct((R, D), BF16),
        compiler_params=_cparams(("parallel", "arbitrary")),
        name="mix",
    )(a_out, o_b, wa_bf, wb_bf, P, P)

    tm2 = TP // 4
    x1, h2 = pl.pallas_call(
        _outproj_kernel,
        grid=(R // tm2,),
        in_specs=[pl.BlockSpec((tm2, D), lambda i: (i, 0)),
                  pl.BlockSpec((tm2, D), lambda i: (i, 0)),
                  pl.BlockSpec((D, D), lambda i: (0, 0)),
                  pl.BlockSpec((1, D), lambda i: (0, 0))],
        out_specs=[pl.BlockSpec((tm2, D), lambda i: (i, 0)),
                   pl.BlockSpec((tm2, D), lambda i: (i, 0))],
        out_shape=[jax.ShapeDtypeStruct((R, D), F32), jax.ShapeDtypeStruct((R, D), BF16)],
        compiler_params=_cparams(("parallel",)),
        name="outproj",
    )(x_all, mix, wo_bf, norm_ffn_g)

    nrb = R // tm2
    nft = dff // COL_TILE
    y_all, a_head, a_tail = pl.pallas_call(
        functools.partial(_ffn_kernel, n_dec=DB),
        grid=(nrb, nft),
        in_specs=[pl.BlockSpec((tm2, D), lambda i, j: (i, 0)),
                  pl.BlockSpec((16, D), lambda i, j: (jnp.maximum(i * (tm2 // 16) - 1, 0), 0)),
                  pl.BlockSpec((tm2, D), lambda i, j: (i, 0)),
                  pl.BlockSpec((D, COL_TILE), lambda i, j: (0, j)),
                  pl.BlockSpec((D, COL_TILE), lambda i, j: (0, nft + j)),
                  pl.BlockSpec((COL_TILE, D), lambda i, j: (j, 0)),
                  pl.BlockSpec((CONV_W, COL_TILE), lambda i, j: (0, j)),
                  pl.BlockSpec((1, COL_TILE), lambda i, j: (0, j)),
                  pl.BlockSpec((DB, COL_TILE), lambda i, j: (0, j)),
                  pl.BlockSpec((DB, COL_TILE), lambda i, j: (0, j)),
                  pl.BlockSpec((1, D), lambda i, j: (0, 0))],
        out_specs=[pl.BlockSpec((tm2, D), lambda i, j: (i, 0)),
                   pl.BlockSpec((DB, COL_TILE), lambda i, j: (i, j)),
                   pl.BlockSpec((8, COL_TILE), lambda i, j: (i, j))],
        out_shape=[jax.ShapeDtypeStruct((R, D), F32),
                   jax.ShapeDtypeStruct((nrb * DB, dff), F32),
                   jax.ShapeDtypeStruct((nrb * 8, dff), F32)],
        scratch_shapes=[pltpu.VMEM((tm2, D), F32)],
        compiler_params=_cparams(("parallel", "arbitrary")),
        name="ffn",
    )(h2, h2, x1, wup_bf, wup_bf, wdn_bf, conv_w[0], conv_b, state_conv[0, :, 0], state_conv[0, :, 1],
      norm_final_g.reshape(1, D))

    y3 = y_all.reshape(B, TP, D)
    y_prompt = y3[:, FRONT:]
    y_sample = y_all[:DB].reshape(DB, 1, D)
    P3 = P.reshape(B, TP, NW)
    new_k_p = P3[:, ROW0:, KB:KB + kvw].reshape(1, B, T, ATT_KV_HEADS, hd)
    new_v_p = P3[:, ROW0:, VB:VB + kvw].reshape(1, B, T, ATT_KV_HEADS, hd)
    new_ki_p = P3[:, ROW0:, SM:SM + IDX_DIM].reshape(1, B, T, IDX_DIM)
    new_gla_p = s_fin[None]
    tails = a_tail.reshape(B, nrb // B, 8, dff)[:, -1, 8 - (CONV_W - 1):, :]
    new_conv_p = tails[None]
    new_k_s = kn_s.reshape(1, DB, 1, ATT_KV_HEADS, hd)
    new_v_s = vn_s.reshape(1, DB, 1, ATT_KV_HEADS, hd)
    new_ki_s = ki_s.reshape(1, DB, 1, IDX_DIM)
    new_gla_s = s_new_s[None]
    new_conv_s = jnp.stack([state_conv[0, :, 1], a_head[:DB]], axis=1)[None]
    return (y_prompt, y_sample, new_k_p, new_v_p, new_ki_p, new_gla_p, new_conv_p,
            new_k_s, new_v_s, new_ki_s, new_gla_s, new_conv_s)
```

```python
import functools
import math

import jax
import jax.numpy as jnp
import numpy as np
from jax import lax
from jax.experimental import pallas as pl
from jax.experimental.pallas import tpu as pltpu

F32 = jnp.float32
BF16 = jnp.bfloat16
I32 = jnp.int32

N_META = 16
GLA_HEADS = 4
GLA_GATE_RANK = 16
GLA_GATE_NORM = 16.0
GLA_CHUNK = 64
GLA_SUB = 8
ATT_HEADS = 16
ATT_KV_HEADS = 4
IDX_HEADS = 16
IDX_DIM = 64
TOPK_MAX = 256
ROPE_THETA = 500000.0
CONV_W = 3
EPS = 1e-6
PAGE_SIZE = 128
WI_OFF = IDX_DIM + GLA_GATE_RANK

LANE = 128
FRONT = 128
ROW0 = FRONT - N_META
COL_TILE = 512
QB_PER_CLASS = 2
VMEM_LIMIT = 56 * 1024 * 1024
INT_MIN = -2 ** 31
LOG2_E = 1.4426950408889634


def _cparams(sem):
    return pltpu.CompilerParams(dimension_semantics=sem, vmem_limit_bytes=VMEM_LIMIT)


def _sigmoid(x):
    return 1.0 / (1.0 + jnp.exp(-x))


def _pattern_value(p):
    return pltpu.bitcast(jnp.where(p < 0, p ^ jnp.int32(0x7FFFFFFF), p), F32)


def _kth_largest(load_scores, rows, k_sel):
    def body(it, t):
        step = jnp.left_shift(jnp.int32(1), 30 - 2 * it)
        scores = load_scores()
        adv = jnp.zeros((rows, 1), I32)
        for m in (1, 2, 3):
            cnt = jnp.sum((scores >= _pattern_value(t + m * step)).astype(I32), axis=1, keepdims=True)
            adv = adv + (cnt >= k_sel).astype(I32)
        return t + adv * step

    t = lax.fori_loop(0, 16, body, jnp.full((rows, 1), INT_MIN, I32))
    return jnp.where(t == INT_MIN, -jnp.inf, _pattern_value(t))


def _rope_block(xb, c, sa, sb, half):
    return xb * c + pltpu.roll(xb, LANE - half, 1) * sa + pltpu.roll(xb, half, 1) * sb


W_RING = 3


def _inproj_w_copy(w_hbm, col, wbuf, sem, slot):
    start = col * COL_TILE
    if not isinstance(col, int):
        start = pl.multiple_of(start, COL_TILE)
    src = w_hbm.at[:, pl.ds(start, COL_TILE)]
    return pltpu.make_async_copy(src, wbuf.at[slot], sem.at[slot])


def _inproj_w_start(w_hbms, jj, wbuf, sem, slot, n_a, n_b, n_c):
    wa, wb, wc, ws = w_hbms
    lo = (0, n_a, n_a + n_b, n_a + n_b + n_c)
    hi = (n_a, n_a + n_b, n_a + n_b + n_c, n_a + n_b + n_c + 1)
    for w_hbm, l_, h_ in zip((wa, wb, wc, ws), lo, hi):
        @pl.when(jnp.logical_and(jj >= l_, jj < h_))
        def _(w_hbm=w_hbm, l_=l_):
            _inproj_w_copy(w_hbm, jj - l_, wbuf, sem, slot).start()


def _inproj_kernel(x_ref, g_ref, wa_ref, wb_ref, wc_ref, ws_ref, tab_ref, o_ref, h_ref, wbuf, sem,
                   *, n_a, n_b, n_c, b128, b64):
    j = pl.program_id(1)
    n_ct = pl.num_programs(1)
    s = pl.program_id(0) * n_ct + j
    w_hbms = (wa_ref, wb_ref, wc_ref, ws_ref)

    @pl.when(s == 0)
    def _():
        _inproj_w_copy(wa_ref, 0, wbuf, sem, 0).start()
        _inproj_w_copy(wa_ref, 1, wbuf, sem, 1).start()

    @pl.when(s + 2 < pl.num_programs(0) * n_ct)
    def _():
        _inproj_w_start(w_hbms, (j + 2) % n_ct, wbuf, sem, (s + 2) % W_RING, n_a, n_b, n_c)

    @pl.when(j == 0)
    def _():
        x = x_ref[...]
        ms = jnp.mean(x * x, axis=-1, keepdims=True)
        h_ref[...] = (x * lax.rsqrt(ms + EPS) * g_ref[...]).astype(BF16)

    slot = s % W_RING
    _inproj_w_copy(ws_ref, 0, wbuf, sem, slot).wait()

    def proj(w_ref):
        del w_ref
        return jnp.dot(h_ref[...], wbuf[slot], preferred_element_type=F32)

    @pl.when(j < n_a)
    def _():
        o_ref[...] = proj(wa_ref)

    @pl.when(jnp.logical_and(j >= n_a, j < n_a + n_b))
    def _():
        acc = proj(wb_ref)
        jb = j - n_a
        is128 = jnp.logical_and(jb >= b128[0], jb < b128[1])
        is64 = jnp.logical_and(jb >= b64[0], jb < b64[1])

        @pl.when(is128)
        def _():
            c, sa, sb = tab_ref[:, 0:128], tab_ref[:, 128:256], tab_ref[:, 256:384]
            for blk in range(COL_TILE // LANE):
                sl = slice(blk * LANE, (blk + 1) * LANE)
                o_ref[:, sl] = _rope_block(acc[:, sl], c, sa, sb, 16)

        @pl.when(is64)
        def _():
            c, sa, sb = tab_ref[:, 384:512], tab_ref[:, 512:640], tab_ref[:, 640:768]
            for blk in range(COL_TILE // LANE):
                sl = slice(blk * LANE, (blk + 1) * LANE)
                o_ref[:, sl] = _rope_block(acc[:, sl], c, sa, sb, 8)

        @pl.when(jnp.logical_not(is128 | is64))
        def _():
            o_ref[...] = acc

    @pl.when(jnp.logical_and(j >= n_a + n_b, j < n_a + n_b + n_c))
    def _():
        o_ref[...] = proj(wc_ref)

    @pl.when(j == n_a + n_b + n_c)
    def _():
        acc = proj(ws_ref)
        lane = lax.broadcasted_iota(I32, (1, LANE), 1)
        first = lane < IDX_DIM
        c = jnp.where(first, tab_ref[:, 384:512], 1.0)
        sa = jnp.where(first, tab_ref[:, 512:640], 0.0)
        sb = jnp.where(first, tab_ref[:, 640:768], 0.0)
        o_ref[:, 0:LANE] = _rope_block(acc[:, 0:LANE], c, sa, sb, 8)
        o_ref[:, LANE:] = acc[:, LANE:]


def _log_alpha(sm, wau_ref, ba_ref):
    x = jnp.dot(sm.astype(BF16), wau_ref[...], preferred_element_type=F32) + ba_ref[...]
    return (jnp.minimum(x, 0.0) - jnp.log(1.0 + jnp.exp(-jnp.abs(x)))) * (1.0 / GLA_GATE_NORM)


def _row_to_col(row, n):
    eye = lax.broadcasted_iota(I32, (n, n), 0) == lax.broadcasted_iota(I32, (n, n), 1)
    return jnp.sum(jnp.where(eye, row, 0.0), axis=1, keepdims=True)


def _readout(o, r, gain):
    ms = jnp.mean(o * o, axis=-1, keepdims=True)
    return o * lax.rsqrt(ms + EPS) * gain * (r * _sigmoid(r))


def _gla_chunk(q, k, v, g, S):
    C, dk = q.shape
    vb = v.astype(BF16)
    row = lax.broadcasted_iota(I32, g.shape, 0)
    b = g
    sh = 1
    while sh < C:
        b = b + jnp.where(row >= sh, pltpu.roll(b, sh, 0), 0.0)
        sh *= 2
    bl = b[C - 1:C, :]
    o = jnp.dot((q * jnp.exp(b)).astype(BF16), S.astype(BF16), preferred_element_type=F32)

    tcol = lax.broadcasted_iota(I32, (GLA_SUB, C), 1)
    trow = lax.broadcasted_iota(I32, (GLA_SUB, C), 0)
    a_rows = []
    kf = None
    prev_ref = None
    for blk in range(C // GLA_SUB):
        r0 = blk * GLA_SUB
        ref = b[r0 - 1:r0, :] if blk > 0 else jnp.zeros_like(bl)
        b_i = b[r0:r0 + GLA_SUB, :]
        q_i = q[r0:r0 + GLA_SUB, :]
        if blk == 0:
            a_off = jnp.zeros((GLA_SUB, C), F32)
        else:
            fresh = k[r0 - GLA_SUB:r0, :] * jnp.exp(ref - b[r0 - GLA_SUB:r0, :])
            kf = fresh if blk == 1 else jnp.concatenate([kf * jnp.exp(ref - prev_ref), fresh], axis=0)
            kf_full = jnp.concatenate([kf, jnp.zeros((C - r0, dk), F32)], axis=0)
            qe = q_i * jnp.exp(b_i - ref)
            a_off = lax.dot_general(qe.astype(BF16), kf_full.astype(BF16), (((1,), (1,)), ((), ())),
                                    preferred_element_type=F32)
        prev_ref = ref
        diag = jnp.zeros((GLA_SUB, C), F32)
        for sl in range(GLA_SUB):
            s = r0 + sl
            w = q_i * k[s:s + 1, :] * jnp.exp(jnp.minimum(b_i - b[s:s + 1, :], 0.0))
            diag = jnp.where(tcol == s, jnp.sum(w, axis=1, keepdims=True), diag)
        a_rows.append(jnp.where(tcol < r0, a_off, jnp.where(tcol <= trow + r0, diag, 0.0)))
    a = jnp.concatenate(a_rows, axis=0)
    o = o + jnp.dot(a.astype(BF16), vb, preferred_element_type=F32)

    kd = k * jnp.exp(bl - b)
    upd = lax.dot_general(kd.astype(BF16), vb, (((0,), (0,)), ((), ())),
                          preferred_element_type=F32)
    return o, S * _row_to_col(jnp.exp(bl), dk) + upd


def _gla_prompt_kernel(q_ref, k_ref, v_ref, r_ref, sm_ref, wau_ref, ba_ref, gain_ref,
                       o_ref, sfin_ref, s_ref, *, dk, dv):
    c = pl.program_id(1)
    H = s_ref.shape[0]

    @pl.when(c == 0)
    def _():
        s_ref[...] = jnp.zeros_like(s_ref)
        o_ref[...] = jnp.zeros_like(o_ref)

    @pl.when(c > 0)
    def _():
        g_all = _log_alpha(sm_ref[...], wau_ref, ba_ref)
        for h in range(H):
            ks, vs = slice(h * dk, (h + 1) * dk), slice(h * dv, (h + 1) * dv)
            o, s_new = _gla_chunk(q_ref[:, ks] * (dk ** -0.5), k_ref[:, ks], v_ref[:, vs],
                                  g_all[:, ks], s_ref[h])
            s_ref[h] = s_new
            o_ref[:, vs] = _readout(o, r_ref[:, vs], gain_ref[:, vs]).astype(BF16)

    @pl.when(c == pl.num_programs(1) - 1)
    def _():
        sfin_ref[...] = s_ref[...]


def _gla_sample_kernel(alias_ref, q_ref, k_ref, v_ref, r_ref, sm_ref, wau_ref, ba_ref, gain_ref,
                       st_ref, o_ref, snew_ref, acc_ref, *, dk, dv):
    del alias_ref
    d = pl.program_id(0)
    H = st_ref.shape[0]
    g_all = _log_alpha(jnp.broadcast_to(sm_ref[...], (8, LANE)), wau_ref, ba_ref)[0:1, :]
    ons = []
    for h in range(H):
        ks, vs = slice(h * dk, (h + 1) * dk), slice(h * dv, (h + 1) * dv)
        q = q_ref[:, ks] * (dk ** -0.5)
        k = k_ref[:, ks]
        v = v_ref[:, vs]
        eg = jnp.exp(g_all[:, ks])
        S = st_ref[h]
        qe = jnp.broadcast_to(q * eg, (8, dk))
        o = jnp.dot(qe.astype(BF16), S.astype(BF16), preferred_element_type=F32)[0:1, :]
        o = o + jnp.sum(q * k, axis=1, keepdims=True) * v
        snew_ref[h] = S * _row_to_col(eg, dk) + _row_to_col(k, dk) * v
        ons.append(_readout(o, r_ref[:, vs], gain_ref[:, vs]))
    on = jnp.concatenate(ons, axis=1)
    row = lax.broadcasted_iota(I32, acc_ref.shape, 0)

    @pl.when(d == 0)
    def _():
        acc_ref[...] = jnp.zeros_like(acc_ref)

    acc_ref[...] = jnp.where(row == d, on, acc_ref[...])

    @pl.when(d == pl.num_programs(0) - 1)
    def _():
        o_ref[...] = acc_ref[...].astype(BF16)


def _keep_lowest_ties(key_ref, bias_ref, thr, TK, k_sel):
    sc = key_ref[:, 0:TK]
    gt = sc > thr
    eq = jnp.logical_and(sc == thr, sc > -jnp.inf)
    room = k_sel - jnp.sum(jnp.where(gt, 1.0, 0.0), axis=1, keepdims=True)
    upper = jnp.where(lax.broadcasted_iota(I32, (LANE, LANE), 0) < lax.broadcasted_iota(I32, (LANE, LANE), 1),
                      1.0, 0.0).astype(BF16)
    run = jnp.zeros_like(room)
    for c in range(TK // LANE):
        blk = slice(c * LANE, (c + 1) * LANE)
        e = jnp.where(eq[:, blk], 1.0, 0.0)
        before = jnp.dot(e.astype(BF16), upper, preferred_element_type=F32) + run
        keep = jnp.logical_or(gt[:, blk], jnp.logical_and(eq[:, blk], before < room))
        bias_ref[:, blk] = jnp.where(keep, 0.0, -jnp.inf)
        run = run + jnp.sum(e, axis=1, keepdims=True)


def _dsa_prompt_kernel(alias_ref, qi_ref, smq_ref, smk_ref, qb_ref, kb_ref, vb_ref, o_ref,
                       ka_ref, kbb_ref, kbf_ref, vbf_ref, key_ref, bias_ref, *, k_sel, hd, width, q_lo):
    del alias_ref
    qb = q_lo + pl.program_id(1)
    TQ = qi_ref.shape[0]
    nt = (((1,), (1,)), ((), ()))
    G = ATT_HEADS // ATT_KV_HEADS

    @pl.when(pl.program_id(1) == 0)
    def _():
        smk = smk_ref[0:width, :]
        lane = lax.broadcasted_iota(I32, (1, LANE), 1)
        ka_ref[...] = jnp.where(lane < IDX_DIM, smk, 0.0).astype(BF16)
        kbb_ref[...] = jnp.where(lane >= IDX_DIM, pltpu.roll(smk, IDX_DIM, 1), 0.0).astype(BF16)
        kbf_ref[...] = kb_ref[0:width, :].astype(BF16)
        vbf_ref[...] = vb_ref[0:width, :].astype(BF16)

    def body(TK):
        score = jnp.zeros((TQ, TK), F32)
        for p in range(IDX_HEADS // 2):
            qp = qi_ref[:, p * LANE:(p + 1) * LANE].astype(BF16)
            for half, kref in ((0, ka_ref), (1, kbb_ref)):
                h = 2 * p + half
                s = lax.dot_general(qp, kref[0:TK, :], nt, preferred_element_type=F32)
                w = smq_ref[:, WI_OFF + h:WI_OFF + h + 1] * (IDX_HEADS ** -0.5 * IDX_DIM ** -0.5)
                score = score + w * jnp.maximum(s, 0.0)

        qrow = qb * TQ + lax.broadcasted_iota(I32, (TQ, 1), 0)
        kcol = lax.broadcasted_iota(I32, (1, TK), 1)
        adm = jnp.logical_and(kcol <= qrow, kcol >= ROW0)
        key_ref[:, 0:TK] = jnp.where(adm, score, -jnp.inf)
        thr = _kth_largest(lambda: key_ref[:, 0:TK], TQ, k_sel)
        sc = key_ref[:, 0:TK]
        ge = jnp.logical_and(sc >= thr, sc > -jnp.inf)
        bias_ref[:, 0:TK] = jnp.where(ge, 0.0, -jnp.inf)
        n_ge = jnp.sum(jnp.where(ge, 1.0, 0.0), axis=1, keepdims=True)

        @pl.when(jnp.max(n_ge) > k_sel)
        def _():
            _keep_lowest_ties(key_ref, bias_ref, thr, TK, k_sel)

        def qk(n):
            q4 = jnp.concatenate(
                [(qb_ref[:, (n * G + gq) * hd:(n * G + gq + 1) * hd] * (hd ** -0.5 * LOG2_E)).astype(BF16)
                 for gq in range(G)], axis=0)
            return lax.dot_general(q4, kbf_ref[0:TK, n * hd:(n + 1) * hd], nt, preferred_element_type=F32)

        s_next = qk(0)
        for n in range(ATT_KV_HEADS):
            s = s_next
            if n + 1 < ATT_KV_HEADS:
                s_next = qk(n + 1)
            s = (s.reshape(G, TQ, TK) + bias_ref[:, 0:TK][None]).reshape(G * TQ, TK)
            m = jnp.max(s, axis=1, keepdims=True)
            m = jnp.where(m == -jnp.inf, 0.0, m)
            p = jnp.exp2(s - m)
            l = jnp.sum(p, axis=1, keepdims=True)
            o = jnp.dot(p.astype(BF16), vbf_ref[0:TK, n * hd:(n + 1) * hd], preferred_element_type=F32)
            o = jnp.where(l > 0.0, o / l, 0.0)
            for gq in range(G):
                o_ref[:, (n * G + gq) * hd:(n * G + gq + 1) * hd] = o[gq * TQ:(gq + 1) * TQ, :].astype(BF16)

    body(width)


def _idx_page_copy(cache_ref, buf_ref, sem, page, slot, p):
    return pltpu.make_async_copy(cache_ref.at[page], buf_ref.at[slot, p], sem.at[slot])


def _dsa_sample_score_kernel(pt_ref, cache_ref, qi_ref, wi_ref, kin_ref, sc_ref, scn_ref,
                             buf_ref, sem, *, n_pages):
    d = pl.program_id(0)
    slot = d % 2

    def fetch(dd, sl):
        def start(p, carry):
            _idx_page_copy(cache_ref, buf_ref, sem, pt_ref[dd, p], sl, p).start()
            return carry
        lax.fori_loop(0, n_pages, start, 0, unroll=8)

    @pl.when(d == 0)
    def _():
        fetch(0, 0)

    @pl.when(d + 1 < pl.num_programs(0))
    def _():
        fetch(d + 1, 1 - slot)

    def wait(p, carry):
        _idx_page_copy(cache_ref, buf_ref, sem, 0, slot, p).wait()
        return carry

    lax.fori_loop(0, n_pages, wait, 0, unroll=8)
    nt = (((1,), (1,)), ((), ()))
    qi = qi_ref[...].astype(BF16)
    w = wi_ref[...] * (IDX_HEADS ** -0.5 * IDX_DIM ** -0.5)
    kp = buf_ref[slot].astype(BF16)
    qib = jnp.broadcast_to(qi[None], (n_pages, IDX_HEADS, IDX_DIM))
    s = lax.dot_general(qib, kp, (((2,), (1,)), ((0,), (0,))), preferred_element_type=F32)
    sc_ref[...] = jnp.sum(w[None] * jnp.maximum(s, 0.0), axis=1)
    kn = jnp.broadcast_to(kin_ref[...], (8, IDX_DIM)).astype(BF16)
    sn = lax.dot_general(qi, kn, nt, preferred_element_type=F32)[:, 0:1]
    scn_ref[...] = jnp.broadcast_to(jnp.sum(w * jnp.maximum(sn, 0.0), axis=0, keepdims=True), (1, LANE))


def _dsa_sample_select_kernel(sc_ref, scn_ref, hl_ref, ptf_ref, sel_ref, pos_ref, key_ref, *, k_sel, chunk):
    DB, NP = sc_ref.shape
    nblk = NP // LANE
    lane1 = lax.broadcasted_iota(I32, (1, LANE), 1)
    key_ref[:, 0:NP] = sc_ref[...]
    key_ref[:, NP:] = jnp.where(lane1 == 0, scn_ref[...], -jnp.inf)
    thr = _kth_largest(lambda: key_ref[...], DB, k_sel)
    keys = key_ref[...]
    gt = keys > thr
    eq = jnp.logical_and(keys == thr, keys > -jnp.inf)
    n_gt = jnp.sum(gt.astype(I32), axis=1, keepdims=True).astype(F32)

    iu = lax.broadcasted_iota(I32, (LANE, LANE), 0)
    ju = lax.broadcasted_iota(I32, (LANE, LANE), 1)
    upper = jnp.where(iu < ju, 1.0, 0.0).astype(BF16)
    ones = jnp.ones((LANE, LANE), BF16)

    def excl_prefix(mask):
        mb = jnp.where(mask, 1.0, 0.0).astype(BF16)
        stacked = jnp.concatenate([mb[:, c * LANE:(c + 1) * LANE] for c in range(nblk + 1)], axis=0)
        within = jnp.dot(stacked, upper, preferred_element_type=F32)
        tot = jnp.dot(stacked, ones, preferred_element_type=F32)[:, 0:1]
        outs = []
        run = jnp.zeros((DB, 1), F32)
        for c in range(nblk + 1):
            outs.append(within[c * DB:(c + 1) * DB, :] + run)
            run = run + tot[c * DB:(c + 1) * DB, :]
        return jnp.concatenate(outs, axis=1)

    pos_gt = excl_prefix(gt)
    pos_eq = excl_prefix(eq) + n_gt
    keep_eq = jnp.logical_and(eq, pos_eq < k_sel)
    pos = jnp.where(gt, pos_gt, jnp.where(keep_eq, pos_eq, -1.0))
    for d in range(DB):
        pos_ref[d] = pos[d:d + 1, :]

    jrow = lax.broadcasted_iota(I32, (k_sel, 1), 0).astype(F32)
    lane_k = lax.broadcasted_iota(I32, (k_sel, LANE), 1)
    width = NP + LANE

    def compact(d, carry):
        acc = jnp.zeros((k_sel, LANE), F32)
        for c0 in range(0, width, chunk):
            c1 = min(c0 + chunk, width)
            e = jnp.where(pos_ref[d, :, c0:c1] == jrow, 1.0, 0.0).astype(BF16)
            acc = acc + jnp.dot(e, hl_ref[c0:c1, :], preferred_element_type=F32)
        onehot = lane_k.astype(F32) == acc[:, 0:1]
        phys = jnp.sum(jnp.where(onehot, ptf_ref[d], 0.0), axis=1, keepdims=True)
        acc = jnp.where(lane_k == 2, phys * PAGE_SIZE + acc[:, 1:2], acc)
        sel_ref[d] = acc.astype(I32)
        return carry

    lax.fori_loop(0, DB, compact, 0)


def _kv_copies(ck_ref, cv_ref, kbuf_ref, vbuf_ref, sem, row, buf, j):
    ck = pltpu.make_async_copy(ck_ref.at[row], kbuf_ref.at[buf, j], sem.at[0, buf])
    cv = pltpu.make_async_copy(cv_ref.at[row], vbuf_ref.at[buf, j], sem.at[1, buf])
    return ck, cv


def _dsa_sample_attend_kernel(rows_ref, alias_ref, ck_ref, cv_ref, selv_ref, q_ref,
                              kn_ref, vn_ref, o_ref, kbuf_ref, vbuf_ref, acc_ref, sem,
                              *, k_sel, n_pages, hd):
    del alias_ref
    d = pl.program_id(0)
    buf = d % 2

    def fetch(dd, bb):
        def start(j, carry):
            ck, cv = _kv_copies(ck_ref, cv_ref, kbuf_ref, vbuf_ref, sem, rows_ref[dd, j], bb, j)
            ck.start()
            cv.start(priority=1)
            return carry
        lax.fori_loop(0, k_sel, start, 0, unroll=8)

    @pl.when(d == 0)
    def _():
        fetch(0, 0)

    @pl.when(d + 1 < pl.num_programs(0))
    def _():
        fetch(d + 1, 1 - buf)

    def wait(j, carry):
        ck, cv = _kv_copies(ck_ref, cv_ref, kbuf_ref, vbuf_ref, sem, 0, buf, j)
        ck.wait()
        cv.wait()
        return carry

    lax.fori_loop(0, k_sel, wait, 0, unroll=8)
    is_new = selv_ref[:, 0:1] >= n_pages
    G = ATT_HEADS // ATT_KV_HEADS
    nt = (((1,), (1,)), ((), ()))
    q = q_ref[...] * (hd ** -0.5)
    outs = []
    for n in range(ATT_KV_HEADS):
        hs = slice(n * hd, (n + 1) * hd)
        kk = jnp.where(is_new, kn_ref[:, hs], kbuf_ref[buf, :, n, :]).astype(BF16)
        vv = jnp.where(is_new, vn_ref[:, hs], vbuf_ref[buf, :, n, :]).astype(BF16)
        qn = jnp.concatenate([q[n * G:(n + 1) * G, :], jnp.zeros((8 - G, hd), F32)], axis=0).astype(BF16)
        s = lax.dot_general(qn, kk, nt, preferred_element_type=F32)
        m = jnp.max(s, axis=1, keepdims=True)
        p = jnp.exp(s - m)
        l = jnp.sum(p, axis=1, keepdims=True)
        o = jnp.dot(p.astype(BF16), vv, preferred_element_type=F32) / l
        outs.append(o[0:G, :])
    acc_ref[pl.ds(d, 1)] = jnp.concatenate(outs, axis=0)[None]

    @pl.when(d == pl.num_programs(0) - 1)
    def _():
        for h in range(ATT_HEADS):
            o_ref[:, h * hd:(h + 1) * hd] = acc_ref[:, h, :].astype(BF16)


def _mix_kernel(a_ref, b_ref, wa_ref, wb_ref, ga_ref, gb_ref, o_ref):
    ba = jnp.dot(a_ref[...], wa_ref[...], preferred_element_type=F32)
    bb = jnp.dot(b_ref[...], wb_ref[...], preferred_element_type=F32)
    o_ref[...] = (_sigmoid(ga_ref[...]) * ba + _sigmoid(gb_ref[...]) * bb).astype(BF16)


def _outproj_kernel(x_ref, m_ref, w_ref, g_ref, x1_ref, h_ref):
    x1 = x_ref[...] + jnp.dot(m_ref[...], w_ref[...], preferred_element_type=F32)
    x1_ref[...] = x1
    ms = jnp.mean(x1 * x1, axis=-1, keepdims=True)
    h_ref[...] = (x1 * lax.rsqrt(ms + EPS) * g_ref[...]).astype(BF16)


def _ffn_kernel(h_ref, halo_ref, x1_ref, wa_ref, wb_ref, wd_ref, cw_ref, cb_ref, st0_ref, st1_ref,
                gf_ref, y_ref, head_ref, tail_ref, acc_ref, *, n_dec):
    i = pl.program_id(0)
    j = pl.program_id(1)

    @pl.when(j == 0)
    def _():
        acc_ref[...] = jnp.zeros_like(acc_ref)

    h = h_ref[...]
    ah = jnp.dot(halo_ref[...], wa_ref[...], preferred_element_type=F32)
    ah = jnp.where(i > 0, ah, 0.0)
    a = jnp.dot(h, wa_ref[...], preferred_element_type=F32)
    row = lax.broadcasted_iota(I32, a.shape, 0)
    hl = ah.shape[0]
    s1 = jnp.where(row == 0, ah[hl - 1:hl, :], pltpu.roll(a, 1, 0))
    s2 = jnp.where(row == 0, ah[hl - 2:hl - 1, :], jnp.where(row == 1, ah[hl - 1:hl, :], pltpu.roll(a, 2, 0)))
    dec = jnp.logical_and(i == 0, row < n_dec)
    pad = jnp.zeros((a.shape[0] - n_dec, a.shape[1]), F32)
    s1 = jnp.where(dec, jnp.concatenate([st1_ref[...], pad], axis=0), s1)
    s2 = jnp.where(dec, jnp.concatenate([st0_ref[...], pad], axis=0), s2)
    conv = cb_ref[...] + cw_ref[0:1, :] * s2 + cw_ref[1:2, :] * s1 + cw_ref[2:3, :] * a
    act = conv * _sigmoid(conv)
    b = jnp.dot(h, wb_ref[...], preferred_element_type=F32)
    gate = (act * b).astype(BF16)
    acc_ref[...] = acc_ref[...] + jnp.dot(gate, wd_ref[...], preferred_element_type=F32)
    head_ref[...] = a[0:head_ref.shape[0], :]
    tail_ref[...] = a[a.shape[0] - 8:, :]

    @pl.when(j == pl.num_programs(1) - 1)
    def _():
        x2 = x1_ref[...] + acc_ref[...]
        ms = jnp.mean(x2 * x2, axis=-1, keepdims=True)
        y_ref[...] = x2 * lax.rsqrt(ms + EPS) * gf_ref[...]


def _rope_tables(pos, hd, reps):
    rot = hd // 4
    half = rot // 2
    inv = jnp.exp(-math.log(ROPE_THETA) * jnp.arange(half, dtype=F32) * 2.0 / rot)
    ang = pos.astype(F32)[:, None] * inv[None, :]
    cos, sin = jnp.cos(ang), jnp.sin(ang)
    n = pos.shape[0]
    one = jnp.ones((n, hd - rot), F32)
    zero_r = jnp.zeros((n, hd - rot), F32)
    zero_h = jnp.zeros((n, half), F32)
    c = jnp.concatenate([cos, cos, one], axis=1)
    sa = jnp.concatenate([-sin, zero_h, zero_r], axis=1)
    sb = jnp.concatenate([zero_h, sin, zero_r], axis=1)
    return [jnp.tile(t, (1, reps)) for t in (c, sa, sb)]


def kernel(x_prompt, x_sample, cache_k, cache_v, cache_idx_k, state_gla, state_conv, page_table, meta_tokens, norm_mix_g, w_in, w_alpha_up, b_alpha, gla_norm_g, w_branch_a, w_branch_b, w_out, norm_ffn_g, w_up, conv_w, conv_b, w_down, norm_final_g):
    B, SEQ, D = x_prompt.shape
    DB = x_sample.shape[0]
    assert x_sample.shape[1] == 1 and w_in.shape[0] == 1
    n_pool = cache_k.shape[1]
    n_pages = page_table.shape[1]
    past = n_pages * PAGE_SIZE
    assert n_pages == LANE and PAGE_SIZE == LANE
    dff = w_down.shape[1]
    H = GLA_HEADS
    dk = D // 2 // H
    dv = D // H
    hd = D // ATT_HEADS
    kvw = ATT_KV_HEADS * hd
    T = SEQ + N_META
    TP = SEQ + FRONT
    R = B * TP
    assert DB <= GLA_CHUNK and DB % 16 == 0 and SEQ % LANE == 0 and dff % COL_TILE == 0
    k_sel_p = min(TOPK_MAX, T // 4)
    k_sel_s = min(TOPK_MAX, (past + 1) // 4)

    sizes = (H * dk, H * dk, H * dv, H * dv, GLA_GATE_RANK, ATT_HEADS * hd, kvw, kvw,
             IDX_HEADS * IDX_DIM, IDX_DIM, IDX_HEADS, D, D)
    offs = [0]
    for s_ in sizes:
        offs.append(offs[-1] + s_)
    w0 = w_in[0]
    wg_a = w0[:, offs[0]:offs[4]].astype(BF16)
    wg_b = w0[:, offs[5]:offs[9]].astype(BF16)
    wg_c = w0[:, offs[11]:offs[13]].astype(BF16)
    small_pad = COL_TILE - (IDX_DIM + GLA_GATE_RANK + IDX_HEADS)
    wg_s = jnp.concatenate([w0[:, offs[9]:offs[10]], w0[:, offs[4]:offs[5]], w0[:, offs[10]:offs[11]],
                            jnp.zeros((D, small_pad), F32)], axis=1).astype(BF16)
    n_a, n_b, n_c = (wg_a.shape[1] // COL_TILE, wg_b.shape[1] // COL_TILE, wg_c.shape[1] // COL_TILE)
    assert all(w_.shape[1] % COL_TILE == 0 for w_ in (wg_a, wg_b, wg_c)) and hd == LANE
    QA, KA, VA, RA = 0, sizes[0], sizes[0] + sizes[1], sizes[0] + sizes[1] + sizes[2]
    QB = n_a * COL_TILE
    KB, VB, QI = QB + sizes[5], QB + sizes[5] + sizes[6], QB + sizes[5] + sizes[6] + sizes[7]
    GA = (n_a + n_b) * COL_TILE
    GB = GA + D
    SM = (n_a + n_b + n_c) * COL_TILE
    NW = SM + COL_TILE
    assert all(v_ % COL_TILE == 0 for v_ in (QA, KA, VA, RA, QB, KB, VB, QI, GA, GB))
    assert VB - KB == COL_TILE
    n_ct = NW // COL_TILE
    wau_pad = jnp.zeros((LANE, H * dk), F32).at[IDX_DIM:IDX_DIM + GLA_GATE_RANK].set(w_alpha_up[0]).astype(BF16)
    wa_bf = w_branch_a[0].astype(BF16)
    wb_bf = w_branch_b[0].astype(BF16)
    wo_bf = w_out[0].astype(BF16)
    wup_bf = w_up[0].astype(BF16)
    wdn_bf = w_down[0].astype(BF16)

    front = jnp.zeros((B, ROW0, D), F32).at[0, :DB].set(x_sample[:, 0])
    meta = jnp.broadcast_to(meta_tokens[None].astype(F32), (B, N_META, D))
    x_all = jnp.concatenate([front, meta, x_prompt], axis=1).reshape(R, D)
    rpos = jnp.maximum(jnp.arange(TP, dtype=jnp.int32) - ROW0, 0)
    pos = jnp.concatenate([rpos.at[:DB].set(past), rpos])
    tabs = jnp.concatenate(_rope_tables(pos, hd, 1) + _rope_tables(pos, IDX_DIM, LANE // IDX_DIM), axis=1)

    tm1 = TP // 2
    bpb = TP // tm1
    P = pl.pallas_call(
        functools.partial(_inproj_kernel, n_a=n_a, n_b=n_b, n_c=n_c,
                          b128=(0, (VB - QB) // COL_TILE),
                          b64=((QI - QB) // COL_TILE, (QI - QB + IDX_HEADS * IDX_DIM) // COL_TILE)),
        grid=(R // tm1, n_ct),
        in_specs=[pl.BlockSpec((tm1, D), lambda i, j: (i, 0)),
                  pl.BlockSpec((1, D), lambda i, j: (0, 0)),
                  pl.BlockSpec(memory_space=pl.ANY),
                  pl.BlockSpec(memory_space=pl.ANY),
                  pl.BlockSpec(memory_space=pl.ANY),
                  pl.BlockSpec(memory_space=pl.ANY),
                  pl.BlockSpec((tm1, 6 * LANE), lambda i, j: (jnp.where(i < bpb, i, bpb + i % bpb), 0))],
        out_specs=pl.BlockSpec((tm1, COL_TILE), lambda i, j: (i, j)),
        out_shape=jax.ShapeDtypeStruct((R, NW), F32),
        scratch_shapes=[pltpu.VMEM((tm1, D), BF16), pltpu.VMEM((W_RING, D, COL_TILE), BF16),
                        pltpu.SemaphoreType.DMA((W_RING,))],
        compiler_params=_cparams(("arbitrary", "arbitrary")),
        name="inproj",
    )(x_all, norm_mix_g, wg_a, wg_b, wg_c, wg_s, tabs)

    C = GLA_CHUNK
    ncb = TP // C
    ba2 = b_alpha.reshape(1, H * dk)
    gain2 = gla_norm_g.reshape(1, H * dv)
    smc = SM // LANE
    a_out, s_fin = pl.pallas_call(
        functools.partial(_gla_prompt_kernel, dk=dk, dv=dv),
        grid=(B, ncb),
        in_specs=[pl.BlockSpec((C, H * dk), lambda b, c: (b * ncb + c, QA // (H * dk))),
                  pl.BlockSpec((C, H * dk), lambda b, c: (b * ncb + c, KA // (H * dk))),
                  pl.BlockSpec((C, H * dv), lambda b, c: (b * ncb + c, VA // (H * dv))),
                  pl.BlockSpec((C, H * dv), lambda b, c: (b * ncb + c, RA // (H * dv))),
                  pl.BlockSpec((C, LANE), lambda b, c: (b * ncb + c, smc)),
                  pl.BlockSpec((LANE, H * dk), lambda b, c: (0, 0)),
                  pl.BlockSpec((1, H * dk), lambda b, c: (0, 0)),
                  pl.BlockSpec((1, H * dv), lambda b, c: (0, 0))],
        out_specs=[pl.BlockSpec((C, H * dv), lambda b, c: (b * ncb + c, 0)),
                   pl.BlockSpec((None, H, dk, dv), lambda b, c: (b, 0, 0, 0))],
        out_shape=[jax.ShapeDtypeStruct((R, H * dv), BF16),
                   jax.ShapeDtypeStruct((B, H, dk, dv), F32)],
        scratch_shapes=[pltpu.VMEM((H, dk, dv), F32)],
        compiler_params=_cparams(("parallel", "arbitrary")),
        name="gla_prompt",
    )(P, P, P, P, P, wau_pad, ba2, gain2)

    p_s = P[:DB]
    P_rows = p_s.reshape(DB, 1, NW)
    a_out, s_new_s = pl.pallas_call(
        functools.partial(_gla_sample_kernel, dk=dk, dv=dv),
        grid=(DB,),
        in_specs=[pl.BlockSpec(memory_space=pl.ANY),
                  pl.BlockSpec((None, 1, H * dk), lambda d: (d, 0, QA // (H * dk))),
                  pl.BlockSpec((None, 1, H * dk), lambda d: (d, 0, KA // (H * dk))),
                  pl.BlockSpec((None, 1, H * dv), lambda d: (d, 0, VA // (H * dv))),
                  pl.BlockSpec((None, 1, H * dv), lambda d: (d, 0, RA // (H * dv))),
                  pl.BlockSpec((None, 1, LANE), lambda d: (d, 0, smc)),
                  pl.BlockSpec((LANE, H * dk), lambda d: (0, 0)),
                  pl.BlockSpec((1, H * dk), lambda d: (0, 0)),
                  pl.BlockSpec((1, H * dv), lambda d: (0, 0)),
                  pl.BlockSpec((None, H, dk, dv), lambda d: (d, 0, 0, 0))],
        out_specs=[pl.BlockSpec((DB, H * dv), lambda d: (0, 0)),
                   pl.BlockSpec((None, H, dk, dv), lambda d: (d, 0, 0, 0))],
        out_shape=[jax.ShapeDtypeStruct((R, H * dv), BF16),
                   jax.ShapeDtypeStruct((DB, H, dk, dv), F32)],
        scratch_shapes=[pltpu.VMEM((DB, H * dv), F32)],
        input_output_aliases={0: 0},
        compiler_params=_cparams(("arbitrary",)),
        name="gla_sample",
    )(a_out, P_rows, P_rows, P_rows, P_rows, P_rows, wau_pad, ba2, gain2, state_gla[0])

    TQ = LANE
    nqb = TP // TQ
    n_cls = -(-nqb // QB_PER_CLASS)
    widths = tuple(min((c + 1) * QB_PER_CLASS * TQ, TP) for c in range(n_cls))
    o_b = jnp.zeros((R, ATT_HEADS * hd), BF16)
    for c, width in enumerate(widths):
        q_lo = c * QB_PER_CLASS
        n_q = min(QB_PER_CLASS, nqb - q_lo)
        row_blk = lambda b, q, q_lo=q_lo: b * nqb + q_lo + q
        specs = [pl.BlockSpec((TQ, IDX_HEADS * IDX_DIM), lambda b, q, r_=row_blk: (r_(b, q), QI // (IDX_HEADS * IDX_DIM))),
                 pl.BlockSpec((TQ, LANE), lambda b, q, r_=row_blk: (r_(b, q), smc)),
                 pl.BlockSpec((TP, LANE), lambda b, q: (b, smc)),
                 pl.BlockSpec((TQ, ATT_HEADS * hd), lambda b, q, r_=row_blk: (r_(b, q), QB // (ATT_HEADS * hd))),
                 pl.BlockSpec((TP, kvw), lambda b, q: (b, KB // kvw)),
                 pl.BlockSpec((TP, kvw), lambda b, q: (b, VB // kvw))]
        o_b = pl.pallas_call(
            functools.partial(_dsa_prompt_kernel, k_sel=k_sel_p, hd=hd, width=width, q_lo=q_lo),
            grid=(B, n_q),
            in_specs=[pl.BlockSpec(memory_space=pl.ANY)] + specs,
            out_specs=pl.BlockSpec((TQ, ATT_HEADS * hd), lambda b, q, r_=row_blk: (r_(b, q), 0)),
            out_shape=jax.ShapeDtypeStruct((R, ATT_HEADS * hd), BF16),
            scratch_shapes=[pltpu.VMEM((width, LANE), BF16), pltpu.VMEM((width, LANE), BF16),
                            pltpu.VMEM((width, kvw), BF16), pltpu.VMEM((width, kvw), BF16),
                            pltpu.VMEM((TQ, width), F32), pltpu.VMEM((TQ, width), F32)],
            input_output_aliases={0: 0},
            compiler_params=_cparams(("parallel", "arbitrary")),
            name=f"dsa_prompt_w{width}",
        )(o_b, P, P, P, P, P, P)

    qi_s = p_s[:, QI:QI + IDX_HEADS * IDX_DIM].reshape(DB, IDX_HEADS, IDX_DIM)
    wi_s = p_s[:, SM + WI_OFF:SM + WI_OFF + IDX_HEADS].reshape(DB, IDX_HEADS, 1)
    ki_s = p_s[:, SM:SM + IDX_DIM].reshape(DB, 1, IDX_DIM)
    sc, scn = pl.pallas_call(
        functools.partial(_dsa_sample_score_kernel, n_pages=n_pages),
        grid_spec=pltpu.PrefetchScalarGridSpec(
            num_scalar_prefetch=1,
            grid=(DB,),
            in_specs=[pl.BlockSpec(memory_space=pl.ANY),
                      pl.BlockSpec((None, IDX_HEADS, IDX_DIM), lambda d, pt: (d, 0, 0)),
                      pl.BlockSpec((None, IDX_HEADS, 1), lambda d, pt: (d, 0, 0)),
                      pl.BlockSpec((None, 1, IDX_DIM), lambda d, pt: (d, 0, 0))],
            out_specs=[pl.BlockSpec((None, n_pages, PAGE_SIZE), lambda d, pt: (d, 0, 0)),
                       pl.BlockSpec((None, 1, LANE), lambda d, pt: (d, 0, 0))],
            scratch_shapes=[pltpu.VMEM((2, n_pages, IDX_DIM, PAGE_SIZE), F32),
                            pltpu.SemaphoreType.DMA((2,))]),
        out_shape=[jax.ShapeDtypeStruct((DB, n_pages, PAGE_SIZE), F32),
                   jax.ShapeDtypeStruct((DB, 1, LANE), F32)],
        compiler_params=_cparams(("arbitrary",)),
        name="dsa_sample_score",
    )(page_table, jnp.swapaxes(cache_idx_k[0], 1, 2), qi_s, wi_s, ki_s)

    hl_np = np.zeros((past + LANE, LANE), np.float32)
    hl_np[:, 0] = np.arange(past + LANE) // PAGE_SIZE
    hl_np[:, 1] = np.arange(past + LANE) % PAGE_SIZE
    hl = jnp.asarray(hl_np, dtype=BF16)
    sel = pl.pallas_call(
        functools.partial(_dsa_sample_select_kernel, k_sel=k_sel_s, chunk=2048),
        out_shape=jax.ShapeDtypeStruct((DB, k_sel_s, LANE), I32),
        scratch_shapes=[pltpu.VMEM((DB, 1, past + LANE), F32), pltpu.VMEM((DB, past + LANE), F32)],
        compiler_params=pltpu.CompilerParams(vmem_limit_bytes=VMEM_LIMIT),
        name="dsa_sample_select",
    )(sc.reshape(DB, past), scn.reshape(DB, LANE), hl, page_table.astype(F32).reshape(DB, 1, n_pages))

    rows_s = sel[:, :, 2]
    q_s = p_s[:, QB:QB + ATT_HEADS * hd].reshape(DB, ATT_HEADS, hd)
    kn_s = p_s[:, KB:KB + kvw]
    vn_s = p_s[:, VB:VB + kvw]
    o_b = pl.pallas_call(
        functools.partial(_dsa_sample_attend_kernel, k_sel=k_sel_s, n_pages=n_pages, hd=hd),
        grid_spec=pltpu.PrefetchScalarGridSpec(
            num_scalar_prefetch=1,
            grid=(DB,),
            in_specs=[pl.BlockSpec(memory_space=pl.ANY),
                      pl.BlockSpec(memory_space=pl.ANY),
                      pl.BlockSpec(memory_space=pl.ANY),
                      pl.BlockSpec((None, k_sel_s, LANE), lambda d, s_: (d, 0, 0)),
                      pl.BlockSpec((None, ATT_HEADS, hd), lambda d, s_: (d, 0, 0)),
                      pl.BlockSpec((None, 1, kvw), lambda d, s_: (d, 0, KB // kvw)),
                      pl.BlockSpec((None, 1, kvw), lambda d, s_: (d, 0, VB // kvw))],
            out_specs=pl.BlockSpec((DB, ATT_HEADS * hd), lambda d, s_: (0, 0)),
            scratch_shapes=[pltpu.VMEM((2, k_sel_s, ATT_KV_HEADS, hd), F32),
                            pltpu.VMEM((2, k_sel_s, ATT_KV_HEADS, hd), F32),
                            pltpu.VMEM((DB, ATT_HEADS, hd), F32), pltpu.SemaphoreType.DMA((2, 2))]),
        out_shape=jax.ShapeDtypeStruct((R, ATT_HEADS * hd), BF16),
        input_output_aliases={1: 0},
        compiler_params=_cparams(("arbitrary",)),
        name="dsa_sample_attend",
    )(rows_s, o_b, cache_k[0].reshape(n_pool * PAGE_SIZE, ATT_KV_HEADS, hd),
      cache_v[0].reshape(n_pool * PAGE_SIZE, ATT_KV_HEADS, hd), sel, q_s, P_rows, P_rows)

    mix = pl.pallas_call(
        _mix_kernel,
        grid=(R // tm1, D // COL_TILE),
        in_specs=[pl.BlockSpec((tm1, H * dv), lambda i, j: (i, 0)),
                  pl.BlockSpec((tm1, ATT_HEADS * hd), lambda i, j: (i, 0)),
                  pl.BlockSpec((H * dv, COL_TILE), lambda i, j: (0, j)),
                  pl.BlockSpec((ATT_HEADS * hd, COL_TILE), lambda i, j: (0, j)),
                  pl.BlockSpec((tm1, COL_TILE), lambda i, j: (i, GA // COL_TILE + j)),
                  pl.BlockSpec((tm1, COL_TILE), lambda i, j: (i, GB // COL_TILE + j))],
        out_specs=pl.BlockSpec((tm1, COL_TILE), lambda i, j: (i, j)),
        out_shape=jax.ShapeDtypeStruct((R, D), BF16),
        compiler_params=_cparams(("parallel", "arbitrary")),
        name="mix",
    )(a_out, o_b, wa_bf, wb_bf, P, P)

    tm2 = TP // 4
    x1, h2 = pl.pallas_call(
        _outproj_kernel,
        grid=(R // tm2,),
        in_specs=[pl.BlockSpec((tm2, D), lambda i: (i, 0)),
                  pl.BlockSpec((tm2, D), lambda i: (i, 0)),
                  pl.BlockSpec((D, D), lambda i: (0, 0)),
                  pl.BlockSpec((1, D), lambda i: (0, 0))],
        out_specs=[pl.BlockSpec((tm2, D), lambda i: (i, 0)),
                   pl.BlockSpec((tm2, D), lambda i: (i, 0))],
        out_shape=[jax.ShapeDtypeStruct((R, D), F32), jax.ShapeDtypeStruct((R, D), BF16)],
        compiler_params=_cparams(("parallel",)),
        name="outproj",
    )(x_all, mix, wo_bf, norm_ffn_g)

    nrb = R // tm2
    nft = dff // COL_TILE
    y_all, a_head, a_tail = pl.pallas_call(
        functools.partial(_ffn_kernel, n_dec=DB),
        grid=(nrb, nft),
        in_specs=[pl.BlockSpec((tm2, D), lambda i, j: (i, 0)),
                  pl.BlockSpec((16, D), lambda i, j: (jnp.maximum(i * (tm2 // 16) - 1, 0), 0)),
                  pl.BlockSpec((tm2, D), lambda i, j: (i, 0)),
                  pl.BlockSpec((D, COL_TILE), lambda i, j: (0, j)),
                  pl.BlockSpec((D, COL_TILE), lambda i, j: (0, nft + j)),
                  pl.BlockSpec((COL_TILE, D), lambda i, j: (j, 0)),
                  pl.BlockSpec((CONV_W, COL_TILE), lambda i, j: (0, j)),
                  pl.BlockSpec((1, COL_TILE), lambda i, j: (0, j)),
                  pl.BlockSpec((DB, COL_TILE), lambda i, j: (0, j)),
                  pl.BlockSpec((DB, COL_TILE), lambda i, j: (0, j)),
                  pl.BlockSpec((1, D), lambda i, j: (0, 0))],
        out_specs=[pl.BlockSpec((tm2, D), lambda i, j: (i, 0)),
                   pl.BlockSpec((DB, COL_TILE), lambda i, j: (i, j)),
                   pl.BlockSpec((8, COL_TILE), lambda i, j: (i, j))],
        out_shape=[jax.ShapeDtypeStruct((R, D), F32),
                   jax.ShapeDtypeStruct((nrb * DB, dff), F32),
                   jax.ShapeDtypeStruct((nrb * 8, dff), F32)],
        scratch_shapes=[pltpu.VMEM((tm2, D), F32)],
        compiler_params=_cparams(("parallel", "arbitrary")),
        name="ffn",
    )(h2, h2, x1, wup_bf, wup_bf, wdn_bf, conv_w[0], conv_b, state_conv[0, :, 0], state_conv[0, :, 1],
      norm_final_g.reshape(1, D))

    y3 = y_all.reshape(B, TP, D)
    y_prompt = y3[:, FRONT:]
    y_sample = y_all[:DB].reshape(DB, 1, D)
    P3 = P.reshape(B, TP, NW)
    new_k_p = P3[:, ROW0:, KB:KB + kvw].reshape(1, B, T, ATT_KV_HEADS, hd)
    new_v_p = P3[:, ROW0:, VB:VB + kvw].reshape(1, B, T, ATT_KV_HEADS, hd)
    new_ki_p = P3[:, ROW0:, SM:SM + IDX_DIM].reshape(1, B, T, IDX_DIM)
    new_gla_p = s_fin[None]
    tails = a_tail.reshape(B, nrb // B, 8, dff)[:, -1, 8 - (CONV_W - 1):, :]
    new_conv_p = tails[None]
    new_k_s = kn_s.reshape(1, DB, 1, ATT_KV_HEADS, hd)
    new_v_s = vn_s.reshape(1, DB, 1, ATT_KV_HEADS, hd)
    new_ki_s = ki_s.reshape(1, DB, 1, IDX_DIM)
    new_gla_s = s_new_s[None]
    new_conv_s = jnp.stack([state_conv[0, :, 1], a_head[:DB]], axis=1)[None]
    return (y_prompt, y_sample, new_k_p, new_v_p, new_ki_p, new_gla_p, new_conv_p,
            new_k_s, new_v_s, new_ki_s, new_gla_s, new_conv_s)
```
